```python
import jax, jax.numpy as jnp
from jax import lax
import numpy as np

D_MODEL = 2048
BATCH = 4
SEQ = 4096
DEPTH = 2

MIX_WIDTH = D_MODEL
FOURIER_WIDTH = MIX_WIDTH // 2
POOL_WIDTH = MIX_WIDTH - FOURIER_WIDTH
N_FOURIER_HEADS = 4
FOURIER_HEAD_DIM = FOURIER_WIDTH // N_FOURIER_HEADS
POOL_WINDOWS = (2, 4, 8, 16)
N_POOL_GROUPS = len(POOL_WINDOWS)
POOL_GROUP_DIM = POOL_WIDTH // N_POOL_GROUPS
N_EXPERTS = 16
CAPACITY_FACTOR = 2
D_EXPERT = D_MODEL
RMS_EPS = 1e-6

kernel_name = "fourier_pool_hybrid_ec_moe_encoder"


def rms_norm(x, g):
    xf = x.astype(jnp.float32)
    inv = lax.rsqrt(jnp.mean(xf * xf, axis=-1, keepdims=True) + RMS_EPS)
    return (xf * inv).astype(x.dtype) * g


def centred_pool_minus_identity(u, window):
    b, l, c = u.shape
    cs = jnp.concatenate([jnp.zeros((b, 1, c), u.dtype), jnp.cumsum(u, axis=1)], axis=1)
    t = jnp.arange(l)
    lo = jnp.clip(t - window // 2, 0, l)
    hi = jnp.clip(t + window - window // 2, 0, l)
    win_sum = jnp.take(cs, hi, axis=1) - jnp.take(cs, lo, axis=1)
    count = (hi - lo).astype(u.dtype)[None, :, None]
    return win_sum / count - u


def token_mixer(h, w_in, w_fourier, w_pool, pool_scale, w_out):
    b, l, _ = h.shape
    u = h @ w_in
    ua = u[..., :FOURIER_WIDTH].reshape(b, l, N_FOURIER_HEADS, FOURIER_HEAD_DIM)
    ub = u[..., FOURIER_WIDTH:].reshape(b, l, N_POOL_GROUPS, POOL_GROUP_DIM)
    fa = jnp.fft.fftn(ua.astype(jnp.float32), axes=(1, 3), norm="ortho").real.astype(h.dtype)
    ya = jnp.einsum('blgc,gcd->blgd', fa, w_fourier)
    pooled = jnp.stack(
        [centred_pool_minus_identity(ub[:, :, gi].astype(jnp.float32), w)
         for gi, w in enumerate(POOL_WINDOWS)], axis=2).astype(h.dtype)
    yb = jnp.einsum('blgc,gcd->blgd', pooled, w_pool) * pool_scale
    y = jnp.concatenate([ya.reshape(b, l, FOURIER_WIDTH), yb.reshape(b, l, POOL_WIDTH)], axis=-1)
    return y @ w_out


def expert_choice_ffn(h, w_router, w_gate, w_up, w_down):
    b, l, d = h.shape
    cap = CAPACITY_FACTOR * l // N_EXPERTS
    probs = jax.nn.softmax((h @ w_router).astype(jnp.float32), axis=-1)
    gates, idx = lax.top_k(jnp.swapaxes(probs, 1, 2), cap)
    flat = (idx + (jnp.arange(b) * l)[:, None, None]).reshape(-1)
    xs = jnp.take(h.reshape(b * l, d), flat, axis=0).reshape(b, N_EXPERTS, cap, d)
    hid = jax.nn.silu(jnp.einsum('becd,edf->becf', xs, w_gate)) * jnp.einsum('becd,edf->becf', xs, w_up)
    ys = jnp.einsum('becf,efd->becd', hid, w_down) * gates[..., None].astype(h.dtype)
    out = jnp.zeros((b * l, d), h.dtype).at[flat].add(ys.reshape(-1, d))
    return out.reshape(b, l, d)


def setup_inputs(seed: int = 0) -> dict:
    key = jax.random.key(seed)
    ks = jax.random.split(key, 14)
    f32 = jnp.float32
    def nrm(k, shape, fan_in):
        return jax.random.normal(k, shape, f32) * (fan_in ** -0.5)
    x = jax.random.normal(ks[0], (BATCH, SEQ, D_MODEL), f32)
    norm1_g = 1.0 + 0.02 * jax.random.normal(ks[1], (DEPTH, D_MODEL), f32)
    w_in = nrm(ks[2], (DEPTH, D_MODEL, MIX_WIDTH), D_MODEL)
    w_fourier = nrm(ks[3], (DEPTH, N_FOURIER_HEADS, FOURIER_HEAD_DIM, FOURIER_HEAD_DIM), FOURIER_HEAD_DIM)
    w_pool = nrm(ks[4], (DEPTH, N_POOL_GROUPS, POOL_GROUP_DIM, POOL_GROUP_DIM), POOL_GROUP_DIM)
    pool_scale = 1.0 + 0.02 * jax.random.normal(ks[5], (DEPTH, N_POOL_GROUPS, POOL_GROUP_DIM), f32)
    w_out = nrm(ks[6], (DEPTH, MIX_WIDTH, D_MODEL), MIX_WIDTH)
    norm2_g = 1.0 + 0.02 * jax.random.normal(ks[7], (DEPTH, D_MODEL), f32)
    w_router = nrm(ks[8], (DEPTH, D_MODEL, N_EXPERTS), D_MODEL)
    w_gate = nrm(ks[9], (DEPTH, N_EXPERTS, D_MODEL, D_EXPERT), D_MODEL)
    w_up = nrm(ks[10], (DEPTH, N_EXPERTS, D_MODEL, D_EXPERT), D_MODEL)
    w_down = nrm(ks[11], (DEPTH, N_EXPERTS, D_EXPERT, D_MODEL), D_EXPERT)
    final_g = 1.0 + 0.02 * jax.random.normal(ks[12], (D_MODEL,), f32)
    return {"x": x, "norm1_g": norm1_g, "w_in": w_in, "w_fourier": w_fourier,
            "w_pool": w_pool, "pool_scale": pool_scale, "w_out": w_out,
            "norm2_g": norm2_g, "w_router": w_router, "w_gate": w_gate,
            "w_up": w_up, "w_down": w_down, "final_g": final_g}


def reference(x, norm1_g, w_in, w_fourier, w_pool, pool_scale, w_out,
              norm2_g, w_router, w_gate, w_up, w_down, final_g):
    for layer in range(DEPTH):
        h = rms_norm(x, norm1_g[layer])
        x = x + token_mixer(h, w_in[layer], w_fourier[layer], w_pool[layer],
                            pool_scale[layer], w_out[layer])
        h = rms_norm(x, norm2_g[layer])
        x = x + expert_choice_ffn(h, w_router[layer], w_gate[layer], w_up[layer], w_down[layer])
    return rms_norm(x, final_g)
```

```python
import functools

import numpy as np
import jax
import jax.numpy as jnp
from jax import lax
from jax.experimental import pallas as pl
from jax.experimental.pallas import tpu as pltpu

D_MODEL = 2048
BATCH = 4
SEQ = 4096
DEPTH = 2
N_HEADS = 4
HEAD_DIM = 256
POOL_WINDOWS = (2, 4, 8, 16)
N_GROUPS = 4
GROUP_DIM = 256
FOURIER_WIDTH = N_HEADS * HEAD_DIM
POOL_WIDTH = N_GROUPS * GROUP_DIM
N_EXPERTS = 16
CAP = 2 * SEQ // N_EXPERTS
SLOTS = BATCH * CAP
D_EXPERT = D_MODEL
RMS_EPS = 1e-6
ROWS = BATCH * SEQ

F32 = jnp.float32
BF16 = jnp.bfloat16
HIGHEST = lax.Precision.HIGHEST

VMEM_LIMIT = 54 * 1024 * 1024


def _cp(sem, vmem=VMEM_LIMIT):
    return pltpu.CompilerParams(dimension_semantics=sem, vmem_limit_bytes=vmem)


def _dot(a, b):
    return jnp.dot(a, b, preferred_element_type=F32)


N_FOLD = 2 * N_HEADS + N_GROUPS
Z_WIDTH = N_FOLD * HEAD_DIM


def _fold_kernel(win_ref, lt_ref, r_ref, sc_ref, o_ref):
    t = jnp.dot(lt_ref[0], r_ref[0, 0], precision=HIGHEST, preferred_element_type=F32) * sc_ref[0, 0]
    o_ref[0] = jnp.dot(win_ref[0], t, precision=HIGHEST, preferred_element_type=F32).astype(BF16)


def _fold_weights(w_in, w_fourier, w_pool, pool_scale):
    c = np.arange(HEAD_DIM)
    ang = 2.0 * np.pi * ((c[:, None] * c[None, :]) % HEAD_DIM) / HEAD_DIM
    ltab = jnp.asarray(np.stack([np.cos(ang) / np.sqrt(HEAD_DIM), np.sin(ang) / np.sqrt(HEAD_DIM),
                                 np.eye(HEAD_DIM)]).astype(np.float32))
    right = jnp.concatenate([w_fourier, w_pool], axis=1)
    scale = jnp.concatenate([jnp.ones((DEPTH, 2 * N_HEADS, HEAD_DIM), F32), pool_scale], axis=1)
    scale = scale.reshape(DEPTH, N_FOLD, 1, HEAD_DIM)

    def src_blk(j):
        return jnp.where(j < N_HEADS, j, j - N_HEADS)

    def tab_blk(j):
        return jnp.where(j < N_HEADS, 0, jnp.where(j < 2 * N_HEADS, 1, 2))

    return pl.pallas_call(
        _fold_kernel,
        out_shape=jax.ShapeDtypeStruct((DEPTH, D_MODEL, Z_WIDTH), BF16),
        grid=(DEPTH, N_FOLD),
        in_specs=[
            pl.BlockSpec((1, D_MODEL, HEAD_DIM), lambda l, j: (l, 0, src_blk(j))),
            pl.BlockSpec((1, HEAD_DIM, HEAD_DIM), lambda l, j: (tab_blk(j), 0, 0)),
            pl.BlockSpec((1, 1, HEAD_DIM, HEAD_DIM), lambda l, j: (l, src_blk(j), 0, 0)),
            pl.BlockSpec((1, 1, 1, HEAD_DIM), lambda l, j: (l, j, 0, 0)),
        ],
        out_specs=pl.BlockSpec((1, D_MODEL, HEAD_DIM), lambda l, j: (l, 0, j)),
        compiler_params=_cp(("parallel", "parallel")),
        name="fold_weights",
    )(w_in, ltab, right, scale)


TM_IN = 512


def _norm_mm_kernel(x_ref, g_ref, w_ref, o_ref):
    x = x_ref[...]
    inv = lax.rsqrt(jnp.mean(x * x, axis=-1, keepdims=True) + RMS_EPS)
    h = ((x * inv) * g_ref[...]).astype(BF16)
    o_ref[...] = _dot(h, w_ref[0]).astype(o_ref.dtype)


def _norm_mm(x2d, gamma, wcat, layer):
    return pl.pallas_call(
        _norm_mm_kernel,
        out_shape=jax.ShapeDtypeStruct((ROWS, Z_WIDTH), BF16),
        grid=(ROWS // TM_IN,),
        in_specs=[
            pl.BlockSpec((TM_IN, D_MODEL), lambda i: (i, 0)),
            pl.BlockSpec((1, D_MODEL), lambda i: (0, 0)),
            pl.BlockSpec((1, D_MODEL, Z_WIDTH), lambda i: (layer, 0, 0)),
        ],
        out_specs=pl.BlockSpec((TM_IN, Z_WIDTH), lambda i: (i, 0)),
        compiler_params=_cp(("parallel",)),
        name="norm_in_proj",
    )(x2d, gamma.reshape(1, D_MODEL), wcat)


TM_DFT = 512
TN_DFT = 512


def _dft_tables():
    k = jnp.arange(SEQ, dtype=jnp.int32)
    ang = ((k[:, None] * k[None, :]) % SEQ).astype(F32) * (2.0 * np.pi / SEQ)
    s = 1.0 / np.sqrt(SEQ)
    return (jnp.cos(ang) * s).astype(BF16), (jnp.sin(ang) * s).astype(BF16)


def _dft_kernel(c_ref, s_ref, p_ref, q_ref, o_ref):
    o_ref[0] = (_dot(c_ref[...], p_ref[0]) - _dot(s_ref[...], q_ref[0])).astype(o_ref.dtype)


def _dft(ctab, stab, z3):
    nq = FOURIER_WIDTH // TN_DFT
    return pl.pallas_call(
        _dft_kernel,
        out_shape=jax.ShapeDtypeStruct((BATCH, SEQ, FOURIER_WIDTH), BF16),
        grid=(SEQ // TM_DFT, BATCH, nq),
        in_specs=[
            pl.BlockSpec((TM_DFT, SEQ), lambda i, b, n: (i, 0)),
            pl.BlockSpec((TM_DFT, SEQ), lambda i, b, n: (i, 0)),
            pl.BlockSpec((1, SEQ, TN_DFT), lambda i, b, n: (b, 0, n)),
            pl.BlockSpec((1, SEQ, TN_DFT), lambda i, b, n: (b, 0, nq + n)),
        ],
        out_specs=pl.BlockSpec((1, TM_DFT, TN_DFT), lambda i, b, n: (b, i, n)),
        compiler_params=_cp(("parallel", "parallel", "parallel")),
        name="position_dft",
    )(ctab, stab, z3, z3)


POOL_PAD = 8


def _pool_kernel(v_ref, o_ref):
    g = pl.program_id(1)
    x = v_ref[0].astype(F32)
    zeros = jnp.zeros((POOL_PAD, GROUP_DIM), F32)
    xp = jnp.concatenate([zeros, x, zeros], axis=0)
    t = lax.broadcasted_iota(jnp.int32, (SEQ, 1), 0)
    for gi, w in enumerate(POOL_WINDOWS):
        @pl.when(g == gi)
        def _(w=w):
            a, span = xp, 1
            while span < w:
                n = a.shape[0] - span
                a = a[:n] + a[span:span + n]
                span *= 2
            start = POOL_PAD - w // 2
            win = a[start:start + SEQ]
            cnt = (jnp.minimum(t + (w - w // 2), SEQ) - jnp.maximum(t - w // 2, 0)).astype(F32)
            o_ref[0] = (win / cnt - x).astype(o_ref.dtype)


def _pool(z3):
    first = (2 * FOURIER_WIDTH) // GROUP_DIM
    return pl.pallas_call(
        _pool_kernel,
        out_shape=jax.ShapeDtypeStruct((BATCH, SEQ, POOL_WIDTH), BF16),
        grid=(BATCH, N_GROUPS),
        in_specs=[pl.BlockSpec((1, SEQ, GROUP_DIM), lambda b, g: (b, 0, first + g))],
        out_specs=pl.BlockSpec((1, SEQ, GROUP_DIM), lambda b, g: (b, 0, g)),
        compiler_params=_cp(("parallel", "parallel")),
        name="pool_minus_identity",
    )(z3)


TM_OUT = 512


def _out_proj_kernel(ya_ref, yb_ref, wa_ref, wb_ref, x_ref, g_ref, wrh_ref, wrl_ref, o_ref, h_ref, lg_ref):
    x1 = x_ref[...] + _dot(ya_ref[...], wa_ref[0]) + _dot(yb_ref[...], wb_ref[0])
    o_ref[...] = x1
    inv = lax.rsqrt(jnp.mean(x1 * x1, axis=-1, keepdims=True) + RMS_EPS)
    h = (x1 * inv) * g_ref[...]
    h_ref[...] = h
    h_hi = h.astype(BF16)
    h_lo = (h - h_hi.astype(F32)).astype(BF16)
    lg_ref[...] = _dot(h_hi, wrh_ref[...]) + (_dot(h_lo, wrh_ref[...]) + _dot(h_hi, wrl_ref[...]))


def _out_proj(ya2d, yb2d, w_out_bf, x2d, gamma, wr_hi, wr_lo, layer):
    row_blk = lambda i: (i, 0)
    fixed = lambda i: (0, 0)
    return pl.pallas_call(
        _out_proj_kernel,
        out_shape=(
            jax.ShapeDtypeStruct((ROWS, D_MODEL), F32),
            jax.ShapeDtypeStruct((ROWS, D_MODEL), F32),
            jax.ShapeDtypeStruct((ROWS, N_EXPERTS), F32),
        ),
        grid=(ROWS // TM_OUT,),
        in_specs=[
            pl.BlockSpec((TM_OUT, FOURIER_WIDTH), row_blk),
            pl.BlockSpec((TM_OUT, POOL_WIDTH), row_blk),
            pl.BlockSpec((1, FOURIER_WIDTH, D_MODEL), lambda i: (layer, 0, 0)),
            pl.BlockSpec((1, POOL_WIDTH, D_MODEL), lambda i: (layer, 1, 0)),
            pl.BlockSpec((TM_OUT, D_MODEL), row_blk),
            pl.BlockSpec((1, D_MODEL), fixed),
            pl.BlockSpec((D_MODEL, N_EXPERTS), fixed),
            pl.BlockSpec((D_MODEL, N_EXPERTS), fixed),
        ],
        out_specs=(
            pl.BlockSpec((TM_OUT, D_MODEL), row_blk),
            pl.BlockSpec((TM_OUT, D_MODEL), row_blk),
            pl.BlockSpec((TM_OUT, N_EXPERTS), row_blk),
        ),
        compiler_params=_cp(("parallel",)),
        name="out_proj_residual",
    )(ya2d, yb2d, w_out_bf, w_out_bf, x2d, gamma.reshape(1, D_MODEL), wr_hi, wr_lo)


LANE = 128
NGRP = SEQ // LANE
GRP_SHIFT = 5
assert 1 << GRP_SHIFT == NGRP
ER = N_EXPERTS * NGRP
TINY = float(np.finfo(np.float32).tiny)
N_BISECT = 40
N_MCOL = 8


def _select_kernel(lg_ref, idx_ref, gate_ref, enc_ref, m_sc, ci_sc, cg_sc):
    lg = lg_ref[0]
    ex = jnp.exp(lg - jnp.max(lg, axis=0, keepdims=True))
    p = ex / jnp.sum(ex, axis=0, keepdims=True)

    def total(v):
        return jnp.sum(jnp.sum(v, axis=2, keepdims=True), axis=1, keepdims=True)

    def bisect(_, lohi):
        lo, hi = lohi
        mid = jnp.sqrt(jnp.maximum(lo, TINY)) * jnp.sqrt(hi)
        ok = total(jnp.where(p >= mid, 1.0, 0.0)) >= CAP
        return jnp.where(ok, mid, lo), jnp.where(ok, hi, mid)

    lo0 = jnp.zeros((N_EXPERTS, 1, 1), F32)
    hi0 = jnp.full((N_EXPERTS, 1, 1), 2.0, F32)
    _, hi = lax.fori_loop(0, N_BISECT, bisect, (lo0, hi0))
    below = jnp.where(p < hi, p, -1.0)
    thr = jnp.max(jnp.max(below, axis=2, keepdims=True), axis=1, keepdims=True)
    gt = p > thr
    eq = p == thr
    need = CAP - total(jnp.where(gt, 1.0, 0.0))

    r_i = lax.broadcasted_iota(jnp.int32, (ER, ER), 0)
    c_i = lax.broadcasted_iota(jnp.int32, (ER, ER), 1)
    same_expert = (c_i >> GRP_SHIFT) == (r_i >> GRP_SHIFT)
    rows_before = jnp.where(jnp.logical_and(c_i < r_i, same_expert), 1.0, 0.0).astype(BF16)
    j_i = lax.broadcasted_iota(jnp.int32, (LANE, LANE), 0)
    l_i = lax.broadcasted_iota(jnp.int32, (LANE, LANE), 1)
    lanes_upto = jnp.where(j_i <= l_i, 1.0, 0.0).astype(BF16)
    ones = jnp.ones((LANE, LANE), BF16)

    def prefix(mask_b):
        within = _dot(mask_b, lanes_upto)
        rowtot = _dot(mask_b, ones)
        rowoff = _dot(rows_before, rowtot.astype(BF16))
        return within, rowoff

    eq_f = jnp.where(eq, 1.0, 0.0)
    w_eq, ro_eq = prefix(eq_f.astype(BF16).reshape(ER, LANE))
    eq_before = (w_eq + ro_eq).reshape(N_EXPERTS, NGRP, LANE) - eq_f
    sel = jnp.logical_or(gt, jnp.logical_and(eq, eq_before < need))
    sel_b = jnp.where(sel, 1.0, 0.0).astype(BF16).reshape(ER, LANE)
    within, rowoff = prefix(sel_b)
    count = within + rowoff
    enc_ref[0] = jnp.where(sel_b > 0, count, -count).astype(jnp.int32)

    half = jnp.floor(rowoff * 0.5)
    p2d = p.reshape(ER, LANE)
    p_1 = p2d.astype(BF16)
    rem = p2d - p_1.astype(F32)
    p_2 = rem.astype(BF16)
    p_3 = (rem - p_2.astype(F32)).astype(BF16)
    grp = (lax.broadcasted_iota(jnp.int32, (ER, LANE), 0) & (NGRP - 1)).astype(F32)
    blocks = [within.astype(BF16), grp.astype(BF16), half.astype(BF16), (rowoff - 2.0 * half).astype(BF16),
              p_1, p_2, p_3, sel_b]
    for k, blk in enumerate(blocks):
        m_sc[:, k * LANE:(k + 1) * LANE] = blk
    ci_sc[...] = jnp.zeros_like(ci_sc)
    cg_sc[...] = jnp.zeros_like(cg_sc)

    s_col = lax.broadcasted_iota(jnp.int32, (CAP, 1), 0).astype(F32)
    lane = lax.broadcasted_iota(jnp.int32, (CAP, LANE), 1)
    lane_f = lane.astype(F32)
    ones8 = jnp.ones((8, LANE), BF16)
    g_r = lax.broadcasted_iota(jnp.int32, (NGRP, NGRP), 0)
    g_c = lax.broadcasted_iota(jnp.int32, (NGRP, NGRP), 1)
    groups_before = jnp.where(g_r < g_c, 1.0, 0.0).astype(BF16)

    def per_expert(e, carry):
        r0 = pl.multiple_of(e * NGRP, NGRP)
        table = m_sc[pl.ds(r0, NGRP), :]
        sel_e = table[:, 7 * LANE:]
        rt = lax.dot_general(ones8, sel_e, (((1,), (1,)), ((), ())), preferred_element_type=F32)
        ro = _dot(rt.astype(BF16), groups_before)
        start = ro[0:1]
        stop = start + rt[0:1]
        in_grp = jnp.logical_and(start <= s_col, s_col < stop)
        got = _dot(jnp.where(in_grp, 1.0, 0.0).astype(BF16), table[:, :7 * LANE])
        s_loc = s_col - (2.0 * got[:, 2 * LANE:3 * LANE] + got[:, 3 * LANE:4 * LANE])
        off = _dot(jnp.where(got[:, :LANE] <= s_loc, 1.0, 0.0).astype(BF16), ones)
        tok = got[:, LANE:2 * LANE] * LANE + off
        hit = lane_f == off
        gate = jnp.zeros((CAP, LANE), F32)
        for k in (4, 5, 6):
            gate = gate + _dot(jnp.where(hit, got[:, k * LANE:(k + 1) * LANE], 0.0).astype(BF16), ones)
        ci_sc[...] = jnp.where(lane == e, tok, ci_sc[...])
        cg_sc[...] = jnp.where(lane == e, gate, cg_sc[...])
        return carry

    lax.fori_loop(0, N_EXPERTS, per_expert, 0)
    idx_ref[0] = ci_sc[...].T[:N_EXPERTS].astype(jnp.int32)
    gate_ref[0] = cg_sc[...].T[:N_EXPERTS]


def _select(lg4):
    return pl.pallas_call(
        _select_kernel,
        out_shape=(
            jax.ShapeDtypeStruct((BATCH, N_EXPERTS, CAP), jnp.int32),
            jax.ShapeDtypeStruct((BATCH, N_EXPERTS, CAP), F32),
            jax.ShapeDtypeStruct((BATCH, ER, LANE), jnp.int32),
        ),
        grid=(BATCH,),
        in_specs=[pl.BlockSpec((1, N_EXPERTS, NGRP, LANE), lambda b: (b, 0, 0, 0))],
        out_specs=(
            pl.BlockSpec((1, N_EXPERTS, CAP), lambda b: (b, 0, 0)),
            pl.BlockSpec((1, N_EXPERTS, CAP), lambda b: (b, 0, 0)),
            pl.BlockSpec((1, ER, LANE), lambda b: (b, 0, 0)),
        ),
        scratch_shapes=[
            pltpu.VMEM((ER, N_MCOL * LANE), BF16),
            pltpu.VMEM((CAP, LANE), F32),
            pltpu.VMEM((CAP, LANE), F32),
        ],
        compiler_params=_cp(("parallel",)),
        name="select_topc",
    )(lg4)


TF = 256
TD = 256
NF = D_EXPERT // TF
ND = D_MODEL // TD


def _ffn_kernel(idx_ref, gate_ref, h_hbm, wg_ref, wu_ref, wd_ref, o_ref, stage, xs, hid, sem):
    j = pl.program_id(1)

    @pl.when(j == 0)
    def _gather():
        for b in range(BATCH):
            def issue(s, c, b=b):
                row = idx_ref[0, 0, b * CAP + s]
                pltpu.make_async_copy(h_hbm.at[pl.ds(row, 1), :], stage.at[pl.ds(s, 1), :], sem).start()
                return c

            lax.fori_loop(0, CAP, issue, 0, unroll=8)
            pltpu.make_async_copy(h_hbm.at[pl.ds(0, CAP), :], stage, sem).wait()
            xs[b * CAP:(b + 1) * CAP, :] = stage[...].astype(BF16)

    @pl.when(j < NF)
    def _up():
        x = xs[...]
        g = _dot(x, wg_ref[...].astype(BF16))
        u = _dot(x, wu_ref[...].astype(BF16))
        hv = ((g * jax.nn.sigmoid(g)) * u).astype(BF16)
        for k in range(NF):
            @pl.when(j == k)
            def _(k=k):
                hid[:, k * TF:(k + 1) * TF] = hv

    @pl.when(j >= NF)
    def _down():
        ys = _dot(hid[...], wd_ref[...].astype(BF16))
        o_ref[0] = (ys * gate_ref[0]).astype(o_ref.dtype)


def _ffn(idx_rows, gates_col, h2, w_gate, w_up, w_down, layer):
    up_blk = lambda e, j: (layer, e, 0, jnp.minimum(j, NF - 1))
    dn_blk = lambda e, j: (layer, e, 0, jnp.maximum(j - NF, 0))
    return pl.pallas_call(
        _ffn_kernel,
        out_shape=jax.ShapeDtypeStruct((N_EXPERTS, SLOTS, D_MODEL), BF16),
        grid=(N_EXPERTS, NF + ND),
        in_specs=[
            pl.BlockSpec((1, 1, SLOTS), lambda e, j: (e, 0, 0), memory_space=pltpu.SMEM),
            pl.BlockSpec((1, SLOTS, 1), lambda e, j: (e, 0, 0)),
            pl.BlockSpec(memory_space=pl.ANY),
            pl.BlockSpec((None, None, D_MODEL, TF), up_blk),
            pl.BlockSpec((None, None, D_MODEL, TF), up_blk),
            pl.BlockSpec((None, None, D_EXPERT, TD), dn_blk),
        ],
        out_specs=pl.BlockSpec((1, SLOTS, TD), lambda e, j: (e, 0, jnp.maximum(j - NF, 0))),
        scratch_shapes=[
            pltpu.VMEM((CAP, D_MODEL), F32),
            pltpu.VMEM((SLOTS, D_MODEL), BF16),
            pltpu.VMEM((SLOTS, D_EXPERT), BF16),
            pltpu.SemaphoreType.DMA,
        ],
        compiler_params=_cp(("arbitrary", "arbitrary")),
        name="expert_swiglu",
    )(idx_rows, gates_col, h2, w_gate, w_up, w_down)


TB_C = 256
NB_C = SEQ // TB_C
ROW_ALIGN = 16
ROW_SHIFT = 4
KC = 256
KC_SHIFT = 8
assert 1 << ROW_SHIFT == ROW_ALIGN and 1 << KC_SHIFT == KC
STAGE_ROWS = N_EXPERTS * (TB_C + ROW_ALIGN)
assert STAGE_ROWS % KC == 0


def _combine_kernel(bnd_ref, x_ref, enc_ref, g_ref, y_hbm, o_ref, stage, sem, *, final_norm):
    b = pl.program_id(0)
    tb = pl.program_id(1)

    @pl.when(jnp.logical_and(b == 0, tb == 0))
    def _init():
        stage[...] = jnp.zeros_like(stage)

    lane_e = lax.broadcasted_iota(jnp.int32, (1, N_EXPERTS), 1)
    base_vec = jnp.zeros((1, N_EXPERTS), jnp.int32)
    pos = jnp.int32(0)
    for e in range(N_EXPERTS):
        lo = bnd_ref[b, tb, e]
        hi = bnd_ref[b, tb + 1, e]
        lo_al = lo - (lo & (ROW_ALIGN - 1))
        nch = jnp.where(hi > lo, lax.shift_right_logical(hi - lo_al + (ROW_ALIGN - 1), ROW_SHIFT), 0)

        def issue(c, carry, e=e, lo_al=lo_al, pos=pos):
            src = pl.multiple_of(b * CAP + lo_al + c * ROW_ALIGN, ROW_ALIGN)
            dst = pl.multiple_of(pos + c * ROW_ALIGN, ROW_ALIGN)
            pltpu.make_async_copy(y_hbm.at[e, pl.ds(src, ROW_ALIGN), :],
                                  stage.at[pl.ds(dst, ROW_ALIGN), :], sem).start()
            return carry

        lax.fori_loop(0, nch, issue, 0)
        base_vec = jnp.where(lane_e == e, pos - lo_al, base_vec)
        pos = pos + nch * ROW_ALIGN

    def drain(c, carry):
        pltpu.make_async_copy(y_hbm.at[0, pl.ds(0, ROW_ALIGN), :],
                              stage.at[pl.ds(0, ROW_ALIGN), :], sem).wait()
        return carry

    lax.fori_loop(0, lax.shift_right_logical(pos, ROW_SHIFT), drain, 0)

    enc = enc_ref[0]
    col = jnp.where(enc > 0, enc - 1 + base_vec, -1)
    o_ref[...] = x_ref[...]
    kiota = lax.broadcasted_iota(jnp.int32, (TB_C, KC), 1)

    def accumulate(kc, carry):
        k0 = pl.multiple_of(kc * KC, KC)
        hit = kiota < 0
        for e in range(N_EXPERTS):
            hit = jnp.logical_or(hit, col[:, e:e + 1] - k0 == kiota)
        onehot = jnp.where(hit, 1.0, 0.0).astype(BF16)
        o_ref[...] += _dot(onehot, stage[pl.ds(k0, KC), :])
        return carry

    lax.fori_loop(0, lax.shift_right_logical(pos + (KC - 1), KC_SHIFT), accumulate, 0)

    if final_norm:
        x = o_ref[...]
        inv = lax.rsqrt(jnp.mean(x * x, axis=-1, keepdims=True) + RMS_EPS)
        o_ref[...] = (x * inv) * g_ref[...]


def _combine(bnd, x1_2d, enc_t, gamma, ysg, final_norm):
    grid_spec = pltpu.PrefetchScalarGridSpec(
        num_scalar_prefetch=1,
        grid=(BATCH, NB_C),
        in_specs=[
            pl.BlockSpec((TB_C, D_MODEL), lambda b, t, bnd: (b * NB_C + t, 0)),
            pl.BlockSpec((1, TB_C, N_EXPERTS), lambda b, t, bnd: (b, t, 0)),
            pl.BlockSpec((1, D_MODEL), lambda b, t, bnd: (0, 0)),
            pl.BlockSpec(memory_space=pl.ANY),
        ],
        out_specs=pl.BlockSpec((TB_C, D_MODEL), lambda b, t, bnd: (b * NB_C + t, 0)),
        scratch_shapes=[
            pltpu.VMEM((STAGE_ROWS, D_MODEL), BF16),
            pltpu.SemaphoreType.DMA,
        ],
    )
    return pl.pallas_call(
        functools.partial(_combine_kernel, final_norm=final_norm),
        out_shape=jax.ShapeDtypeStruct((ROWS, D_MODEL), F32),
        grid_spec=grid_spec,
        compiler_params=_cp(("arbitrary", "arbitrary")),
        name="combine_final" if final_norm else "combine",
    )(bnd, x1_2d, enc_t, gamma.reshape(1, D_MODEL), ysg)


def kernel(x, norm1_g, w_in, w_fourier, w_pool, pool_scale, w_out, norm2_g, w_router, w_gate, w_up, w_down, final_g):
    wcat = _fold_weights(w_in, w_fourier, w_pool, pool_scale)
    ctab, stab = _dft_tables()
    w_out_bf = w_out.astype(BF16)
    wr_hi = w_router.astype(BF16)
    wr_lo = (w_router - wr_hi.astype(F32)).astype(BF16)
    batch_base = (jnp.arange(BATCH, dtype=jnp.int32) * SEQ)[:, None, None]

    xc = x.reshape(ROWS, D_MODEL)
    for layer in range(DEPTH):
        z = _norm_mm(xc, norm1_g[layer], wcat, layer)
        z3 = z.reshape(BATCH, SEQ, Z_WIDTH)
        ya = _dft(ctab, stab, z3)
        yb = _pool(z3)
        x1, h2, lg = _out_proj(ya.reshape(ROWS, FOURIER_WIDTH), yb.reshape(ROWS, POOL_WIDTH), w_out_bf, xc,
                               norm2_g[layer], wr_hi[layer], wr_lo[layer], layer)
        lg4 = jnp.transpose(lg.reshape(BATCH, NGRP, LANE, N_EXPERTS), (0, 3, 1, 2))
        idx, gates, enc = _select(lg4)
        enc = enc.reshape(BATCH, N_EXPERTS, SEQ)
        idx_rows = jnp.swapaxes(idx + batch_base, 0, 1).reshape(N_EXPERTS, 1, SLOTS)
        gates_col = jnp.swapaxes(gates, 0, 1).reshape(N_EXPERTS, SLOTS, 1)
        ysg = _ffn(idx_rows, gates_col, h2, w_gate, w_up, w_down, layer)

        counts = jnp.abs(enc)
        ends = counts[:, :, TB_C - 1::TB_C]
        bnd = jnp.concatenate([jnp.zeros((BATCH, N_EXPERTS, 1), jnp.int32), ends], axis=2)
        bnd = jnp.swapaxes(bnd, 1, 2)
        enc_t = jnp.swapaxes(enc, 1, 2)
        last = layer == DEPTH - 1
        xc = _combine(bnd, x1, enc_t, final_g, ysg, last)
    return xc.reshape(BATCH, SEQ, D_MODEL)
```

```python
import functools

import numpy as np
import jax
import jax.numpy as jnp
from jax import lax
from jax.experimental import pallas as pl
from jax.experimental.pallas import tpu as pltpu

D_MODEL = 2048
BATCH = 4
SEQ = 4096
DEPTH = 2
N_HEADS = 4
HEAD_DIM = 256
POOL_WINDOWS = (2, 4, 8, 16)
N_GROUPS = 4
GROUP_DIM = 256
FOURIER_WIDTH = N_HEADS * HEAD_DIM
POOL_WIDTH = N_GROUPS * GROUP_DIM
N_EXPERTS = 16
CAP = 2 * SEQ // N_EXPERTS
SLOTS = BATCH * CAP
D_EXPERT = D_MODEL
RMS_EPS = 1e-6
ROWS = BATCH * SEQ

F32 = jnp.float32
BF16 = jnp.bfloat16
HIGHEST = lax.Precision.HIGHEST

VMEM_LIMIT = 54 * 1024 * 1024


def _cp(sem, vmem=VMEM_LIMIT):
    return pltpu.CompilerParams(dimension_semantics=sem, vmem_limit_bytes=vmem)


def _dot(a, b):
    return jnp.dot(a, b, preferred_element_type=F32)


N_FOLD = 2 * N_HEADS + N_GROUPS
Z_WIDTH = N_FOLD * HEAD_DIM


def _fold_kernel(win_ref, lt_ref, r_ref, sc_ref, o_ref):
    t = jnp.dot(lt_ref[0], r_ref[0, 0], precision=HIGHEST, preferred_element_type=F32) * sc_ref[0, 0]
    o_ref[0] = jnp.dot(win_ref[0], t, precision=HIGHEST, preferred_element_type=F32).astype(BF16)


def _fold_weights(w_in, w_fourier, w_pool, pool_scale):
    c = np.arange(HEAD_DIM)
    ang = 2.0 * np.pi * ((c[:, None] * c[None, :]) % HEAD_DIM) / HEAD_DIM
    ltab = jnp.asarray(np.stack([np.cos(ang) / np.sqrt(HEAD_DIM), np.sin(ang) / np.sqrt(HEAD_DIM),
                                 np.eye(HEAD_DIM)]).astype(np.float32))
    right = jnp.concatenate([w_fourier, w_pool], axis=1)
    scale = jnp.concatenate([jnp.ones((DEPTH, 2 * N_HEADS, HEAD_DIM), F32), pool_scale], axis=1)
    scale = scale.reshape(DEPTH, N_FOLD, 1, HEAD_DIM)

    def src_blk(j):
        return jnp.where(j < N_HEADS, j, j - N_HEADS)

    def tab_blk(j):
        return jnp.where(j < N_HEADS, 0, jnp.where(j < 2 * N_HEADS, 1, 2))

    return pl.pallas_call(
        _fold_kernel,
        out_shape=jax.ShapeDtypeStruct((DEPTH, D_MODEL, Z_WIDTH), BF16),
        grid=(DEPTH, N_FOLD),
        in_specs=[
            pl.BlockSpec((1, D_MODEL, HEAD_DIM), lambda l, j: (l, 0, src_blk(j))),
            pl.BlockSpec((1, HEAD_DIM, HEAD_DIM), lambda l, j: (tab_blk(j), 0, 0)),
            pl.BlockSpec((1, 1, HEAD_DIM, HEAD_DIM), lambda l, j: (l, src_blk(j), 0, 0)),
            pl.BlockSpec((1, 1, 1, HEAD_DIM), lambda l, j: (l, j, 0, 0)),
        ],
        out_specs=pl.BlockSpec((1, D_MODEL, HEAD_DIM), lambda l, j: (l, 0, j)),
        compiler_params=_cp(("parallel", "parallel")),
        name="fold_weights",
    )(w_in, ltab, right, scale)


TM_IN = 512


def _norm_mm_kernel(x_ref, g_ref, w_ref, o_ref):
    x = x_ref[...]
    inv = lax.rsqrt(jnp.mean(x * x, axis=-1, keepdims=True) + RMS_EPS)
    h = ((x * inv) * g_ref[...]).astype(BF16)
    o_ref[...] = _dot(h, w_ref[0]).astype(o_ref.dtype)


def _norm_mm(x2d, gamma, wcat, layer):
    return pl.pallas_call(
        _norm_mm_kernel,
        out_shape=jax.ShapeDtypeStruct((ROWS, Z_WIDTH), BF16),
        grid=(ROWS // TM_IN,),
        in_specs=[
            pl.BlockSpec((TM_IN, D_MODEL), lambda i: (i, 0)),
            pl.BlockSpec((1, D_MODEL), lambda i: (0, 0)),
            pl.BlockSpec((1, D_MODEL, Z_WIDTH), lambda i: (layer, 0, 0)),
        ],
        out_specs=pl.BlockSpec((TM_IN, Z_WIDTH), lambda i: (i, 0)),
        compiler_params=_cp(("parallel",)),
        name="norm_in_proj",
    )(x2d, gamma.reshape(1, D_MODEL), wcat)


TM_DFT = 512
TN_DFT = 512
HALF_SEQ = SEQ // 2
N_KBLK = HALF_SEQ // TM_DFT
TAB_GROUP = 16
KX = TM_DFT + TAB_GROUP
COARSE = 64
N_KH = 40


def _table_kernel(ac_ref, as_ref, bc_ref, bs_ref, c_ref, s_ref):
    i = pl.program_id(0)

    def group(gi, carry):
        r0 = pl.multiple_of(gi * TAB_GROUP, TAB_GROUP)
        k0 = i * TM_DFT + r0
        kh = lax.shift_right_logical(k0, 6)
        kl = pl.multiple_of(k0 & (COARSE - 1), TAB_GROUP)
        ca = ac_ref[pl.ds(kh, 1), :]
        sa = as_ref[pl.ds(kh, 1), :]
        cb = bc_ref[pl.ds(kl, TAB_GROUP), :]
        sb = bs_ref[pl.ds(kl, TAB_GROUP), :]
        c_ref[0, pl.ds(r0, TAB_GROUP), :] = (ca * cb - sa * sb).astype(BF16)
        s_ref[0, pl.ds(r0, TAB_GROUP), :] = (sa * cb + ca * sb).astype(BF16)
        return carry

    lax.fori_loop(0, KX // TAB_GROUP, group, 0)


def _dft_tables():
    n = jnp.arange(SEQ, dtype=jnp.int32)[None, :]
    kh = jnp.arange(N_KH, dtype=jnp.int32)[:, None]
    kl = jnp.arange(COARSE, dtype=jnp.int32)[:, None]
    alpha = ((kh * n) % (SEQ // COARSE)).astype(F32) * (2.0 * np.pi * COARSE / SEQ)
    beta = ((kl * n) % SEQ).astype(F32) * (2.0 * np.pi / SEQ)
    scale = 1.0 / np.sqrt(SEQ)
    full = lambda rows: pl.BlockSpec((rows, SEQ), lambda i: (0, 0))
    out = pl.BlockSpec((1, KX, SEQ), lambda i: (i, 0, 0))
    return pl.pallas_call(
        _table_kernel,
        out_shape=(jax.ShapeDtypeStruct((N_KBLK, KX, SEQ), BF16),) * 2,
        grid=(N_KBLK,),
        in_specs=[full(N_KH), full(N_KH), full(COARSE), full(COARSE)],
        out_specs=(out, out),
        compiler_params=_cp(("parallel",)),
        name="dft_tables",
    )(jnp.cos(alpha), jnp.sin(alpha), jnp.cos(beta) * scale, jnp.sin(beta) * scale)


def _dft_kernel(c_ref, s_ref, p_ref, q_ref, lo_ref, hi_ref):
    a = _dot(c_ref[0], p_ref[0])
    bq = _dot(s_ref[0], q_ref[0])
    lo_ref[0] = (a[:TM_DFT] - bq[:TM_DFT]).astype(BF16)
    mirrored = (a + bq).astype(BF16)
    u = lax.broadcasted_iota(jnp.int32, (TM_DFT, KX), 0)
    r = lax.broadcasted_iota(jnp.int32, (TM_DFT, KX), 1)
    flip = jnp.where(r == TM_DFT - u, 1.0, 0.0).astype(BF16)
    hi_ref[0] = _dot(flip, mirrored).astype(BF16)


def _dft(ctab, stab, z3):
    nq = FOURIER_WIDTH // TN_DFT
    half = jax.ShapeDtypeStruct((BATCH, HALF_SEQ, FOURIER_WIDTH), BF16)
    return pl.pallas_call(
        _dft_kernel,
        out_shape=(half, half),
        grid=(N_KBLK, BATCH, nq),
        in_specs=[
            pl.BlockSpec((1, KX, SEQ), lambda i, b, n: (i, 0, 0)),
            pl.BlockSpec((1, KX, SEQ), lambda i, b, n: (i, 0, 0)),
            pl.BlockSpec((1, SEQ, TN_DFT), lambda i, b, n: (b, 0, n)),
            pl.BlockSpec((1, SEQ, TN_DFT), lambda i, b, n: (b, 0, nq + n)),
        ],
        out_specs=(
            pl.BlockSpec((1, TM_DFT, TN_DFT), lambda i, b, n: (b, i, n)),
            pl.BlockSpec((1, TM_DFT, TN_DFT), lambda i, b, n: (b, N_KBLK - 1 - i, n)),
        ),
        compiler_params=_cp(("parallel", "parallel", "parallel")),
        name="position_dft",
    )(ctab, stab, z3, z3)


POOL_PAD = 8


def _pool_kernel(v_ref, o_ref):
    g = pl.program_id(1)
    x = v_ref[0].astype(F32)
    zeros = jnp.zeros((POOL_PAD, GROUP_DIM), F32)
    xp = jnp.concatenate([zeros, x, zeros], axis=0)
    t = lax.broadcasted_iota(jnp.int32, (SEQ, 1), 0)
    for gi, w in enumerate(POOL_WINDOWS):
        @pl.when(g == gi)
        def _(w=w):
            a, span = xp, 1
            while span < w:
                n = a.shape[0] - span
                a = a[:n] + a[span:span + n]
                span *= 2
            start = POOL_PAD - w // 2
            win = a[start:start + SEQ]
            cnt = (jnp.minimum(t + (w - w // 2), SEQ) - jnp.maximum(t - w // 2, 0)).astype(F32)
            o_ref[0] = (win / cnt - x).astype(o_ref.dtype)


def _pool(z3):
    first = (2 * FOURIER_WIDTH) // GROUP_DIM
    return pl.pallas_call(
        _pool_kernel,
        out_shape=jax.ShapeDtypeStruct((BATCH, SEQ, POOL_WIDTH), BF16),
        grid=(BATCH, N_GROUPS),
        in_specs=[pl.BlockSpec((1, SEQ, GROUP_DIM), lambda b, g: (b, 0, first + g))],
        out_specs=pl.BlockSpec((1, SEQ, GROUP_DIM), lambda b, g: (b, 0, g)),
        compiler_params=_cp(("parallel", "parallel")),
        name="pool_minus_identity",
    )(z3)


TM_OUT = 512


BLK_PER_HALF = HALF_SEQ // TM_OUT
BLK_PER_SEQ = SEQ // TM_OUT


def _out_proj_kernel(ylo_ref, yhi_ref, yb_ref, wa_ref, wb_ref, x_ref, g_ref, wrs_ref, wrh_ref, o_ref, h_ref, lg_ref):
    upper = (pl.program_id(0) % BLK_PER_SEQ) >= BLK_PER_HALF
    ya = jnp.where(upper, yhi_ref[...], ylo_ref[...])
    x1 = x_ref[...] + _dot(ya, wa_ref[0]) + _dot(yb_ref[...], wb_ref[0])
    o_ref[...] = x1
    inv = lax.rsqrt(jnp.mean(x1 * x1, axis=-1, keepdims=True) + RMS_EPS)
    h = (x1 * inv) * g_ref[...]
    h_ref[...] = h
    h_hi = h.astype(BF16)
    h_lo = (h - h_hi.astype(F32)).astype(BF16)
    contract_last = (((1,), (1,)), ((), ()))
    a = lax.dot_general(wrs_ref[...], h_hi, contract_last, preferred_element_type=F32)
    b = lax.dot_general(wrh_ref[...], h_lo, contract_last, preferred_element_type=F32)
    lg_ref[...] = a[:N_EXPERTS] + (a[N_EXPERTS:] + b)


def _out_proj(ya_lo, ya_hi, yb2d, w_out_bf, x2d, gamma, wr_split, wr_hi, layer):
    row_blk = lambda i: (i, 0)
    fixed = lambda i: (0, 0)
    half_blk = lambda i: ((i // BLK_PER_SEQ) * BLK_PER_HALF + i % BLK_PER_HALF, 0)
    return pl.pallas_call(
        _out_proj_kernel,
        out_shape=(
            jax.ShapeDtypeStruct((ROWS, D_MODEL), F32),
            jax.ShapeDtypeStruct((ROWS, D_MODEL), F32),
            jax.ShapeDtypeStruct((N_EXPERTS, ROWS), F32),
        ),
        grid=(ROWS // TM_OUT,),
        in_specs=[
            pl.BlockSpec((TM_OUT, FOURIER_WIDTH), half_blk),
            pl.BlockSpec((TM_OUT, FOURIER_WIDTH), half_blk),
            pl.BlockSpec((TM_OUT, POOL_WIDTH), row_blk),
            pl.BlockSpec((1, FOURIER_WIDTH, D_MODEL), lambda i: (layer, 0, 0)),
            pl.BlockSpec((1, POOL_WIDTH, D_MODEL), lambda i: (layer, 1, 0)),
            pl.BlockSpec((TM_OUT, D_MODEL), row_blk),
            pl.BlockSpec((1, D_MODEL), fixed),
            pl.BlockSpec((2 * N_EXPERTS, D_MODEL), fixed),
            pl.BlockSpec((N_EXPERTS, D_MODEL), fixed),
        ],
        out_specs=(
            pl.BlockSpec((TM_OUT, D_MODEL), row_blk),
            pl.BlockSpec((TM_OUT, D_MODEL), row_blk),
            pl.BlockSpec((N_EXPERTS, TM_OUT), lambda i: (0, i)),
        ),
        compiler_params=_cp(("parallel",)),
        name="out_proj_residual",
    )(ya_lo.reshape(BATCH * HALF_SEQ, FOURIER_WIDTH), ya_hi.reshape(BATCH * HALF_SEQ, FOURIER_WIDTH),
      yb2d, w_out_bf, w_out_bf, x2d, gamma.reshape(1, D_MODEL), wr_split, wr_hi)


LANE = 128
NGRP = SEQ // LANE
GRP_SHIFT = 5
assert 1 << GRP_SHIFT == NGRP
ER = N_EXPERTS * NGRP
TINY = float(np.finfo(np.float32).tiny)
N_BISECT = 40
N_MCOL = 8


def _select_kernel(lg_ref, idx_ref, gate_ref, enc_ref, m_sc, ci_sc, cg_sc):
    lg = lg_ref[...]
    ex = jnp.exp(lg - jnp.max(lg, axis=0, keepdims=True))
    p = ex / jnp.sum(ex, axis=0, keepdims=True)

    def total(v):
        return jnp.sum(jnp.sum(v, axis=2, keepdims=True), axis=1, keepdims=True)

    def bisect(_, lohi):
        lo, hi = lohi
        mid = jnp.sqrt(jnp.maximum(lo, TINY)) * jnp.sqrt(hi)
        ok = total(jnp.where(p >= mid, 1.0, 0.0)) >= CAP
        return jnp.where(ok, mid, lo), jnp.where(ok, hi, mid)

    lo0 = jnp.zeros((N_EXPERTS, 1, 1), F32)
    hi0 = jnp.full((N_EXPERTS, 1, 1), 2.0, F32)
    _, hi = lax.fori_loop(0, N_BISECT, bisect, (lo0, hi0))
    below = jnp.where(p < hi, p, -1.0)
    thr = jnp.max(jnp.max(below, axis=2, keepdims=True), axis=1, keepdims=True)
    gt = p > thr
    eq = p == thr
    need = CAP - total(jnp.where(gt, 1.0, 0.0))

    r_i = lax.broadcasted_iota(jnp.int32, (ER, ER), 0)
    c_i = lax.broadcasted_iota(jnp.int32, (ER, ER), 1)
    same_expert = (c_i >> GRP_SHIFT) == (r_i >> GRP_SHIFT)
    rows_before = jnp.where(jnp.logical_and(c_i < r_i, same_expert), 1.0, 0.0).astype(BF16)
    j_i = lax.broadcasted_iota(jnp.int32, (LANE, LANE), 0)
    l_i = lax.broadcasted_iota(jnp.int32, (LANE, LANE), 1)
    lanes_upto = jnp.where(j_i <= l_i, 1.0, 0.0).astype(BF16)
    ones = jnp.ones((LANE, LANE), BF16)

    def prefix(mask_b):
        within = _dot(mask_b, lanes_upto)
        rowtot = _dot(mask_b, ones)
        rowoff = _dot(rows_before, rowtot.astype(BF16))
        return within, rowoff

    eq_f = jnp.where(eq, 1.0, 0.0)
    w_eq, ro_eq = prefix(eq_f.astype(BF16).reshape(ER, LANE))
    eq_before = (w_eq + ro_eq).reshape(N_EXPERTS, NGRP, LANE) - eq_f
    sel = jnp.logical_or(gt, jnp.logical_and(eq, eq_before < need))
    sel_b = jnp.where(sel, 1.0, 0.0).astype(BF16).reshape(ER, LANE)
    within, rowoff = prefix(sel_b)
    count = within + rowoff
    enc_ref[0] = jnp.where(sel_b > 0, count, -count).astype(jnp.int32)

    half = jnp.floor(rowoff * 0.5)
    p2d = p.reshape(ER, LANE)
    p_1 = p2d.astype(BF16)
    rem = p2d - p_1.astype(F32)
    p_2 = rem.astype(BF16)
    p_3 = (rem - p_2.astype(F32)).astype(BF16)
    grp = (lax.broadcasted_iota(jnp.int32, (ER, LANE), 0) & (NGRP - 1)).astype(F32)
    blocks = [within.astype(BF16), grp.astype(BF16), half.astype(BF16), (rowoff - 2.0 * half).astype(BF16),
              p_1, p_2, p_3, sel_b]
    for k, blk in enumerate(blocks):
        m_sc[:, k * LANE:(k + 1) * LANE] = blk
    ci_sc[...] = jnp.zeros_like(ci_sc)
    cg_sc[...] = jnp.zeros_like(cg_sc)

    s_col = lax.broadcasted_iota(jnp.int32, (CAP, 1), 0).astype(F32)
    lane = lax.broadcasted_iota(jnp.int32, (CAP, LANE), 1)
    lane_f = lane.astype(F32)
    ones8 = jnp.ones((8, LANE), BF16)
    g_r = lax.broadcasted_iota(jnp.int32, (NGRP, NGRP), 0)
    g_c = lax.broadcasted_iota(jnp.int32, (NGRP, NGRP), 1)
    groups_before = jnp.where(g_r < g_c, 1.0, 0.0).astype(BF16)

    def per_expert(e, carry):
        r0 = pl.multiple_of(e * NGRP, NGRP)
        table = m_sc[pl.ds(r0, NGRP), :]
        sel_e = table[:, 7 * LANE:]
        rt = lax.dot_general(ones8, sel_e, (((1,), (1,)), ((), ())), preferred_element_type=F32)
        ro = _dot(rt.astype(BF16), groups_before)
        start = ro[0:1]
        stop = start + rt[0:1]
        in_grp = jnp.logical_and(start <= s_col, s_col < stop)
        got = _dot(jnp.where(in_grp, 1.0, 0.0).astype(BF16), table[:, :7 * LANE])
        s_loc = s_col - (2.0 * got[:, 2 * LANE:3 * LANE] + got[:, 3 * LANE:4 * LANE])
        off = _dot(jnp.where(got[:, :LANE] <= s_loc, 1.0, 0.0).astype(BF16), ones)
        tok = got[:, LANE:2 * LANE] * LANE + off
        hit = lane_f == off
        gate = jnp.zeros((CAP, LANE), F32)
        for k in (4, 5, 6):
            gate = gate + _dot(jnp.where(hit, got[:, k * LANE:(k + 1) * LANE], 0.0).astype(BF16), ones)
        ci_sc[...] = jnp.where(lane == e, tok, ci_sc[...])
        cg_sc[...] = jnp.where(lane == e, gate, cg_sc[...])
        return carry

    lax.fori_loop(0, N_EXPERTS, per_expert, 0)
    idx_ref[0] = ci_sc[...].T[:N_EXPERTS].astype(jnp.int32)
    gate_ref[0] = cg_sc[...].T[:N_EXPERTS]


def _select(lg4):
    return pl.pallas_call(
        _select_kernel,
        out_shape=(
            jax.ShapeDtypeStruct((BATCH, N_EXPERTS, CAP), jnp.int32),
            jax.ShapeDtypeStruct((BATCH, N_EXPERTS, CAP), F32),
            jax.ShapeDtypeStruct((BATCH, ER, LANE), jnp.int32),
        ),
        grid=(BATCH,),
        in_specs=[pl.BlockSpec((N_EXPERTS, None, NGRP, LANE), lambda b: (0, b, 0, 0))],
        out_specs=(
            pl.BlockSpec((1, N_EXPERTS, CAP), lambda b: (b, 0, 0)),
            pl.BlockSpec((1, N_EXPERTS, CAP), lambda b: (b, 0, 0)),
            pl.BlockSpec((1, ER, LANE), lambda b: (b, 0, 0)),
        ),
        scratch_shapes=[
            pltpu.VMEM((ER, N_MCOL * LANE), BF16),
            pltpu.VMEM((CAP, LANE), F32),
            pltpu.VMEM((CAP, LANE), F32),
        ],
        compiler_params=_cp(("parallel",)),
        name="select_topc",
    )(lg4)


TF = 256
TD = 256
NF = D_EXPERT // TF
ND = D_MODEL // TD


def _ffn_kernel(idx_ref, meta_ref, h_hbm, wg_ref, wu_ref, wd_ref, o_ref, stage, xs, hid, sem):
    j = pl.program_id(1)

    @pl.when(j == 0)
    def _gather():
        for b in range(BATCH):
            def issue(s, c, b=b):
                row = idx_ref[0, 0, b * CAP + s]
                pltpu.make_async_copy(h_hbm.at[pl.ds(row, 1), :], stage.at[pl.ds(s, 1), :], sem).start()
                return c

            lax.fori_loop(0, CAP, issue, 0, unroll=8)
            pltpu.make_async_copy(h_hbm.at[pl.ds(0, CAP), :], stage, sem).wait()
            xs[b * CAP:(b + 1) * CAP, :] = stage[...].astype(BF16)

    @pl.when(j < NF)
    def _up():
        x = xs[...]
        g = _dot(x, wg_ref[...].astype(BF16))
        u = _dot(x, wu_ref[...].astype(BF16))
        hv = ((g * jax.nn.sigmoid(g)) * u).astype(BF16)
        for k in range(NF):
            @pl.when(j == k)
            def _(k=k):
                hid[:, k * TF:(k + 1) * TF] = hv

    @pl.when(jnp.logical_and(j >= NF, j < NF + ND))
    def _down():
        ys = _dot(hid[...], wd_ref[...].astype(BF16))
        o_ref[0] = (ys * meta_ref[0][:, 0:1]).astype(o_ref.dtype)

    @pl.when(j == NF + ND)
    def _token_id():
        meta = meta_ref[0]
        lane = lax.broadcasted_iota(jnp.int32, (SLOTS, TD), 1)
        o_ref[0] = jnp.where(lane < TD // 2, meta[:, 1:2], meta[:, 2:3]).astype(o_ref.dtype)


Y_WIDTH = D_MODEL + TD
N_META = 4


def _ffn(idx_rows, meta, h2, w_gate, w_up, w_down, layer):
    up_blk = lambda e, j: (layer, e, 0, jnp.minimum(j, NF - 1))
    dn_blk = lambda e, j: (layer, e, 0, jnp.clip(j - NF, 0, ND - 1))
    return pl.pallas_call(
        _ffn_kernel,
        out_shape=jax.ShapeDtypeStruct((N_EXPERTS, SLOTS, Y_WIDTH), BF16),
        grid=(N_EXPERTS, NF + ND + 1),
        in_specs=[
            pl.BlockSpec((1, 1, SLOTS), lambda e, j: (e, 0, 0), memory_space=pltpu.SMEM),
            pl.BlockSpec((1, SLOTS, N_META), lambda e, j: (e, 0, 0)),
            pl.BlockSpec(memory_space=pl.ANY),
            pl.BlockSpec((None, None, D_MODEL, TF), up_blk),
            pl.BlockSpec((None, None, D_MODEL, TF), up_blk),
            pl.BlockSpec((None, None, D_EXPERT, TD), dn_blk),
        ],
        out_specs=pl.BlockSpec((1, SLOTS, TD), lambda e, j: (e, 0, jnp.maximum(j - NF, 0))),
        scratch_shapes=[
            pltpu.VMEM((CAP, D_MODEL), F32),
            pltpu.VMEM((SLOTS, D_MODEL), BF16),
            pltpu.VMEM((SLOTS, D_EXPERT), BF16),
            pltpu.SemaphoreType.DMA,
        ],
        compiler_params=_cp(("arbitrary", "arbitrary")),
        name="expert_swiglu",
    )(idx_rows, meta, h2, w_gate, w_up, w_down)


TB_C = 256
NB_C = SEQ // TB_C
ROW_ALIGN = 16
ROW_SHIFT = 4
KC = 256
KC_SHIFT = 8
assert 1 << ROW_SHIFT == ROW_ALIGN and 1 << KC_SHIFT == KC
N_PHASE = 2
E_PER_PHASE = N_EXPERTS // N_PHASE
STAGE_MAX = E_PER_PHASE * (TB_C + ROW_ALIGN)
STAGE_ROWS = -(-STAGE_MAX // KC) * KC
N_STEPS_C = BATCH * NB_C


def _combine_kernel(bnd_ref, x_ref, g_ref, y_hbm, o_ref, stage, sem, rows_sm, *, final_norm):
    b = pl.program_id(0)
    tb = pl.program_id(1)
    step = b * NB_C + tb

    def chunk_copy(e, src, buf, dst):
        return pltpu.make_async_copy(y_hbm.at[e, pl.ds(src, ROW_ALIGN), :],
                                     stage.at[buf, pl.ds(dst, ROW_ALIGN), :], sem.at[buf])

    def issue(bq, tq, ph):
        pos = jnp.int32(0)
        for e in range(ph * E_PER_PHASE, (ph + 1) * E_PER_PHASE):
            lo = bnd_ref[bq, tq, e]
            hi = bnd_ref[bq, tq + 1, e]
            lo_al = lo - (lo & (ROW_ALIGN - 1))
            nch = jnp.where(hi > lo, lax.shift_right_logical(hi - lo_al + (ROW_ALIGN - 1), ROW_SHIFT), 0)

            def start(c, carry, e=e, lo_al=lo_al, pos=pos):
                src = pl.multiple_of(bq * CAP + lo_al + c * ROW_ALIGN, ROW_ALIGN)
                dst = pl.multiple_of(pos + c * ROW_ALIGN, ROW_ALIGN)
                chunk_copy(e, src, ph, dst).start()
                return carry

            lax.fori_loop(0, nch, start, 0)
            pos = pos + nch * ROW_ALIGN
        rows_sm[ph] = pos

    @pl.when(step == 0)
    def _first():
        stage[...] = jnp.zeros_like(stage)
        issue(b, tb, 0)

    o_ref[...] = x_ref[...]
    t0 = (tb * TB_C).astype(F32)
    lane = lax.broadcasted_iota(jnp.int32, (KC, LANE), 1).astype(F32)
    krow = lax.broadcasted_iota(jnp.int32, (KC, LANE), 0)

    for ph in range(N_PHASE):
        if ph + 1 < N_PHASE:
            issue(b, tb, ph + 1)
        else:
            @pl.when(step + 1 < N_STEPS_C)
            def _prefetch():
                wrap = tb + 1 == NB_C
                issue(jnp.where(wrap, b + 1, b), jnp.where(wrap, 0, tb + 1), 0)

        rows = rows_sm[ph]

        def drain(c, carry, ph=ph):
            chunk_copy(0, 0, ph, 0).wait()
            return carry

        lax.fori_loop(0, lax.shift_right_logical(rows, ROW_SHIFT), drain, 0)

        def accumulate(kc, carry, ph=ph, rows=rows):
            k0 = pl.multiple_of(kc * KC, KC)
            blk = stage[ph, pl.ds(k0, KC), :]
            tok = blk[:, D_MODEL:D_MODEL + LANE].astype(F32) * 64.0 + blk[:, D_MODEL + LANE:].astype(F32) - t0
            live = krow + k0 < rows
            hits = [jnp.logical_and(live, tok - float(q * LANE) == lane) for q in range(TB_C // LANE)]
            onehot_t = jnp.where(jnp.concatenate(hits, axis=1), 1.0, 0.0).astype(BF16)
            o_ref[...] += lax.dot_general(onehot_t, blk[:, :D_MODEL], (((0,), (0,)), ((), ())),
                                          preferred_element_type=F32)
            return carry

        lax.fori_loop(0, lax.shift_right_logical(rows + (KC - 1), KC_SHIFT), accumulate, 0)

    if final_norm:
        x = o_ref[...]
        inv = lax.rsqrt(jnp.mean(x * x, axis=-1, keepdims=True) + RMS_EPS)
        o_ref[...] = (x * inv) * g_ref[...]


def _combine(bnd, x1_2d, gamma, ysg, final_norm):
    grid_spec = pltpu.PrefetchScalarGridSpec(
        num_scalar_prefetch=1,
        grid=(BATCH, NB_C),
        in_specs=[
            pl.BlockSpec((TB_C, D_MODEL), lambda b, t, bnd: (b * NB_C + t, 0)),
            pl.BlockSpec((1, D_MODEL), lambda b, t, bnd: (0, 0)),
            pl.BlockSpec(memory_space=pl.ANY),
        ],
        out_specs=pl.BlockSpec((TB_C, D_MODEL), lambda b, t, bnd: (b * NB_C + t, 0)),
        scratch_shapes=[
            pltpu.VMEM((N_PHASE, STAGE_ROWS, Y_WIDTH), BF16),
            pltpu.SemaphoreType.DMA((N_PHASE,)),
            pltpu.SMEM((N_PHASE,), jnp.int32),
        ],
    )
    return pl.pallas_call(
        functools.partial(_combine_kernel, final_norm=final_norm),
        out_shape=jax.ShapeDtypeStruct((ROWS, D_MODEL), F32),
        grid_spec=grid_spec,
        compiler_params=_cp(("arbitrary", "arbitrary")),
        name="combine_final" if final_norm else "combine",
    )(bnd, x1_2d, gamma.reshape(1, D_MODEL), ysg)


def kernel(x, norm1_g, w_in, w_fourier, w_pool, pool_scale, w_out, norm2_g, w_router, w_gate, w_up, w_down, final_g):
    wcat = _fold_weights(w_in, w_fourier, w_pool, pool_scale)
    ctab, stab = _dft_tables()
    w_out_bf = w_out.astype(BF16)
    w_router_t = jnp.swapaxes(w_router, 1, 2)
    wr_hi = w_router_t.astype(BF16)
    wr_lo = (w_router_t - wr_hi.astype(F32)).astype(BF16)
    wr_split = jnp.concatenate([wr_hi, wr_lo], axis=1)
    batch_base = (jnp.arange(BATCH, dtype=jnp.int32) * SEQ)[:, None, None]

    xc = x.reshape(ROWS, D_MODEL)
    for layer in range(DEPTH):
        z = _norm_mm(xc, norm1_g[layer], wcat, layer)
        z3 = z.reshape(BATCH, SEQ, Z_WIDTH)
        ya_lo, ya_hi = _dft(ctab, stab, z3)
        yb = _pool(z3)
        x1, h2, lg = _out_proj(ya_lo, ya_hi, yb.reshape(ROWS, POOL_WIDTH), w_out_bf, xc,
                               norm2_g[layer], wr_split[layer], wr_hi[layer], layer)
        idx, gates, enc = _select(lg.reshape(N_EXPERTS, BATCH, NGRP, LANE))

        idx_rows = jnp.swapaxes(idx + batch_base, 0, 1).reshape(N_EXPERTS, 1, SLOTS)
        meta = jnp.stack([gates, (idx >> 6).astype(F32), (idx & 63).astype(F32), jnp.zeros_like(gates)], axis=-1)
        meta = jnp.swapaxes(meta, 0, 1).reshape(N_EXPERTS, SLOTS, N_META)
        counts = jnp.abs(enc).reshape(BATCH, N_EXPERTS, SEQ)
        ends = counts[:, :, TB_C - 1::TB_C]
        bnd = jnp.concatenate([jnp.zeros((BATCH, N_EXPERTS, 1), jnp.int32), ends], axis=2)
        bnd = jnp.swapaxes(bnd, 1, 2)

        ysg = _ffn(idx_rows, meta, h2, w_gate, w_up, w_down, layer)
        xc = _combine(bnd, x1, final_g, ysg, layer == DEPTH - 1)
    return xc.reshape(BATCH, SEQ, D_MODEL)
```

```python
import functools

import numpy as np
import jax
import jax.numpy as jnp
from jax import lax
from jax.experimental import pallas as pl
from jax.experimental.pallas import tpu as pltpu

D_MODEL = 2048
BATCH = 4
SEQ = 4096
DEPTH = 2
N_HEADS = 4
HEAD_DIM = 256
POOL_WINDOWS = (2, 4, 8, 16)
N_GROUPS = 4
GROUP_DIM = 256
FOURIER_WIDTH = N_HEADS * HEAD_DIM
POOL_WIDTH = N_GROUPS * GROUP_DIM
N_EXPERTS = 16
CAP = 2 * SEQ // N_EXPERTS
SLOTS = BATCH * CAP
D_EXPERT = D_MODEL
RMS_EPS = 1e-6
ROWS = BATCH * SEQ

F32 = jnp.float32
BF16 = jnp.bfloat16
HIGHEST = lax.Precision.HIGHEST

VMEM_LIMIT = 54 * 1024 * 1024


def _cp(sem, vmem=VMEM_LIMIT):
    return pltpu.CompilerParams(dimension_semantics=sem, vmem_limit_bytes=vmem)


def _dot(a, b):
    return jnp.dot(a, b, preferred_element_type=F32)


N_FOLD = N_HEADS + N_GROUPS
Z_WIDTH = N_FOLD * HEAD_DIM


def _fold_in_kernel(win_ref, r_ref, sc_ref, o_ref):
    j = pl.program_id(1)

    @pl.when(j < N_HEADS)
    def _fourier():
        o_ref[0] = win_ref[0].astype(BF16)

    @pl.when(j >= N_HEADS)
    def _pool():
        t = r_ref[0, 0] * sc_ref[0, 0]
        o_ref[0] = jnp.dot(win_ref[0], t, precision=HIGHEST, preferred_element_type=F32).astype(BF16)


def _fold_in_weights(w_in, w_pool, pool_scale):
    grp = lambda j: jnp.maximum(j - N_HEADS, 0)
    return pl.pallas_call(
        _fold_in_kernel,
        out_shape=jax.ShapeDtypeStruct((DEPTH, D_MODEL, Z_WIDTH), BF16),
        grid=(DEPTH, N_FOLD),
        in_specs=[
            pl.BlockSpec((1, D_MODEL, HEAD_DIM), lambda l, j: (l, 0, j)),
            pl.BlockSpec((1, 1, GROUP_DIM, GROUP_DIM), lambda l, j: (l, grp(j), 0, 0)),
            pl.BlockSpec((1, 1, 1, GROUP_DIM), lambda l, j: (l, grp(j), 0, 0)),
        ],
        out_specs=pl.BlockSpec((1, D_MODEL, HEAD_DIM), lambda l, j: (l, 0, j)),
        compiler_params=_cp(("parallel", "parallel")),
        name="fold_in_weights",
    )(w_in, w_pool, pool_scale.reshape(DEPTH, N_GROUPS, 1, GROUP_DIM))


def _fold_head_kernel(ct_ref, st_ref, wf_ref, mc_ref, ms_ref):
    wf = wf_ref[0, 0]
    mc_ref[0, 0] = jnp.dot(ct_ref[...], wf, precision=HIGHEST, preferred_element_type=F32).astype(BF16)
    ms_ref[0, 0] = jnp.dot(st_ref[...], wf, precision=HIGHEST, preferred_element_type=F32).astype(BF16)


def _fold_head_weights(w_fourier):
    c = np.arange(HEAD_DIM)
    ang = 2.0 * np.pi * ((c[:, None] * c[None, :]) % HEAD_DIM) / HEAD_DIM
    ctab = jnp.asarray((np.cos(ang) / np.sqrt(HEAD_DIM)).astype(np.float32))
    stab = jnp.asarray((np.sin(ang) / np.sqrt(HEAD_DIM)).astype(np.float32))
    tab = pl.BlockSpec((HEAD_DIM, HEAD_DIM), lambda l, h: (0, 0))
    blk = pl.BlockSpec((1, 1, HEAD_DIM, HEAD_DIM), lambda l, h: (l, h, 0, 0))
    out = jax.ShapeDtypeStruct((DEPTH, N_HEADS, HEAD_DIM, HEAD_DIM), BF16)
    return pl.pallas_call(
        _fold_head_kernel,
        out_shape=(out, out),
        grid=(DEPTH, N_HEADS),
        in_specs=[tab, tab, blk],
        out_specs=(blk, blk),
        compiler_params=_cp(("parallel", "parallel")),
        name="fold_head_weights",
    )(ctab, stab, w_fourier)


TM_IN = 512


def _norm_mm_kernel(x_ref, g_ref, w_ref, o_ref):
    x = x_ref[...]
    inv = lax.rsqrt(jnp.mean(x * x, axis=-1, keepdims=True) + RMS_EPS)
    h = ((x * inv) * g_ref[...]).astype(BF16)
    o_ref[...] = _dot(h, w_ref[0]).astype(o_ref.dtype)


def _norm_mm(x2d, gamma, wcat, layer):
    return pl.pallas_call(
        _norm_mm_kernel,
        out_shape=jax.ShapeDtypeStruct((ROWS, Z_WIDTH), BF16),
        grid=(ROWS // TM_IN,),
        in_specs=[
            pl.BlockSpec((TM_IN, D_MODEL), lambda i: (i, 0)),
            pl.BlockSpec((1, D_MODEL), lambda i: (0, 0)),
            pl.BlockSpec((1, D_MODEL, Z_WIDTH), lambda i: (layer, 0, 0)),
        ],
        out_specs=pl.BlockSpec((TM_IN, Z_WIDTH), lambda i: (i, 0)),
        compiler_params=_cp(("parallel",)),
        name="norm_in_proj",
    )(x2d, gamma.reshape(1, D_MODEL), wcat)


TM_DFT = 512
TN_DFT = 512
HALF_SEQ = SEQ // 2
N_KBLK = HALF_SEQ // TM_DFT
TAB_GROUP = 16
KX = TM_DFT + TAB_GROUP
COARSE = 64
N_KH = 40


def _table_kernel(ac_ref, as_ref, bc_ref, bs_ref, c_ref, s_ref):
    i = pl.program_id(0)

    def group(gi, carry):
        r0 = pl.multiple_of(gi * TAB_GROUP, TAB_GROUP)
        k0 = i * TM_DFT + r0
        kh = lax.shift_right_logical(k0, 6)
        kl = pl.multiple_of(k0 & (COARSE - 1), TAB_GROUP)
        ca = ac_ref[pl.ds(kh, 1), :]
        sa = as_ref[pl.ds(kh, 1), :]
        cb = bc_ref[pl.ds(kl, TAB_GROUP), :]
        sb = bs_ref[pl.ds(kl, TAB_GROUP), :]
        c_ref[0, pl.ds(r0, TAB_GROUP), :] = (ca * cb - sa * sb).astype(BF16)
        s_ref[0, pl.ds(r0, TAB_GROUP), :] = (sa * cb + ca * sb).astype(BF16)
        return carry

    lax.fori_loop(0, KX // TAB_GROUP, group, 0)


def _dft_tables():
    n = jnp.arange(SEQ, dtype=jnp.int32)[None, :]
    kh = jnp.arange(N_KH, dtype=jnp.int32)[:, None]
    kl = jnp.arange(COARSE, dtype=jnp.int32)[:, None]
    alpha = ((kh * n) % (SEQ // COARSE)).astype(F32) * (2.0 * np.pi * COARSE / SEQ)
    beta = ((kl * n) % SEQ).astype(F32) * (2.0 * np.pi / SEQ)
    scale = 1.0 / np.sqrt(SEQ)
    full = lambda rows: pl.BlockSpec((rows, SEQ), lambda i: (0, 0))
    out = pl.BlockSpec((1, KX, SEQ), lambda i: (i, 0, 0))
    return pl.pallas_call(
        _table_kernel,
        out_shape=(jax.ShapeDtypeStruct((N_KBLK, KX, SEQ), BF16),) * 2,
        grid=(N_KBLK,),
        in_specs=[full(N_KH), full(N_KH), full(COARSE), full(COARSE)],
        out_specs=(out, out),
        compiler_params=_cp(("parallel",)),
        name="dft_tables",
    )(jnp.cos(alpha), jnp.sin(alpha), jnp.cos(beta) * scale, jnp.sin(beta) * scale)


HEADS_PER_BLK = TN_DFT // HEAD_DIM


def _dft_kernel(c_ref, s_ref, u_ref, mc_ref, ms_ref, lo_ref, hi_ref):
    u = u_ref[0]
    pc = _dot(c_ref[0], u).astype(BF16)
    ps = _dot(s_ref[0], u).astype(BF16)
    heads = lambda v, m_ref: jnp.concatenate(
        [_dot(v[:, h * HEAD_DIM:(h + 1) * HEAD_DIM], m_ref[h]) for h in range(HEADS_PER_BLK)], axis=1)
    a = heads(pc, mc_ref)
    bq = heads(ps, ms_ref)
    lo_ref[0] = (a[:TM_DFT] - bq[:TM_DFT]).astype(BF16)
    mirrored = (a + bq).astype(BF16)
    u = lax.broadcasted_iota(jnp.int32, (TM_DFT, KX), 0)
    r = lax.broadcasted_iota(jnp.int32, (TM_DFT, KX), 1)
    flip = jnp.where(r == TM_DFT - u, 1.0, 0.0).astype(BF16)
    hi_ref[0] = _dot(flip, mirrored).astype(BF16)


def _dft(ctab, stab, z3, mc, ms, layer):
    nq = FOURIER_WIDTH // TN_DFT
    half = jax.ShapeDtypeStruct((BATCH, HALF_SEQ, FOURIER_WIDTH), BF16)
    head_blk = pl.BlockSpec((None, HEADS_PER_BLK, HEAD_DIM, HEAD_DIM), lambda i, b, n: (layer, n, 0, 0))
    return pl.pallas_call(
        _dft_kernel,
        out_shape=(half, half),
        grid=(N_KBLK, BATCH, nq),
        in_specs=[
            pl.BlockSpec((1, KX, SEQ), lambda i, b, n: (i, 0, 0)),
            pl.BlockSpec((1, KX, SEQ), lambda i, b, n: (i, 0, 0)),
            pl.BlockSpec((1, SEQ, TN_DFT), lambda i, b, n: (b, 0, n)),
            head_blk,
            head_blk,
        ],
        out_specs=(
            pl.BlockSpec((1, TM_DFT, TN_DFT), lambda i, b, n: (b, i, n)),
            pl.BlockSpec((1, TM_DFT, TN_DFT), lambda i, b, n: (b, N_KBLK - 1 - i, n)),
        ),
        compiler_params=_cp(("parallel", "parallel", "parallel")),
        name="position_dft",
    )(ctab, stab, z3, mc, ms)


POOL_PAD = 8


def _pool_kernel(v_ref, o_ref):
    g = pl.program_id(1)
    x = v_ref[0].astype(F32)
    zeros = jnp.zeros((POOL_PAD, GROUP_DIM), F32)
    xp = jnp.concatenate([zeros, x, zeros], axis=0)
    t = lax.broadcasted_iota(jnp.int32, (SEQ, 1), 0)
    for gi, w in enumerate(POOL_WINDOWS):
        @pl.when(g == gi)
        def _(w=w):
            a, span = xp, 1
            while span < w:
                n = a.shape[0] - span
                a = a[:n] + a[span:span + n]
                span *= 2
            start = POOL_PAD - w // 2
            win = a[start:start + SEQ]
            cnt = (jnp.minimum(t + (w - w // 2), SEQ) - jnp.maximum(t - w // 2, 0)).astype(F32)
            o_ref[0] = (win / cnt - x).astype(o_ref.dtype)


def _pool(z3):
    first = FOURIER_WIDTH // GROUP_DIM
    return pl.pallas_call(
        _pool_kernel,
        out_shape=jax.ShapeDtypeStruct((BATCH, SEQ, POOL_WIDTH), BF16),
        grid=(BATCH, N_GROUPS),
        in_specs=[pl.BlockSpec((1, SEQ, GROUP_DIM), lambda b, g: (b, 0, first + g))],
        out_specs=pl.BlockSpec((1, SEQ, GROUP_DIM), lambda b, g: (b, 0, g)),
        compiler_params=_cp(("parallel", "parallel")),
        name="pool_minus_identity",
    )(z3)


TM_OUT = 512


BLK_PER_HALF = HALF_SEQ // TM_OUT
BLK_PER_SEQ = SEQ // TM_OUT


def _out_proj_kernel(ylo_ref, yhi_ref, yb_ref, wa_ref, wb_ref, x_ref, g_ref, wrs_ref, wrh_ref, o_ref, h_ref, lg_ref):
    upper = (pl.program_id(0) % BLK_PER_SEQ) >= BLK_PER_HALF
    ya = jnp.where(upper, yhi_ref[...], ylo_ref[...])
    x1 = x_ref[...] + _dot(ya, wa_ref[0]) + _dot(yb_ref[...], wb_ref[0])
    o_ref[...] = x1
    inv = lax.rsqrt(jnp.mean(x1 * x1, axis=-1, keepdims=True) + RMS_EPS)
    h = (x1 * inv) * g_ref[...]
    h_ref[...] = h
    h_hi = h.astype(BF16)
    h_lo = (h - h_hi.astype(F32)).astype(BF16)
    contract_last = (((1,), (1,)), ((), ()))
    a = lax.dot_general(wrs_ref[...], h_hi, contract_last, preferred_element_type=F32)
    b = lax.dot_general(wrh_ref[...], h_lo, contract_last, preferred_element_type=F32)
    lg_ref[...] = a[:N_EXPERTS] + (a[N_EXPERTS:] + b)


def _out_proj(ya_lo, ya_hi, yb2d, w_out_bf, x2d, gamma, wr_split, wr_hi, layer):
    row_blk = lambda i: (i, 0)
    fixed = lambda i: (0, 0)
    half_blk = lambda i: ((i // BLK_PER_SEQ) * BLK_PER_HALF + i % BLK_PER_HALF, 0)
    return pl.pallas_call(
        _out_proj_kernel,
        out_shape=(
            jax.ShapeDtypeStruct((ROWS, D_MODEL), F32),
            jax.ShapeDtypeStruct((ROWS, D_MODEL), F32),
            jax.ShapeDtypeStruct((N_EXPERTS, ROWS), F32),
        ),
        grid=(ROWS // TM_OUT,),
        in_specs=[
            pl.BlockSpec((TM_OUT, FOURIER_WIDTH), half_blk),
            pl.BlockSpec((TM_OUT, FOURIER_WIDTH), half_blk),
            pl.BlockSpec((TM_OUT, POOL_WIDTH), row_blk),
            pl.BlockSpec((1, FOURIER_WIDTH, D_MODEL), lambda i: (layer, 0, 0)),
            pl.BlockSpec((1, POOL_WIDTH, D_MODEL), lambda i: (layer, 1, 0)),
            pl.BlockSpec((TM_OUT, D_MODEL), row_blk),
            pl.BlockSpec((1, D_MODEL), fixed),
            pl.BlockSpec((2 * N_EXPERTS, D_MODEL), fixed),
            pl.BlockSpec((N_EXPERTS, D_MODEL), fixed),
        ],
        out_specs=(
            pl.BlockSpec((TM_OUT, D_MODEL), row_blk),
            pl.BlockSpec((TM_OUT, D_MODEL), row_blk),
            pl.BlockSpec((N_EXPERTS, TM_OUT), lambda i: (0, i)),
        ),
        compiler_params=_cp(("parallel",)),
        name="out_proj_residual",
    )(ya_lo.reshape(BATCH * HALF_SEQ, FOURIER_WIDTH), ya_hi.reshape(BATCH * HALF_SEQ, FOURIER_WIDTH),
      yb2d, w_out_bf, w_out_bf, x2d, gamma.reshape(1, D_MODEL), wr_split, wr_hi)


LANE = 128
NGRP = SEQ // LANE
GRP_SHIFT = 5
assert 1 << GRP_SHIFT == NGRP
ER = N_EXPERTS * NGRP
TINY = float(np.finfo(np.float32).tiny)
N_BISECT = 36
N_MCOL = 8


def _select_kernel(lg_ref, idx_ref, gate_ref, enc_ref, m_sc, ci_sc, cg_sc):
    lg = lg_ref[...]
    ex = jnp.exp(lg - jnp.max(lg, axis=0, keepdims=True))
    p = ex / jnp.sum(ex, axis=0, keepdims=True)

    def total(v):
        return jnp.sum(jnp.sum(v, axis=2, keepdims=True), axis=1, keepdims=True)

    def bisect(_, lohi):
        lo, hi = lohi
        mid = jnp.sqrt(jnp.maximum(lo, TINY)) * jnp.sqrt(hi)
        ok = total(jnp.where(p >= mid, 1.0, 0.0)) >= CAP
        return jnp.where(ok, mid, lo), jnp.where(ok, hi, mid)

    lo0 = jnp.zeros((N_EXPERTS, 1, 1), F32)
    hi0 = jnp.full((N_EXPERTS, 1, 1), 2.0, F32)
    _, hi = lax.fori_loop(0, N_BISECT, bisect, (lo0, hi0))
    below = jnp.where(p < hi, p, -1.0)
    thr = jnp.max(jnp.max(below, axis=2, keepdims=True), axis=1, keepdims=True)
    gt = p > thr
    eq = p == thr
    need = CAP - total(jnp.where(gt, 1.0, 0.0))

    r_i = lax.broadcasted_iota(jnp.int32, (ER, ER), 0)
    c_i = lax.broadcasted_iota(jnp.int32, (ER, ER), 1)
    same_expert = (c_i >> GRP_SHIFT) == (r_i >> GRP_SHIFT)
    rows_before = jnp.where(jnp.logical_and(c_i < r_i, same_expert), 1.0, 0.0).astype(BF16)
    j_i = lax.broadcasted_iota(jnp.int32, (LANE, LANE), 0)
    l_i = lax.broadcasted_iota(jnp.int32, (LANE, LANE), 1)
    lanes_upto = jnp.where(j_i <= l_i, 1.0, 0.0).astype(BF16)
    ones = jnp.ones((LANE, LANE), BF16)

    def prefix(mask_b):
        within = _dot(mask_b, lanes_upto)
        rowtot = _dot(mask_b, ones)
        rowoff = _dot(rows_before, rowtot.astype(BF16))
        return within, rowoff

    eq_f = jnp.where(eq, 1.0, 0.0)
    w_eq, ro_eq = prefix(eq_f.astype(BF16).reshape(ER, LANE))
    eq_before = (w_eq + ro_eq).reshape(N_EXPERTS, NGRP, LANE) - eq_f
    sel = jnp.logical_or(gt, jnp.logical_and(eq, eq_before < need))
    sel_b = jnp.where(sel, 1.0, 0.0).astype(BF16).reshape(ER, LANE)
    within, rowoff = prefix(sel_b)
    count = within + rowoff
    enc_ref[0] = jnp.where(sel_b > 0, count, -count).astype(jnp.int32)

    half = jnp.floor(rowoff * 0.5)
    p2d = p.reshape(ER, LANE)
    p_1 = p2d.astype(BF16)
    rem = p2d - p_1.astype(F32)
    p_2 = rem.astype(BF16)
    p_3 = (rem - p_2.astype(F32)).astype(BF16)
    grp = (lax.broadcasted_iota(jnp.int32, (ER, LANE), 0) & (NGRP - 1)).astype(F32)
    blocks = [within.astype(BF16), grp.astype(BF16), half.astype(BF16), (rowoff - 2.0 * half).astype(BF16),
              p_1, p_2, p_3, sel_b]
    for k, blk in enumerate(blocks):
        m_sc[:, k * LANE:(k + 1) * LANE] = blk
    ci_sc[...] = jnp.zeros_like(ci_sc)
    cg_sc[...] = jnp.zeros_like(cg_sc)

    s_col = lax.broadcasted_iota(jnp.int32, (CAP, 1), 0).astype(F32)
    lane = lax.broadcasted_iota(jnp.int32, (CAP, LANE), 1)
    lane_f = lane.astype(F32)
    ones8 = jnp.ones((8, LANE), BF16)
    g_r = lax.broadcasted_iota(jnp.int32, (NGRP, NGRP), 0)
    g_c = lax.broadcasted_iota(jnp.int32, (NGRP, NGRP), 1)
    groups_before = jnp.where(g_r < g_c, 1.0, 0.0).astype(BF16)

    def per_expert(e, carry):
        r0 = pl.multiple_of(e * NGRP, NGRP)
        table = m_sc[pl.ds(r0, NGRP), :]
        sel_e = table[:, 7 * LANE:]
        rt = lax.dot_general(ones8, sel_e, (((1,), (1,)), ((), ())), preferred_element_type=F32)
        ro = _dot(rt.astype(BF16), groups_before)
        start = ro[0:1]
        stop = start + rt[0:1]
        in_grp = jnp.logical_and(start <= s_col, s_col < stop)
        got = _dot(jnp.where(in_grp, 1.0, 0.0).astype(BF16), table[:, :7 * LANE])
        s_loc = s_col - (2.0 * got[:, 2 * LANE:3 * LANE] + got[:, 3 * LANE:4 * LANE])
        off = _dot(jnp.where(got[:, :LANE] <= s_loc, 1.0, 0.0).astype(BF16), ones)
        tok = got[:, LANE:2 * LANE] * LANE + off
        hit = lane_f == off
        gate = jnp.zeros((CAP, LANE), F32)
        for k in (4, 5, 6):
            gate = gate + _dot(jnp.where(hit, got[:, k * LANE:(k + 1) * LANE], 0.0).astype(BF16), ones)
        ci_sc[...] = jnp.where(lane == e, tok, ci_sc[...])
        cg_sc[...] = jnp.where(lane == e, gate, cg_sc[...])
        return carry

    lax.fori_loop(0, N_EXPERTS, per_expert, 0)
    idx_ref[0] = ci_sc[...].T[:N_EXPERTS].astype(jnp.int32)
    gate_ref[0] = cg_sc[...].T[:N_EXPERTS]


def _select(lg4):
    return pl.pallas_call(
        _select_kernel,
        out_shape=(
            jax.ShapeDtypeStruct((BATCH, N_EXPERTS, CAP), jnp.int32),
            jax.ShapeDtypeStruct((BATCH, N_EXPERTS, CAP), F32),
            jax.ShapeDtypeStruct((BATCH, ER, LANE), jnp.int32),
        ),
        grid=(BATCH,),
        in_specs=[pl.BlockSpec((N_EXPERTS, None, NGRP, LANE), lambda b: (0, b, 0, 0))],
        out_specs=(
            pl.BlockSpec((1, N_EXPERTS, CAP), lambda b: (b, 0, 0)),
            pl.BlockSpec((1, N_EXPERTS, CAP), lambda b: (b, 0, 0)),
            pl.BlockSpec((1, ER, LANE), lambda b: (b, 0, 0)),
        ),
        scratch_shapes=[
            pltpu.VMEM((ER, N_MCOL * LANE), BF16),
            pltpu.VMEM((CAP, LANE), F32),
            pltpu.VMEM((CAP, LANE), F32),
        ],
        compiler_params=_cp(("parallel",)),
        name="select_topc",
    )(lg4)


TF = 256
TD = 256
NF = D_EXPERT // TF
ND = D_MODEL // TD
M_CHUNK = 512
SUBLANES = 8


def _ffn_kernel(idx_ref, meta_ref, h_hbm, wg_ref, wu_ref, wd_ref, o_ref, stage, xs, hid, sem):
    j = pl.program_id(1)

    @pl.when(j == 0)
    def _gather():
        for b in range(BATCH):
            def issue(grp, c, b=b):
                base = pl.multiple_of(grp * SUBLANES, SUBLANES)
                for k in range(SUBLANES):
                    row = idx_ref[0, 0, b * CAP + base + k]
                    pltpu.make_async_copy(h_hbm.at[pl.ds(row, 1), :], stage.at[pl.ds(base + k, 1), :], sem).start()
                return c

            lax.fori_loop(0, CAP // SUBLANES, issue, 0)
            pltpu.make_async_copy(h_hbm.at[pl.ds(0, CAP), :], stage, sem).wait()
            xs[b * CAP:(b + 1) * CAP, :] = stage[...].astype(BF16)

    @pl.when(j < NF)
    def _up():
        wg = wg_ref[...].astype(BF16)
        wu = wu_ref[...].astype(BF16)
        for c in range(SLOTS // M_CHUNK):
            rows = slice(c * M_CHUNK, (c + 1) * M_CHUNK)
            x = xs[rows, :]
            g = _dot(x, wg)
            u = _dot(x, wu)
            hid[j, rows, :] = ((g * jax.nn.sigmoid(g)) * u).astype(BF16)

    @pl.when(jnp.logical_and(j >= NF, j < NF + ND))
    def _down():
        wd = wd_ref[...].astype(BF16)
        for c in range(SLOTS // M_CHUNK):
            rows = slice(c * M_CHUNK, (c + 1) * M_CHUNK)
            hrows = jnp.concatenate([hid[k, rows, :] for k in range(NF)], axis=1)
            o_ref[0, rows, :] = (_dot(hrows, wd) * meta_ref[0, rows, 0:1]).astype(o_ref.dtype)

    @pl.when(j == NF + ND)
    def _token_id():
        meta = meta_ref[0]
        lane = lax.broadcasted_iota(jnp.int32, (SLOTS, TD), 1)
        o_ref[0] = jnp.where(lane < TD // 2, meta[:, 1:2], meta[:, 2:3]).astype(o_ref.dtype)


Y_WIDTH = D_MODEL + TD
N_META = 4


def _ffn(idx_rows, meta, h2, w_gate, w_up, w_down, layer):
    up_blk = lambda e, j: (layer, e, 0, jnp.minimum(j, NF - 1))
    dn_blk = lambda e, j: (layer, e, 0, jnp.clip(j - NF, 0, ND - 1))
    return pl.pallas_call(
        _ffn_kernel,
        out_shape=jax.ShapeDtypeStruct((N_EXPERTS, SLOTS, Y_WIDTH), BF16),
        grid=(N_EXPERTS, NF + ND + 1),
        in_specs=[
            pl.BlockSpec((1, 1, SLOTS), lambda e, j: (e, 0, 0), memory_space=pltpu.SMEM),
            pl.BlockSpec((1, SLOTS, N_META), lambda e, j: (e, 0, 0)),
            pl.BlockSpec(memory_space=pl.ANY),
            pl.BlockSpec((None, None, D_MODEL, TF), up_blk),
            pl.BlockSpec((None, None, D_MODEL, TF), up_blk),
            pl.BlockSpec((None, None, D_EXPERT, TD), dn_blk),
        ],
        out_specs=pl.BlockSpec((1, SLOTS, TD), lambda e, j: (e, 0, jnp.maximum(j - NF, 0))),
        scratch_shapes=[
            pltpu.VMEM((CAP, D_MODEL), F32),
            pltpu.VMEM((SLOTS, D_MODEL), BF16),
            pltpu.VMEM((NF, SLOTS, TF), BF16),
            pltpu.SemaphoreType.DMA,
        ],
        compiler_params=_cp(("arbitrary", "arbitrary")),
        name="expert_swiglu",
    )(idx_rows, meta, h2, w_gate, w_up, w_down)


TB_C = 256
NB_C = SEQ // TB_C
ROW_ALIGN = 16
ROW_SHIFT = 4
KC = 256
KC_SHIFT = 8
assert 1 << ROW_SHIFT == ROW_ALIGN and 1 << KC_SHIFT == KC
N_PHASE = 2
E_PER_PHASE = N_EXPERTS // N_PHASE
STAGE_MAX = E_PER_PHASE * (TB_C + ROW_ALIGN)
STAGE_ROWS = -(-STAGE_MAX // KC) * KC
N_STEPS_C = BATCH * NB_C


def _combine_kernel(bnd_ref, x_ref, g_ref, y_hbm, o_ref, stage, sem, rows_sm, *, final_norm):
    b = pl.program_id(0)
    tb = pl.program_id(1)
    step = b * NB_C + tb

    def chunk_copy(e, src, buf, dst):
        return pltpu.make_async_copy(y_hbm.at[e, pl.ds(src, ROW_ALIGN), :],
                                     stage.at[buf, pl.ds(dst, ROW_ALIGN), :], sem.at[buf])

    def issue(bq, tq, ph):
        pos = jnp.int32(0)
        for e in range(ph * E_PER_PHASE, (ph + 1) * E_PER_PHASE):
            lo = bnd_ref[bq, tq, e]
            hi = bnd_ref[bq, tq + 1, e]
            lo_al = lo - (lo & (ROW_ALIGN - 1))
            nch = jnp.where(hi > lo, lax.shift_right_logical(hi - lo_al + (ROW_ALIGN - 1), ROW_SHIFT), 0)

            def start(c, carry, e=e, lo_al=lo_al, pos=pos):
                src = pl.multiple_of(bq * CAP + lo_al + c * ROW_ALIGN, ROW_ALIGN)
                dst = pl.multiple_of(pos + c * ROW_ALIGN, ROW_ALIGN)
                chunk_copy(e, src, ph, dst).start()
                return carry

            lax.fori_loop(0, nch, start, 0)
            pos = pos + nch * ROW_ALIGN
        rows_sm[ph] = pos

    @pl.when(step == 0)
    def _first():
        stage[...] = jnp.zeros_like(stage)
        issue(b, tb, 0)

    o_ref[...] = x_ref[...]
    t0 = (tb * TB_C).astype(F32)
    lane = lax.broadcasted_iota(jnp.int32, (KC, LANE), 1).astype(F32)
    krow = lax.broadcasted_iota(jnp.int32, (KC, LANE), 0)

    for ph in range(N_PHASE):
        if ph + 1 < N_PHASE:
            issue(b, tb, ph + 1)
        else:
            @pl.when(step + 1 < N_STEPS_C)
            def _prefetch():
                wrap = tb + 1 == NB_C
                issue(jnp.where(wrap, b + 1, b), jnp.where(wrap, 0, tb + 1), 0)

        rows = rows_sm[ph]

        def drain(c, carry, ph=ph):
            chunk_copy(0, 0, ph, 0).wait()
            return carry

        lax.fori_loop(0, lax.shift_right_logical(rows, ROW_SHIFT), drain, 0)

        def accumulate(kc, carry, ph=ph, rows=rows):
            k0 = pl.multiple_of(kc * KC, KC)
            blk = stage[ph, pl.ds(k0, KC), :]
            tok = blk[:, D_MODEL:D_MODEL + LANE].astype(F32) * 64.0 + blk[:, D_MODEL + LANE:].astype(F32) - t0
            live = krow + k0 < rows
            hits = [jnp.logical_and(live, tok - float(q * LANE) == lane) for q in range(TB_C // LANE)]
            onehot_t = jnp.where(jnp.concatenate(hits, axis=1), 1.0, 0.0).astype(BF16)
            o_ref[...] += lax.dot_general(onehot_t, blk[:, :D_MODEL], (((0,), (0,)), ((), ())),
                                          preferred_element_type=F32)
            return carry

        lax.fori_loop(0, lax.shift_right_logical(rows + (KC - 1), KC_SHIFT), accumulate, 0)

    if final_norm:
        x = o_ref[...]
        inv = lax.rsqrt(jnp.mean(x * x, axis=-1, keepdims=True) + RMS_EPS)
        o_ref[...] = (x * inv) * g_ref[...]


def _combine(bnd, x1_2d, gamma, ysg, final_norm):
    grid_spec = pltpu.PrefetchScalarGridSpec(
        num_scalar_prefetch=1,
        grid=(BATCH, NB_C),
        in_specs=[
            pl.BlockSpec((TB_C, D_MODEL), lambda b, t, bnd: (b * NB_C + t, 0)),
            pl.BlockSpec((1, D_MODEL), lambda b, t, bnd: (0, 0)),
            pl.BlockSpec(memory_space=pl.ANY),
        ],
        out_specs=pl.BlockSpec((TB_C, D_MODEL), lambda b, t, bnd: (b * NB_C + t, 0)),
        scratch_shapes=[
            pltpu.VMEM((N_PHASE, STAGE_ROWS, Y_WIDTH), BF16),
            pltpu.SemaphoreType.DMA((N_PHASE,)),
            pltpu.SMEM((N_PHASE,), jnp.int32),
        ],
    )
    return pl.pallas_call(
        functools.partial(_combine_kernel, final_norm=final_norm),
        out_shape=jax.ShapeDtypeStruct((ROWS, D_MODEL), F32),
        grid_spec=grid_spec,
        compiler_params=_cp(("arbitrary", "arbitrary")),
        name="combine_final" if final_norm else "combine",
    )(bnd, x1_2d, gamma.reshape(1, D_MODEL), ysg)


def kernel(x, norm1_g, w_in, w_fourier, w_pool, pool_scale, w_out, norm2_g, w_router, w_gate, w_up, w_down, final_g):
    wcat = _fold_in_weights(w_in, w_pool, pool_scale)
    mc, ms = _fold_head_weights(w_fourier)
    ctab, stab = _dft_tables()
    w_out_bf = w_out.astype(BF16)
    w_router_t = jnp.swapaxes(w_router, 1, 2)
    wr_hi = w_router_t.astype(BF16)
    wr_lo = (w_router_t - wr_hi.astype(F32)).astype(BF16)
    wr_split = jnp.concatenate([wr_hi, wr_lo], axis=1)
    batch_base = (jnp.arange(BATCH, dtype=jnp.int32) * SEQ)[:, None, None]

    xc = x.reshape(ROWS, D_MODEL)
    for layer in range(DEPTH):
        z = _norm_mm(xc, norm1_g[layer], wcat, layer)
        z3 = z.reshape(BATCH, SEQ, Z_WIDTH)
        ya_lo, ya_hi = _dft(ctab, stab, z3, mc, ms, layer)
        yb = _pool(z3)
        x1, h2, lg = _out_proj(ya_lo, ya_hi, yb.reshape(ROWS, POOL_WIDTH), w_out_bf, xc,
                               norm2_g[layer], wr_split[layer], wr_hi[layer], layer)
        idx, gates, enc = _select(lg.reshape(N_EXPERTS, BATCH, NGRP, LANE))

        idx_rows = jnp.swapaxes(idx + batch_base, 0, 1).reshape(N_EXPERTS, 1, SLOTS)
        meta = jnp.stack([gates, (idx >> 6).astype(F32), (idx & 63).astype(F32), jnp.zeros_like(gates)], axis=-1)
        meta = jnp.swapaxes(meta, 0, 1).reshape(N_EXPERTS, SLOTS, N_META)
        counts = jnp.abs(enc).reshape(BATCH, N_EXPERTS, SEQ)
        ends = counts[:, :, TB_C - 1::TB_C]
        bnd = jnp.concatenate([jnp.zeros((BATCH, N_EXPERTS, 1), jnp.int32), ends], axis=2)
        bnd = jnp.swapaxes(bnd, 1, 2)

        ysg = _ffn(idx_rows, meta, h2, w_gate, w_up, w_down, layer)
        xc = _combine(bnd, x1, final_g, ysg, layer == DEPTH - 1)
    return xc.reshape(BATCH, SEQ, D_MODEL)
```

```python
import functools

import numpy as np
import jax
import jax.numpy as jnp
from jax import lax
from jax.experimental import pallas as pl
from jax.experimental.pallas import tpu as pltpu

D_MODEL = 2048
BATCH = 4
SEQ = 4096
DEPTH = 2
N_HEADS = 4
HEAD_DIM = 256
POOL_WINDOWS = (2, 4, 8, 16)
N_GROUPS = 4
GROUP_DIM = 256
FOURIER_WIDTH = N_HEADS * HEAD_DIM
POOL_WIDTH = N_GROUPS * GROUP_DIM
N_EXPERTS = 16
CAP = 2 * SEQ // N_EXPERTS
SLOTS = BATCH * CAP
D_EXPERT = D_MODEL
RMS_EPS = 1e-6
ROWS = BATCH * SEQ

F32 = jnp.float32
BF16 = jnp.bfloat16
HIGHEST = lax.Precision.HIGHEST

VMEM_LIMIT = 54 * 1024 * 1024


def _cp(sem, vmem=VMEM_LIMIT):
    return pltpu.CompilerParams(dimension_semantics=sem, vmem_limit_bytes=vmem)


def _dot(a, b):
    return jnp.dot(a, b, preferred_element_type=F32)


N_FOLD = N_HEADS + N_GROUPS
Z_WIDTH = N_FOLD * HEAD_DIM


def _fold_in_kernel(win_ref, r_ref, sc_ref, o_ref):
    j = pl.program_id(1)

    @pl.when(j < N_HEADS)
    def _fourier():
        o_ref[0] = win_ref[0].astype(BF16)

    @pl.when(j >= N_HEADS)
    def _pool():
        t = r_ref[0, 0] * sc_ref[0, 0]
        o_ref[0] = jnp.dot(win_ref[0], t, precision=HIGHEST, preferred_element_type=F32).astype(BF16)


def _fold_in_weights(w_in, w_pool, pool_scale):
    grp = lambda j: jnp.maximum(j - N_HEADS, 0)
    return pl.pallas_call(
        _fold_in_kernel,
        out_shape=jax.ShapeDtypeStruct((DEPTH, D_MODEL, Z_WIDTH), BF16),
        grid=(DEPTH, N_FOLD),
        in_specs=[
            pl.BlockSpec((1, D_MODEL, HEAD_DIM), lambda l, j: (l, 0, j)),
            pl.BlockSpec((1, 1, GROUP_DIM, GROUP_DIM), lambda l, j: (l, grp(j), 0, 0)),
            pl.BlockSpec((1, 1, 1, GROUP_DIM), lambda l, j: (l, grp(j), 0, 0)),
        ],
        out_specs=pl.BlockSpec((1, D_MODEL, HEAD_DIM), lambda l, j: (l, 0, j)),
        compiler_params=_cp(("parallel", "parallel")),
        name="fold_in_weights",
    )(w_in, w_pool, pool_scale.reshape(DEPTH, N_GROUPS, 1, GROUP_DIM))


def _fold_head_kernel(ct_ref, st_ref, wf_ref, mc_ref, ms_ref):
    wf = wf_ref[0, 0]
    mc_ref[0, 0] = jnp.dot(ct_ref[...], wf, precision=HIGHEST, preferred_element_type=F32).astype(BF16)
    ms_ref[0, 0] = jnp.dot(st_ref[...], wf, precision=HIGHEST, preferred_element_type=F32).astype(BF16)


def _fold_head_weights(w_fourier):
    c = np.arange(HEAD_DIM)
    ang = 2.0 * np.pi * ((c[:, None] * c[None, :]) % HEAD_DIM) / HEAD_DIM
    ctab = jnp.asarray((np.cos(ang) / np.sqrt(HEAD_DIM)).astype(np.float32))
    stab = jnp.asarray((np.sin(ang) / np.sqrt(HEAD_DIM)).astype(np.float32))
    tab = pl.BlockSpec((HEAD_DIM, HEAD_DIM), lambda l, h: (0, 0))
    blk = pl.BlockSpec((1, 1, HEAD_DIM, HEAD_DIM), lambda l, h: (l, h, 0, 0))
    out = jax.ShapeDtypeStruct((DEPTH, N_HEADS, HEAD_DIM, HEAD_DIM), BF16)
    return pl.pallas_call(
        _fold_head_kernel,
        out_shape=(out, out),
        grid=(DEPTH, N_HEADS),
        in_specs=[tab, tab, blk],
        out_specs=(blk, blk),
        compiler_params=_cp(("parallel", "parallel")),
        name="fold_head_weights",
    )(ctab, stab, w_fourier)


TM_IN = 512


def _norm_mm_kernel(x_ref, g_ref, w_ref, o_ref):
    x = x_ref[...]
    inv = lax.rsqrt(jnp.mean(x * x, axis=-1, keepdims=True) + RMS_EPS)
    h = ((x * inv) * g_ref[...]).astype(BF16)
    o_ref[...] = _dot(h, w_ref[0]).astype(o_ref.dtype)


def _norm_mm(x2d, gamma, wcat, layer):
    return pl.pallas_call(
        _norm_mm_kernel,
        out_shape=jax.ShapeDtypeStruct((ROWS, Z_WIDTH), BF16),
        grid=(ROWS // TM_IN,),
        in_specs=[
            pl.BlockSpec((TM_IN, D_MODEL), lambda i: (i, 0)),
            pl.BlockSpec((1, D_MODEL), lambda i: (0, 0)),
            pl.BlockSpec((1, D_MODEL, Z_WIDTH), lambda i: (layer, 0, 0)),
        ],
        out_specs=pl.BlockSpec((TM_IN, Z_WIDTH), lambda i: (i, 0)),
        compiler_params=_cp(("parallel",)),
        name="norm_in_proj",
    )(x2d, gamma.reshape(1, D_MODEL), wcat)


TM_DFT = 512
TN_DFT = 512
HALF_SEQ = SEQ // 2
N_KBLK = HALF_SEQ // TM_DFT
TAB_GROUP = 16
KX = TM_DFT + TAB_GROUP
COARSE = 64
N_KH = 40


def _table_kernel(ac_ref, as_ref, bc_ref, bs_ref, c_ref, s_ref):
    i = pl.program_id(0)

    def group(gi, carry):
        r0 = pl.multiple_of(gi * TAB_GROUP, TAB_GROUP)
        k0 = i * TM_DFT + r0
        kh = lax.shift_right_logical(k0, 6)
        kl = pl.multiple_of(k0 & (COARSE - 1), TAB_GROUP)
        ca = ac_ref[pl.ds(kh, 1), :]
        sa = as_ref[pl.ds(kh, 1), :]
        cb = bc_ref[pl.ds(kl, TAB_GROUP), :]
        sb = bs_ref[pl.ds(kl, TAB_GROUP), :]
        c_ref[0, pl.ds(r0, TAB_GROUP), :] = (ca * cb - sa * sb).astype(BF16)
        s_ref[0, pl.ds(r0, TAB_GROUP), :] = (sa * cb + ca * sb).astype(BF16)
        return carry

    lax.fori_loop(0, KX // TAB_GROUP, group, 0)


def _dft_tables():
    n = jnp.arange(SEQ, dtype=jnp.int32)[None, :]
    kh = jnp.arange(N_KH, dtype=jnp.int32)[:, None]
    kl = jnp.arange(COARSE, dtype=jnp.int32)[:, None]
    alpha = ((kh * n) % (SEQ // COARSE)).astype(F32) * (2.0 * np.pi * COARSE / SEQ)
    beta = ((kl * n) % SEQ).astype(F32) * (2.0 * np.pi / SEQ)
    scale = 1.0 / np.sqrt(SEQ)
    full = lambda rows: pl.BlockSpec((rows, SEQ), lambda i: (0, 0))
    out = pl.BlockSpec((1, KX, SEQ), lambda i: (i, 0, 0))
    return pl.pallas_call(
        _table_kernel,
        out_shape=(jax.ShapeDtypeStruct((N_KBLK, KX, SEQ), BF16),) * 2,
        grid=(N_KBLK,),
        in_specs=[full(N_KH), full(N_KH), full(COARSE), full(COARSE)],
        out_specs=(out, out),
        compiler_params=_cp(("parallel",)),
        name="dft_tables",
    )(jnp.cos(alpha), jnp.sin(alpha), jnp.cos(beta) * scale, jnp.sin(beta) * scale)


HEADS_PER_BLK = TN_DFT // HEAD_DIM


def _dft_kernel(c_ref, s_ref, u_ref, mc_ref, ms_ref, lo_ref, hi_ref):
    u = u_ref[0]
    pc = _dot(c_ref[0], u).astype(BF16)
    ps = _dot(s_ref[0], u).astype(BF16)
    heads = lambda v, m_ref: jnp.concatenate(
        [_dot(v[:, h * HEAD_DIM:(h + 1) * HEAD_DIM], m_ref[h]) for h in range(HEADS_PER_BLK)], axis=1)
    a = heads(pc, mc_ref)
    bq = heads(ps, ms_ref)
    lo_ref[0] = (a[:TM_DFT] - bq[:TM_DFT]).astype(BF16)
    mirrored = (a + bq).astype(BF16)
    u = lax.broadcasted_iota(jnp.int32, (TM_DFT, KX), 0)
    r = lax.broadcasted_iota(jnp.int32, (TM_DFT, KX), 1)
    flip = jnp.where(r == TM_DFT - u, 1.0, 0.0).astype(BF16)
    hi_ref[0] = _dot(flip, mirrored).astype(BF16)


def _dft(ctab, stab, z3, mc, ms, layer):
    nq = FOURIER_WIDTH // TN_DFT
    half = jax.ShapeDtypeStruct((BATCH, HALF_SEQ, FOURIER_WIDTH), BF16)
    head_blk = pl.BlockSpec((None, HEADS_PER_BLK, HEAD_DIM, HEAD_DIM), lambda i, b, n: (layer, n, 0, 0))
    return pl.pallas_call(
        _dft_kernel,
        out_shape=(half, half),
        grid=(N_KBLK, BATCH, nq),
        in_specs=[
            pl.BlockSpec((1, KX, SEQ), lambda i, b, n: (i, 0, 0)),
            pl.BlockSpec((1, KX, SEQ), lambda i, b, n: (i, 0, 0)),
            pl.BlockSpec((1, SEQ, TN_DFT), lambda i, b, n: (b, 0, n)),
            head_blk,
            head_blk,
        ],
        out_specs=(
            pl.BlockSpec((1, TM_DFT, TN_DFT), lambda i, b, n: (b, i, n)),
            pl.BlockSpec((1, TM_DFT, TN_DFT), lambda i, b, n: (b, N_KBLK - 1 - i, n)),
        ),
        compiler_params=_cp(("parallel", "parallel", "parallel")),
        name="position_dft",
    )(ctab, stab, z3, mc, ms)


POOL_PAD = 8


def _pool_kernel(v_ref, o_ref):
    g = pl.program_id(1)
    x = v_ref[0].astype(F32)
    zeros = jnp.zeros((POOL_PAD, GROUP_DIM), F32)
    xp = jnp.concatenate([zeros, x, zeros], axis=0)
    t = lax.broadcasted_iota(jnp.int32, (SEQ, 1), 0)
    for gi, w in enumerate(POOL_WINDOWS):
        @pl.when(g == gi)
        def _(w=w):
            a, span = xp, 1
            while span < w:
                n = a.shape[0] - span
                a = a[:n] + a[span:span + n]
                span *= 2
            start = POOL_PAD - w // 2
            win = a[start:start + SEQ]
            cnt = (jnp.minimum(t + (w - w // 2), SEQ) - jnp.maximum(t - w // 2, 0)).astype(F32)
            o_ref[0] = (win / cnt - x).astype(o_ref.dtype)


def _pool(z3):
    first = FOURIER_WIDTH // GROUP_DIM
    return pl.pallas_call(
        _pool_kernel,
        out_shape=jax.ShapeDtypeStruct((BATCH, SEQ, POOL_WIDTH), BF16),
        grid=(BATCH, N_GROUPS),
        in_specs=[pl.BlockSpec((1, SEQ, GROUP_DIM), lambda b, g: (b, 0, first + g))],
        out_specs=pl.BlockSpec((1, SEQ, GROUP_DIM), lambda b, g: (b, 0, g)),
        compiler_params=_cp(("parallel", "parallel")),
        name="pool_minus_identity",
    )(z3)


TM_OUT = 512


BLK_PER_HALF = HALF_SEQ // TM_OUT
BLK_PER_SEQ = SEQ // TM_OUT


def _out_proj_kernel(ylo_ref, yhi_ref, yb_ref, wa_ref, wb_ref, x_ref, g_ref, wrs_ref, wrh_ref, o_ref, h_ref, lg_ref):
    upper = (pl.program_id(0) % BLK_PER_SEQ) >= BLK_PER_HALF
    ya = jnp.where(upper, yhi_ref[...], ylo_ref[...])
    x1 = x_ref[...] + _dot(ya, wa_ref[0]) + _dot(yb_ref[...], wb_ref[0])
    o_ref[...] = x1
    inv = lax.rsqrt(jnp.mean(x1 * x1, axis=-1, keepdims=True) + RMS_EPS)
    h = (x1 * inv) * g_ref[...]
    h_ref[...] = h
    h_hi = h.astype(BF16)
    h_lo = (h - h_hi.astype(F32)).astype(BF16)
    contract_last = (((1,), (1,)), ((), ()))
    a = lax.dot_general(wrs_ref[...], h_hi, contract_last, preferred_element_type=F32)
    b = lax.dot_general(wrh_ref[...], h_lo, contract_last, preferred_element_type=F32)
    lg_ref[...] = a[:N_EXPERTS] + (a[N_EXPERTS:] + b)


def _out_proj(ya_lo, ya_hi, yb2d, w_out_bf, x2d, gamma, wr_split, wr_hi, layer):
    row_blk = lambda i: (i, 0)
    fixed = lambda i: (0, 0)
    half_blk = lambda i: ((i // BLK_PER_SEQ) * BLK_PER_HALF + i % BLK_PER_HALF, 0)
    return pl.pallas_call(
        _out_proj_kernel,
        out_shape=(
            jax.ShapeDtypeStruct((ROWS, D_MODEL), F32),
            jax.ShapeDtypeStruct((ROWS, D_MODEL), F32),
            jax.ShapeDtypeStruct((N_EXPERTS, ROWS), F32),
        ),
        grid=(ROWS // TM_OUT,),
        in_specs=[
            pl.BlockSpec((TM_OUT, FOURIER_WIDTH), half_blk),
            pl.BlockSpec((TM_OUT, FOURIER_WIDTH), half_blk),
            pl.BlockSpec((TM_OUT, POOL_WIDTH), row_blk),
            pl.BlockSpec((1, FOURIER_WIDTH, D_MODEL), lambda i: (layer, 0, 0)),
            pl.BlockSpec((1, POOL_WIDTH, D_MODEL), lambda i: (layer, 1, 0)),
            pl.BlockSpec((TM_OUT, D_MODEL), row_blk),
            pl.BlockSpec((1, D_MODEL), fixed),
            pl.BlockSpec((2 * N_EXPERTS, D_MODEL), fixed),
            pl.BlockSpec((N_EXPERTS, D_MODEL), fixed),
        ],
        out_specs=(
            pl.BlockSpec((TM_OUT, D_MODEL), row_blk),
            pl.BlockSpec((TM_OUT, D_MODEL), row_blk),
            pl.BlockSpec((N_EXPERTS, TM_OUT), lambda i: (0, i)),
        ),
        compiler_params=_cp(("parallel",)),
        name="out_proj_residual",
    )(ya_lo.reshape(BATCH * HALF_SEQ, FOURIER_WIDTH), ya_hi.reshape(BATCH * HALF_SEQ, FOURIER_WIDTH),
      yb2d, w_out_bf, w_out_bf, x2d, gamma.reshape(1, D_MODEL), wr_split, wr_hi)


LANE = 128
NGRP = SEQ // LANE
GRP_SHIFT = 5
assert 1 << GRP_SHIFT == NGRP
ER = N_EXPERTS * NGRP
TINY = float(np.finfo(np.float32).tiny)
N_BISECT = 36
N_MCOL = 8
EXPERTS_PER_ITER = 4


def _select_kernel(lg_ref, idx_ref, gate_ref, enc_ref, m_sc, ci_sc, cg_sc):
    lg = lg_ref[...]
    ex = jnp.exp(lg - jnp.max(lg, axis=0, keepdims=True))
    p = ex / jnp.sum(ex, axis=0, keepdims=True)

    def total(v):
        return jnp.sum(jnp.sum(v, axis=2, keepdims=True), axis=1, keepdims=True)

    def bisect(_, lohi):
        lo, hi = lohi
        mid = jnp.sqrt(jnp.maximum(lo, TINY)) * jnp.sqrt(hi)
        ok = total(jnp.where(p >= mid, 1.0, 0.0)) >= CAP
        return jnp.where(ok, mid, lo), jnp.where(ok, hi, mid)

    lo0 = jnp.zeros((N_EXPERTS, 1, 1), F32)
    hi0 = jnp.full((N_EXPERTS, 1, 1), 2.0, F32)
    _, hi = lax.fori_loop(0, N_BISECT, bisect, (lo0, hi0))
    below = jnp.where(p < hi, p, -1.0)
    thr = jnp.max(jnp.max(below, axis=2, keepdims=True), axis=1, keepdims=True)
    gt = p > thr
    eq = p == thr
    need = CAP - total(jnp.where(gt, 1.0, 0.0))

    r_i = lax.broadcasted_iota(jnp.int32, (ER, ER), 0)
    c_i = lax.broadcasted_iota(jnp.int32, (ER, ER), 1)
    same_expert = (c_i >> GRP_SHIFT) == (r_i >> GRP_SHIFT)
    rows_before = jnp.where(jnp.logical_and(c_i < r_i, same_expert), 1.0, 0.0).astype(BF16)
    j_i = lax.broadcasted_iota(jnp.int32, (LANE, LANE), 0)
    l_i = lax.broadcasted_iota(jnp.int32, (LANE, LANE), 1)
    lanes_upto = jnp.where(j_i <= l_i, 1.0, 0.0).astype(BF16)
    ones = jnp.ones((LANE, LANE), BF16)

    def prefix(mask_b):
        within = _dot(mask_b, lanes_upto)
        rowtot = _dot(mask_b, ones)
        rowoff = _dot(rows_before, rowtot.astype(BF16))
        return within, rowoff

    eq_f = jnp.where(eq, 1.0, 0.0)
    w_eq, ro_eq = prefix(eq_f.astype(BF16).reshape(ER, LANE))
    eq_before = (w_eq + ro_eq).reshape(N_EXPERTS, NGRP, LANE) - eq_f
    sel = jnp.logical_or(gt, jnp.logical_and(eq, eq_before < need))
    sel_b = jnp.where(sel, 1.0, 0.0).astype(BF16).reshape(ER, LANE)
    within, rowoff = prefix(sel_b)
    count = within + rowoff
    enc_ref[0] = jnp.where(sel_b > 0, count, -count).astype(jnp.int32)

    half = jnp.floor(rowoff * 0.5)
    p2d = p.reshape(ER, LANE)
    p_1 = p2d.astype(BF16)
    rem = p2d - p_1.astype(F32)
    p_2 = rem.astype(BF16)
    p_3 = (rem - p_2.astype(F32)).astype(BF16)
    grp = (lax.broadcasted_iota(jnp.int32, (ER, LANE), 0) & (NGRP - 1)).astype(F32)
    blocks = [within.astype(BF16), grp.astype(BF16), half.astype(BF16), (rowoff - 2.0 * half).astype(BF16),
              p_1, p_2, p_3, sel_b]
    for k, blk in enumerate(blocks):
        m_sc[:, k * LANE:(k + 1) * LANE] = blk
    ci_sc[...] = jnp.zeros_like(ci_sc)
    cg_sc[...] = jnp.zeros_like(cg_sc)

    s_col = lax.broadcasted_iota(jnp.int32, (CAP, 1), 0).astype(F32)
    lane = lax.broadcasted_iota(jnp.int32, (CAP, LANE), 1)
    lane_f = lane.astype(F32)
    ones8 = jnp.ones((8, LANE), BF16)
    g_r = lax.broadcasted_iota(jnp.int32, (NGRP, NGRP), 0)
    g_c = lax.broadcasted_iota(jnp.int32, (NGRP, NGRP), 1)
    groups_before = jnp.where(g_r < g_c, 1.0, 0.0).astype(BF16)

    def one_expert(e):
        r0 = pl.multiple_of(e * NGRP, NGRP)
        table = m_sc[pl.ds(r0, NGRP), :]
        sel_e = table[:, 7 * LANE:]
        rt = lax.dot_general(ones8, sel_e, (((1,), (1,)), ((), ())), preferred_element_type=F32)
        ro = _dot(rt.astype(BF16), groups_before)
        start = ro[0:1]
        stop = start + rt[0:1]
        in_grp = jnp.logical_and(start <= s_col, s_col < stop)
        got = _dot(jnp.where(in_grp, 1.0, 0.0).astype(BF16), table[:, :7 * LANE])
        s_loc = s_col - (2.0 * got[:, 2 * LANE:3 * LANE] + got[:, 3 * LANE:4 * LANE])
        off = _dot(jnp.where(got[:, :LANE] <= s_loc, 1.0, 0.0).astype(BF16), ones)
        tok = got[:, LANE:2 * LANE] * LANE + off
        hit = lane_f == off
        gate = jnp.zeros((CAP, LANE), F32)
        for k in (4, 5, 6):
            gate = gate + _dot(jnp.where(hit, got[:, k * LANE:(k + 1) * LANE], 0.0).astype(BF16), ones)
        return tok, gate

    def per_group(i, carry):
        ci = ci_sc[...]
        cg = cg_sc[...]
        for k in range(EXPERTS_PER_ITER):
            e = i * EXPERTS_PER_ITER + k
            tok, gate = one_expert(e)
            ci = jnp.where(lane == e, tok, ci)
            cg = jnp.where(lane == e, gate, cg)
        ci_sc[...] = ci
        cg_sc[...] = cg
        return carry

    lax.fori_loop(0, N_EXPERTS // EXPERTS_PER_ITER, per_group, 0)
    idx_ref[0] = ci_sc[...].T[:N_EXPERTS].astype(jnp.int32)
    gate_ref[0] = cg_sc[...].T[:N_EXPERTS]


def _select(lg4):
    return pl.pallas_call(
        _select_kernel,
        out_shape=(
            jax.ShapeDtypeStruct((BATCH, N_EXPERTS, CAP), jnp.int32),
            jax.ShapeDtypeStruct((BATCH, N_EXPERTS, CAP), F32),
            jax.ShapeDtypeStruct((BATCH, ER, LANE), jnp.int32),
        ),
        grid=(BATCH,),
        in_specs=[pl.BlockSpec((N_EXPERTS, None, NGRP, LANE), lambda b: (0, b, 0, 0))],
        out_specs=(
            pl.BlockSpec((1, N_EXPERTS, CAP), lambda b: (b, 0, 0)),
            pl.BlockSpec((1, N_EXPERTS, CAP), lambda b: (b, 0, 0)),
            pl.BlockSpec((1, ER, LANE), lambda b: (b, 0, 0)),
        ),
        scratch_shapes=[
            pltpu.VMEM((ER, N_MCOL * LANE), BF16),
            pltpu.VMEM((CAP, LANE), F32),
            pltpu.VMEM((CAP, LANE), F32),
        ],
        compiler_params=_cp(("parallel",)),
        name="select_topc",
    )(lg4)


TF = 256
TD = 256
NF = D_EXPERT // TF
ND = D_MODEL // TD
M_CHUNK = 512
SUBLANES = 8


def _ffn_kernel(idx_ref, meta_ref, h_hbm, wg_ref, wu_ref, wd_ref, o_ref, stage, xs, hid, sem):
    j = pl.program_id(1)

    @pl.when(j == 0)
    def _gather():
        def start_rows(b):
            buf = b % 2

            def issue(grp, c):
                base = pl.multiple_of(grp * SUBLANES, SUBLANES)
                for k in range(SUBLANES):
                    row = idx_ref[0, 0, b * CAP + base + k]
                    pltpu.make_async_copy(h_hbm.at[pl.ds(row, 1), :], stage.at[buf, pl.ds(base + k, 1), :],
                                          sem.at[buf]).start()
                return c

            lax.fori_loop(0, CAP // SUBLANES, issue, 0)

        start_rows(0)
        for b in range(BATCH):
            if b + 1 < BATCH:
                start_rows(b + 1)
            buf = b % 2
            pltpu.make_async_copy(h_hbm.at[pl.ds(0, CAP), :], stage.at[buf], sem.at[buf]).wait()
            xs[b * CAP:(b + 1) * CAP, :] = stage[buf].astype(BF16)

    @pl.when(j < NF)
    def _up():
        wg = wg_ref[...].astype(BF16)
        wu = wu_ref[...].astype(BF16)
        for c in range(SLOTS // M_CHUNK):
            rows = slice(c * M_CHUNK, (c + 1) * M_CHUNK)
            x = xs[rows, :]
            g = _dot(x, wg)
            u = _dot(x, wu)
            hid[j, rows, :] = ((g * jax.nn.sigmoid(g)) * u).astype(BF16)

    @pl.when(jnp.logical_and(j >= NF, j < NF + ND))
    def _down():
        wd = wd_ref[...].astype(BF16)
        for c in range(SLOTS // M_CHUNK):
            rows = slice(c * M_CHUNK, (c + 1) * M_CHUNK)
            hrows = jnp.concatenate([hid[k, rows, :] for k in range(NF)], axis=1)
            o_ref[0, rows, :] = (_dot(hrows, wd) * meta_ref[0, rows, 0:1]).astype(o_ref.dtype)

    @pl.when(j == NF + ND)
    def _token_id():
        meta = meta_ref[0]
        lane = lax.broadcasted_iota(jnp.int32, (SLOTS, TD), 1)
        o_ref[0] = jnp.where(lane < TD // 2, meta[:, 1:2], meta[:, 2:3]).astype(o_ref.dtype)


Y_WIDTH = D_MODEL + TD
N_META = 4


def _ffn(idx_rows, meta, h2, w_gate, w_up, w_down, layer):
    up_blk = lambda e, j: (layer, e, 0, jnp.minimum(j, NF - 1))
    dn_blk = lambda e, j: (layer, e, 0, jnp.clip(j - NF, 0, ND - 1))
    return pl.pallas_call(
        _ffn_kernel,
        out_shape=jax.ShapeDtypeStruct((N_EXPERTS, SLOTS, Y_WIDTH), BF16),
        grid=(N_EXPERTS, NF + ND + 1),
        in_specs=[
            pl.BlockSpec((1, 1, SLOTS), lambda e, j: (e, 0, 0), memory_space=pltpu.SMEM),
            pl.BlockSpec((1, SLOTS, N_META), lambda e, j: (e, 0, 0)),
            pl.BlockSpec(memory_space=pl.ANY),
            pl.BlockSpec((None, None, D_MODEL, TF), up_blk),
            pl.BlockSpec((None, None, D_MODEL, TF), up_blk),
            pl.BlockSpec((None, None, D_EXPERT, TD), dn_blk),
        ],
        out_specs=pl.BlockSpec((1, SLOTS, TD), lambda e, j: (e, 0, jnp.maximum(j - NF, 0))),
        scratch_shapes=[
            pltpu.VMEM((2, CAP, D_MODEL), F32),
            pltpu.VMEM((SLOTS, D_MODEL), BF16),
            pltpu.VMEM((NF, SLOTS, TF), BF16),
            pltpu.SemaphoreType.DMA((2,)),
        ],
        compiler_params=_cp(("arbitrary", "arbitrary")),
        name="expert_swiglu",
    )(idx_rows, meta, h2, w_gate, w_up, w_down)


TB_C = 256
NB_C = SEQ // TB_C
ROW_ALIGN = 16
ROW_SHIFT = 4
KC = 256
KC_SHIFT = 8
assert 1 << ROW_SHIFT == ROW_ALIGN and 1 << KC_SHIFT == KC
N_PHASE = 2
E_PER_PHASE = N_EXPERTS // N_PHASE
STAGE_MAX = E_PER_PHASE * (TB_C + ROW_ALIGN)
STAGE_ROWS = -(-STAGE_MAX // KC) * KC
N_STEPS_C = BATCH * NB_C


def _combine_kernel(bnd_ref, x_ref, g_ref, y_hbm, o_ref, stage, sem, rows_sm, *, final_norm):
    b = pl.program_id(0)
    tb = pl.program_id(1)
    step = b * NB_C + tb

    def chunk_copy(e, src, buf, dst):
        return pltpu.make_async_copy(y_hbm.at[e, pl.ds(src, ROW_ALIGN), :],
                                     stage.at[buf, pl.ds(dst, ROW_ALIGN), :], sem.at[buf])

    def issue(bq, tq, ph):
        pos = jnp.int32(0)
        for e in range(ph * E_PER_PHASE, (ph + 1) * E_PER_PHASE):
            lo = bnd_ref[bq, tq, e]
            hi = bnd_ref[bq, tq + 1, e]
            lo_al = lo - (lo & (ROW_ALIGN - 1))
            nch = jnp.where(hi > lo, lax.shift_right_logical(hi - lo_al + (ROW_ALIGN - 1), ROW_SHIFT), 0)

            def start(c, carry, e=e, lo_al=lo_al, pos=pos):
                src = pl.multiple_of(bq * CAP + lo_al + c * ROW_ALIGN, ROW_ALIGN)
                dst = pl.multiple_of(pos + c * ROW_ALIGN, ROW_ALIGN)
                chunk_copy(e, src, ph, dst).start()
                return carry

            lax.fori_loop(0, nch, start, 0)
            pos = pos + nch * ROW_ALIGN
        rows_sm[ph] = pos

    @pl.when(step == 0)
    def _first():
        stage[...] = jnp.zeros_like(stage)
        issue(b, tb, 0)

    o_ref[...] = x_ref[...]
    t0 = (tb * TB_C).astype(F32)
    lane = lax.broadcasted_iota(jnp.int32, (KC, LANE), 1).astype(F32)
    krow = lax.broadcasted_iota(jnp.int32, (KC, LANE), 0)

    for ph in range(N_PHASE):
        if ph + 1 < N_PHASE:
            issue(b, tb, ph + 1)
        else:
            @pl.when(step + 1 < N_STEPS_C)
            def _prefetch():
                wrap = tb + 1 == NB_C
                issue(jnp.where(wrap, b + 1, b), jnp.where(wrap, 0, tb + 1), 0)

        rows = rows_sm[ph]

        def drain(c, carry, ph=ph):
            chunk_copy(0, 0, ph, 0).wait()
            return carry

        lax.fori_loop(0, lax.shift_right_logical(rows, ROW_SHIFT), drain, 0)

        def accumulate(kc, carry, ph=ph, rows=rows):
            k0 = pl.multiple_of(kc * KC, KC)
            blk = stage[ph, pl.ds(k0, KC), :]
            tok = blk[:, D_MODEL:D_MODEL + LANE].astype(F32) * 64.0 + blk[:, D_MODEL + LANE:].astype(F32) - t0
            live = krow + k0 < rows
            hits = [jnp.logical_and(live, tok - float(q * LANE) == lane) for q in range(TB_C // LANE)]
            onehot_t = jnp.where(jnp.concatenate(hits, axis=1), 1.0, 0.0).astype(BF16)
            o_ref[...] += lax.dot_general(onehot_t, blk[:, :D_MODEL], (((0,), (0,)), ((), ())),
                                          preferred_element_type=F32)
            return carry

        lax.fori_loop(0, lax.shift_right_logical(rows + (KC - 1), KC_SHIFT), accumulate, 0)

    if final_norm:
        x = o_ref[...]
        inv = lax.rsqrt(jnp.mean(x * x, axis=-1, keepdims=True) + RMS_EPS)
        o_ref[...] = (x * inv) * g_ref[...]


def _combine(bnd, x1_2d, gamma, ysg, final_norm):
    grid_spec = pltpu.PrefetchScalarGridSpec(
        num_scalar_prefetch=1,
        grid=(BATCH, NB_C),
        in_specs=[
            pl.BlockSpec((TB_C, D_MODEL), lambda b, t, bnd: (b * NB_C + t, 0)),
            pl.BlockSpec((1, D_MODEL), lambda b, t, bnd: (0, 0)),
            pl.BlockSpec(memory_space=pl.ANY),
        ],
        out_specs=pl.BlockSpec((TB_C, D_MODEL), lambda b, t, bnd: (b * NB_C + t, 0)),
        scratch_shapes=[
            pltpu.VMEM((N_PHASE, STAGE_ROWS, Y_WIDTH), BF16),
            pltpu.SemaphoreType.DMA((N_PHASE,)),
            pltpu.SMEM((N_PHASE,), jnp.int32),
        ],
    )
    return pl.pallas_call(
        functools.partial(_combine_kernel, final_norm=final_norm),
        out_shape=jax.ShapeDtypeStruct((ROWS, D_MODEL), F32),
        grid_spec=grid_spec,
        compiler_params=_cp(("arbitrary", "arbitrary")),
        name="combine_final" if final_norm else "combine",
    )(bnd, x1_2d, gamma.reshape(1, D_MODEL), ysg)


def kernel(x, norm1_g, w_in, w_fourier, w_pool, pool_scale, w_out, norm2_g, w_router, w_gate, w_up, w_down, final_g):
    wcat = _fold_in_weights(w_in, w_pool, pool_scale)
    mc, ms = _fold_head_weights(w_fourier)
    ctab, stab = _dft_tables()
    w_out_bf = w_out.astype(BF16)
    w_router_t = jnp.swapaxes(w_router, 1, 2)
    wr_hi = w_router_t.astype(BF16)
    wr_lo = (w_router_t - wr_hi.astype(F32)).astype(BF16)
    wr_split = jnp.concatenate([wr_hi, wr_lo], axis=1)
    batch_base = (jnp.arange(BATCH, dtype=jnp.int32) * SEQ)[:, None, None]

    xc = x.reshape(ROWS, D_MODEL)
    for layer in range(DEPTH):
        z = _norm_mm(xc, norm1_g[layer], wcat, layer)
        z3 = z.reshape(BATCH, SEQ, Z_WIDTH)
        ya_lo, ya_hi = _dft(ctab, stab, z3, mc, ms, layer)
        yb = _pool(z3)
        x1, h2, lg = _out_proj(ya_lo, ya_hi, yb.reshape(ROWS, POOL_WIDTH), w_out_bf, xc,
                               norm2_g[layer], wr_split[layer], wr_hi[layer], layer)
        idx, gates, enc = _select(lg.reshape(N_EXPERTS, BATCH, NGRP, LANE))

        idx_rows = jnp.swapaxes(idx + batch_base, 0, 1).reshape(N_EXPERTS, 1, SLOTS)
        meta = jnp.stack([gates, (idx >> 6).astype(F32), (idx & 63).astype(F32), jnp.zeros_like(gates)], axis=-1)
        meta = jnp.swapaxes(meta, 0, 1).reshape(N_EXPERTS, SLOTS, N_META)
        counts = jnp.abs(enc).reshape(BATCH, N_EXPERTS, SEQ)
        ends = counts[:, :, TB_C - 1::TB_C]
        bnd = jnp.concatenate([jnp.zeros((BATCH, N_EXPERTS, 1), jnp.int32), ends], axis=2)
        bnd = jnp.swapaxes(bnd, 1, 2)

        ysg = _ffn(idx_rows, meta, h2, w_gate, w_up, w_down, layer)
        xc = _combine(bnd, x1, final_g, ysg, layer == DEPTH - 1)
    return xc.reshape(BATCH, SEQ, D_MODEL)
```

```python
import functools

import numpy as np
import jax
import jax.numpy as jnp
from jax import lax
from jax.experimental import pallas as pl
from jax.experimental.pallas import tpu as pltpu

D_MODEL = 2048
BATCH = 4
SEQ = 4096
DEPTH = 2
N_HEADS = 4
HEAD_DIM = 256
POOL_WINDOWS = (2, 4, 8, 16)
N_GROUPS = 4
GROUP_DIM = 256
FOURIER_WIDTH = N_HEADS * HEAD_DIM
POOL_WIDTH = N_GROUPS * GROUP_DIM
N_EXPERTS = 16
CAP = 2 * SEQ // N_EXPERTS
SLOTS = BATCH * CAP
D_EXPERT = D_MODEL
RMS_EPS = 1e-6
ROWS = BATCH * SEQ

F32 = jnp.float32
BF16 = jnp.bfloat16
HIGHEST = lax.Precision.HIGHEST

VMEM_LIMIT = 54 * 1024 * 1024


def _cp(sem, vmem=VMEM_LIMIT):
    return pltpu.CompilerParams(dimension_semantics=sem, vmem_limit_bytes=vmem)


def _dot(a, b):
    return jnp.dot(a, b, preferred_element_type=F32)


N_FOLD = N_HEADS + N_GROUPS
Z_WIDTH = N_FOLD * HEAD_DIM


def _fold_in_kernel(win_ref, r_ref, sc_ref, o_ref):
    j = pl.program_id(1)

    @pl.when(j < N_HEADS)
    def _fourier():
        o_ref[0] = win_ref[0].astype(BF16)

    @pl.when(j >= N_HEADS)
    def _pool():
        t = r_ref[0, 0] * sc_ref[0, 0]
        o_ref[0] = jnp.dot(win_ref[0], t, precision=HIGHEST, preferred_element_type=F32).astype(BF16)


def _fold_in_weights(w_in, w_pool, pool_scale):
    grp = lambda j: jnp.maximum(j - N_HEADS, 0)
    return pl.pallas_call(
        _fold_in_kernel,
        out_shape=jax.ShapeDtypeStruct((DEPTH, D_MODEL, Z_WIDTH), BF16),
        grid=(DEPTH, N_FOLD),
        in_specs=[
            pl.BlockSpec((1, D_MODEL, HEAD_DIM), lambda l, j: (l, 0, j)),
            pl.BlockSpec((1, 1, GROUP_DIM, GROUP_DIM), lambda l, j: (l, grp(j), 0, 0)),
            pl.BlockSpec((1, 1, 1, GROUP_DIM), lambda l, j: (l, grp(j), 0, 0)),
        ],
        out_specs=pl.BlockSpec((1, D_MODEL, HEAD_DIM), lambda l, j: (l, 0, j)),
        compiler_params=_cp(("parallel", "parallel")),
        name="fold_in_weights",
    )(w_in, w_pool, pool_scale.reshape(DEPTH, N_GROUPS, 1, GROUP_DIM))


def _fold_head_kernel(ct_ref, st_ref, wf_ref, mc_ref, ms_ref):
    wf = wf_ref[0, 0]
    mc_ref[0, 0] = jnp.dot(ct_ref[...], wf, precision=HIGHEST, preferred_element_type=F32).astype(BF16)
    ms_ref[0, 0] = jnp.dot(st_ref[...], wf, precision=HIGHEST, preferred_element_type=F32).astype(BF16)


def _fold_head_weights(w_fourier):
    c = np.arange(HEAD_DIM)
    ang = 2.0 * np.pi * ((c[:, None] * c[None, :]) % HEAD_DIM) / HEAD_DIM
    ctab = jnp.asarray((np.cos(ang) / np.sqrt(HEAD_DIM)).astype(np.float32))
    stab = jnp.asarray((np.sin(ang) / np.sqrt(HEAD_DIM)).astype(np.float32))
    tab = pl.BlockSpec((HEAD_DIM, HEAD_DIM), lambda l, h: (0, 0))
    blk = pl.BlockSpec((1, 1, HEAD_DIM, HEAD_DIM), lambda l, h: (l, h, 0, 0))
    out = jax.ShapeDtypeStruct((DEPTH, N_HEADS, HEAD_DIM, HEAD_DIM), BF16)
    return pl.pallas_call(
        _fold_head_kernel,
        out_shape=(out, out),
        grid=(DEPTH, N_HEADS),
        in_specs=[tab, tab, blk],
        out_specs=(blk, blk),
        compiler_params=_cp(("parallel", "parallel")),
        name="fold_head_weights",
    )(ctab, stab, w_fourier)


TM_IN = 512


def _norm_mm_kernel(x_ref, g_ref, w_ref, o_ref):
    x = x_ref[...]
    inv = lax.rsqrt(jnp.mean(x * x, axis=-1, keepdims=True) + RMS_EPS)
    h = ((x * inv) * g_ref[...]).astype(BF16)
    o_ref[...] = _dot(h, w_ref[0]).astype(o_ref.dtype)


def _norm_mm(x2d, gamma, wcat, layer):
    return pl.pallas_call(
        _norm_mm_kernel,
        out_shape=jax.ShapeDtypeStruct((ROWS, Z_WIDTH), BF16),
        grid=(ROWS // TM_IN,),
        in_specs=[
            pl.BlockSpec((TM_IN, D_MODEL), lambda i: (i, 0)),
            pl.BlockSpec((1, D_MODEL), lambda i: (0, 0)),
            pl.BlockSpec((1, D_MODEL, Z_WIDTH), lambda i: (layer, 0, 0)),
        ],
        out_specs=pl.BlockSpec((TM_IN, Z_WIDTH), lambda i: (i, 0)),
        compiler_params=_cp(("parallel",)),
        name="norm_in_proj",
    )(x2d, gamma.reshape(1, D_MODEL), wcat)


TM_DFT = 512
TN_DFT = 512
HALF_SEQ = SEQ // 2
N_KBLK = HALF_SEQ // TM_DFT
TAB_GROUP = 16
KX = TM_DFT + TAB_GROUP
COARSE = 64
N_KH = 40


def _table_kernel(ac_ref, as_ref, bc_ref, bs_ref, c_ref, s_ref):
    i = pl.program_id(0)

    def group(gi, carry):
        r0 = pl.multiple_of(gi * TAB_GROUP, TAB_GROUP)
        k0 = i * TM_DFT + r0
        kh = lax.shift_right_logical(k0, 6)
        kl = pl.multiple_of(k0 & (COARSE - 1), TAB_GROUP)
        ca = ac_ref[pl.ds(kh, 1), :]
        sa = as_ref[pl.ds(kh, 1), :]
        cb = bc_ref[pl.ds(kl, TAB_GROUP), :]
        sb = bs_ref[pl.ds(kl, TAB_GROUP), :]
        c_ref[0, pl.ds(r0, TAB_GROUP), :] = (ca * cb - sa * sb).astype(BF16)
        s_ref[0, pl.ds(r0, TAB_GROUP), :] = (sa * cb + ca * sb).astype(BF16)
        return carry

    lax.fori_loop(0, KX // TAB_GROUP, group, 0)


def _dft_tables():
    n = jnp.arange(SEQ, dtype=jnp.int32)[None, :]
    kh = jnp.arange(N_KH, dtype=jnp.int32)[:, None]
    kl = jnp.arange(COARSE, dtype=jnp.int32)[:, None]
    alpha = ((kh * n) % (SEQ // COARSE)).astype(F32) * (2.0 * np.pi * COARSE / SEQ)
    beta = ((kl * n) % SEQ).astype(F32) * (2.0 * np.pi / SEQ)
    scale = 1.0 / np.sqrt(SEQ)
    full = lambda rows: pl.BlockSpec((rows, SEQ), lambda i: (0, 0))
    out = pl.BlockSpec((1, KX, SEQ), lambda i: (i, 0, 0))
    return pl.pallas_call(
        _table_kernel,
        out_shape=(jax.ShapeDtypeStruct((N_KBLK, KX, SEQ), BF16),) * 2,
        grid=(N_KBLK,),
        in_specs=[full(N_KH), full(N_KH), full(COARSE), full(COARSE)],
        out_specs=(out, out),
        compiler_params=_cp(("parallel",)),
        name="dft_tables",
    )(jnp.cos(alpha), jnp.sin(alpha), jnp.cos(beta) * scale, jnp.sin(beta) * scale)


HEADS_PER_BLK = TN_DFT // HEAD_DIM


def _dft_kernel(c_ref, s_ref, u_ref, mc_ref, ms_ref, lo_ref, hi_ref):
    u = u_ref[0]
    pc = _dot(c_ref[0], u).astype(BF16)
    ps = _dot(s_ref[0], u).astype(BF16)
    heads = lambda v, m_ref: jnp.concatenate(
        [_dot(v[:, h * HEAD_DIM:(h + 1) * HEAD_DIM], m_ref[h]) for h in range(HEADS_PER_BLK)], axis=1)
    a = heads(pc, mc_ref)
    bq = heads(ps, ms_ref)
    lo_ref[0] = (a[:TM_DFT] - bq[:TM_DFT]).astype(BF16)
    mirrored = (a + bq).astype(BF16)
    u = lax.broadcasted_iota(jnp.int32, (TM_DFT, KX), 0)
    r = lax.broadcasted_iota(jnp.int32, (TM_DFT, KX), 1)
    flip = jnp.where(r == TM_DFT - u, 1.0, 0.0).astype(BF16)
    hi_ref[0] = _dot(flip, mirrored).astype(BF16)


def _dft(ctab, stab, z3, mc, ms, layer):
    nq = FOURIER_WIDTH // TN_DFT
    half = jax.ShapeDtypeStruct((BATCH, HALF_SEQ, FOURIER_WIDTH), BF16)
    head_blk = pl.BlockSpec((None, HEADS_PER_BLK, HEAD_DIM, HEAD_DIM), lambda i, b, n: (layer, n, 0, 0))
    return pl.pallas_call(
        _dft_kernel,
        out_shape=(half, half),
        grid=(N_KBLK, BATCH, nq),
        in_specs=[
            pl.BlockSpec((1, KX, SEQ), lambda i, b, n: (i, 0, 0)),
            pl.BlockSpec((1, KX, SEQ), lambda i, b, n: (i, 0, 0)),
            pl.BlockSpec((1, SEQ, TN_DFT), lambda i, b, n: (b, 0, n)),
            head_blk,
            head_blk,
        ],
        out_specs=(
            pl.BlockSpec((1, TM_DFT, TN_DFT), lambda i, b, n: (b, i, n)),
            pl.BlockSpec((1, TM_DFT, TN_DFT), lambda i, b, n: (b, N_KBLK - 1 - i, n)),
        ),
        compiler_params=_cp(("parallel", "parallel", "parallel")),
        name="position_dft",
    )(ctab, stab, z3, mc, ms)


POOL_PAD = 8


def _pool_kernel(v_ref, o_ref):
    g = pl.program_id(1)
    x = v_ref[0].astype(F32)
    zeros = jnp.zeros((POOL_PAD, GROUP_DIM), F32)
    xp = jnp.concatenate([zeros, x, zeros], axis=0)
    t = lax.broadcasted_iota(jnp.int32, (SEQ, 1), 0)
    for gi, w in enumerate(POOL_WINDOWS):
        @pl.when(g == gi)
        def _(w=w):
            a, span = xp, 1
            while span < w:
                n = a.shape[0] - span
                a = a[:n] + a[span:span + n]
                span *= 2
            start = POOL_PAD - w // 2
            win = a[start:start + SEQ]
            cnt = (jnp.minimum(t + (w - w // 2), SEQ) - jnp.maximum(t - w // 2, 0)).astype(F32)
            o_ref[0] = (win / cnt - x).astype(o_ref.dtype)


def _pool(z3):
    first = FOURIER_WIDTH // GROUP_DIM
    return pl.pallas_call(
        _pool_kernel,
        out_shape=jax.ShapeDtypeStruct((BATCH, SEQ, POOL_WIDTH), BF16),
        grid=(BATCH, N_GROUPS),
        in_specs=[pl.BlockSpec((1, SEQ, GROUP_DIM), lambda b, g: (b, 0, first + g))],
        out_specs=pl.BlockSpec((1, SEQ, GROUP_DIM), lambda b, g: (b, 0, g)),
        compiler_params=_cp(("parallel", "parallel")),
        name="pool_minus_identity",
    )(z3)


TM_OUT = 512


BLK_PER_HALF = HALF_SEQ // TM_OUT
BLK_PER_SEQ = SEQ // TM_OUT


def _out_proj_kernel(ylo_ref, yhi_ref, yb_ref, wa_ref, wb_ref, x_ref, g_ref, wrs_ref, wrh_ref, o_ref, h_ref, lg_ref):
    upper = (pl.program_id(0) % BLK_PER_SEQ) >= BLK_PER_HALF
    ya = jnp.where(upper, yhi_ref[...], ylo_ref[...])
    x1 = x_ref[...] + _dot(ya, wa_ref[0]) + _dot(yb_ref[...], wb_ref[0])
    o_ref[...] = x1
    inv = lax.rsqrt(jnp.mean(x1 * x1, axis=-1, keepdims=True) + RMS_EPS)
    h = (x1 * inv) * g_ref[...]
    h_ref[...] = h
    h_hi = h.astype(BF16)
    h_lo = (h - h_hi.astype(F32)).astype(BF16)
    contract_last = (((1,), (1,)), ((), ()))
    a = lax.dot_general(wrs_ref[...], h_hi, contract_last, preferred_element_type=F32)
    b = lax.dot_general(wrh_ref[...], h_lo, contract_last, preferred_element_type=F32)
    lg_ref[...] = a[:N_EXPERTS] + (a[N_EXPERTS:] + b)


def _out_proj(ya_lo, ya_hi, yb2d, w_out_bf, x2d, gamma, wr_split, wr_hi, layer):
    row_blk = lambda i: (i, 0)
    fixed = lambda i: (0, 0)
    half_blk = lambda i: ((i // BLK_PER_SEQ) * BLK_PER_HALF + i % BLK_PER_HALF, 0)
    return pl.pallas_call(
        _out_proj_kernel,
        out_shape=(
            jax.ShapeDtypeStruct((ROWS, D_MODEL), F32),
            jax.ShapeDtypeStruct((ROWS, D_MODEL), F32),
            jax.ShapeDtypeStruct((N_EXPERTS, ROWS), F32),
        ),
        grid=(ROWS // TM_OUT,),
        in_specs=[
            pl.BlockSpec((TM_OUT, FOURIER_WIDTH), half_blk),
            pl.BlockSpec((TM_OUT, FOURIER_WIDTH), half_blk),
            pl.BlockSpec((TM_OUT, POOL_WIDTH), row_blk),
            pl.BlockSpec((1, FOURIER_WIDTH, D_MODEL), lambda i: (layer, 0, 0)),
            pl.BlockSpec((1, POOL_WIDTH, D_MODEL), lambda i: (layer, 1, 0)),
            pl.BlockSpec((TM_OUT, D_MODEL), row_blk),
            pl.BlockSpec((1, D_MODEL), fixed),
            pl.BlockSpec((2 * N_EXPERTS, D_MODEL), fixed),
            pl.BlockSpec((N_EXPERTS, D_MODEL), fixed),
        ],
        out_specs=(
            pl.BlockSpec((TM_OUT, D_MODEL), row_blk),
            pl.BlockSpec((TM_OUT, D_MODEL), row_blk),
            pl.BlockSpec((N_EXPERTS, TM_OUT), lambda i: (0, i)),
        ),
        compiler_params=_cp(("parallel",)),
        name="out_proj_residual",
    )(ya_lo.reshape(BATCH * HALF_SEQ, FOURIER_WIDTH), ya_hi.reshape(BATCH * HALF_SEQ, FOURIER_WIDTH),
      yb2d, w_out_bf, w_out_bf, x2d, gamma.reshape(1, D_MODEL), wr_split, wr_hi)


LANE = 128
NGRP = SEQ // LANE
GRP_SHIFT = 5
assert 1 << GRP_SHIFT == NGRP
ER = N_EXPERTS * NGRP
TINY = float(np.finfo(np.float32).tiny)
N_BISECT = 36
N_MCOL = 8
EXPERTS_PER_ITER = 4


def _select_kernel(lg_ref, idx_ref, gate_ref, enc_ref, m_sc, ci_sc, cg_sc):
    lg = lg_ref[...]
    ex = jnp.exp(lg - jnp.max(lg, axis=0, keepdims=True))
    p = ex / jnp.sum(ex, axis=0, keepdims=True)

    def total(v):
        return jnp.sum(jnp.sum(v, axis=2, keepdims=True), axis=1, keepdims=True)

    def bisect(_, lohi):
        lo, hi = lohi
        mid = jnp.sqrt(jnp.maximum(lo, TINY)) * jnp.sqrt(hi)
        ok = total(jnp.where(p >= mid, 1.0, 0.0)) >= CAP
        return jnp.where(ok, mid, lo), jnp.where(ok, hi, mid)

    lo0 = jnp.zeros((N_EXPERTS, 1, 1), F32)
    hi0 = jnp.full((N_EXPERTS, 1, 1), 2.0, F32)
    _, hi = lax.fori_loop(0, N_BISECT, bisect, (lo0, hi0))
    below = jnp.where(p < hi, p, -1.0)
    thr = jnp.max(jnp.max(below, axis=2, keepdims=True), axis=1, keepdims=True)
    gt = p > thr
    eq = p == thr
    need = CAP - total(jnp.where(gt, 1.0, 0.0))

    r_i = lax.broadcasted_iota(jnp.int32, (ER, ER), 0)
    c_i = lax.broadcasted_iota(jnp.int32, (ER, ER), 1)
    same_expert = (c_i >> GRP_SHIFT) == (r_i >> GRP_SHIFT)
    rows_before = jnp.where(jnp.logical_and(c_i < r_i, same_expert), 1.0, 0.0).astype(BF16)
    j_i = lax.broadcasted_iota(jnp.int32, (LANE, LANE), 0)
    l_i = lax.broadcasted_iota(jnp.int32, (LANE, LANE), 1)
    lanes_upto = jnp.where(j_i <= l_i, 1.0, 0.0).astype(BF16)
    ones = jnp.ones((LANE, LANE), BF16)

    def prefix(mask_b):
        within = _dot(mask_b, lanes_upto)
        rowtot = _dot(mask_b, ones)
        rowoff = _dot(rows_before, rowtot.astype(BF16))
        return within, rowoff

    eq_f = jnp.where(eq, 1.0, 0.0)
    w_eq, ro_eq = prefix(eq_f.astype(BF16).reshape(ER, LANE))
    eq_before = (w_eq + ro_eq).reshape(N_EXPERTS, NGRP, LANE) - eq_f
    sel = jnp.logical_or(gt, jnp.logical_and(eq, eq_before < need))
    sel_b = jnp.where(sel, 1.0, 0.0).astype(BF16).reshape(ER, LANE)
    within, rowoff = prefix(sel_b)
    count = within + rowoff
    enc_ref[0] = jnp.where(sel_b > 0, count, -count).astype(jnp.int32)

    half = jnp.floor(rowoff * 0.5)
    p2d = p.reshape(ER, LANE)
    p_1 = p2d.astype(BF16)
    rem = p2d - p_1.astype(F32)
    p_2 = rem.astype(BF16)
    p_3 = (rem - p_2.astype(F32)).astype(BF16)
    grp = (lax.broadcasted_iota(jnp.int32, (ER, LANE), 0) & (NGRP - 1)).astype(F32)
    blocks = [within.astype(BF16), grp.astype(BF16), half.astype(BF16), (rowoff - 2.0 * half).astype(BF16),
              p_1, p_2, p_3, sel_b]
    for k, blk in enumerate(blocks):
        m_sc[:, k * LANE:(k + 1) * LANE] = blk
    ci_sc[...] = jnp.zeros_like(ci_sc)
    cg_sc[...] = jnp.zeros_like(cg_sc)

    s_col = lax.broadcasted_iota(jnp.int32, (CAP, 1), 0).astype(F32)
    lane = lax.broadcasted_iota(jnp.int32, (CAP, LANE), 1)
    lane_f = lane.astype(F32)
    ones8 = jnp.ones((8, LANE), BF16)
    g_r = lax.broadcasted_iota(jnp.int32, (NGRP, NGRP), 0)
    g_c = lax.broadcasted_iota(jnp.int32, (NGRP, NGRP), 1)
    groups_before = jnp.where(g_r < g_c, 1.0, 0.0).astype(BF16)

    def one_expert(e):
        r0 = pl.multiple_of(e * NGRP, NGRP)
        table = m_sc[pl.ds(r0, NGRP), :]
        sel_e = table[:, 7 * LANE:]
        rt = lax.dot_general(ones8, sel_e, (((1,), (1,)), ((), ())), preferred_element_type=F32)
        ro = _dot(rt.astype(BF16), groups_before)
        start = ro[0:1]
        stop = start + rt[0:1]
        in_grp = jnp.logical_and(start <= s_col, s_col < stop)
        got = _dot(jnp.where(in_grp, 1.0, 0.0).astype(BF16), table[:, :7 * LANE])
        s_loc = s_col - (2.0 * got[:, 2 * LANE:3 * LANE] + got[:, 3 * LANE:4 * LANE])
        off = _dot(jnp.where(got[:, :LANE] <= s_loc, 1.0, 0.0).astype(BF16), ones)
        tok = got[:, LANE:2 * LANE] * LANE + off
        hit = lane_f == off
        gate = jnp.zeros((CAP, LANE), F32)
        for k in (4, 5, 6):
            gate = gate + _dot(jnp.where(hit, got[:, k * LANE:(k + 1) * LANE], 0.0).astype(BF16), ones)
        return tok, gate

    def per_group(i, carry):
        ci = ci_sc[...]
        cg = cg_sc[...]
        for k in range(EXPERTS_PER_ITER):
            e = i * EXPERTS_PER_ITER + k
            tok, gate = one_expert(e)
            ci = jnp.where(lane == e, tok, ci)
            cg = jnp.where(lane == e, gate, cg)
        ci_sc[...] = ci
        cg_sc[...] = cg
        return carry

    lax.fori_loop(0, N_EXPERTS // EXPERTS_PER_ITER, per_group, 0)
    idx_ref[0] = ci_sc[...].T[:N_EXPERTS].astype(jnp.int32)
    gate_ref[0] = cg_sc[...].T[:N_EXPERTS]


def _select(lg4):
    return pl.pallas_call(
        _select_kernel,
        out_shape=(
            jax.ShapeDtypeStruct((BATCH, N_EXPERTS, CAP), jnp.int32),
            jax.ShapeDtypeStruct((BATCH, N_EXPERTS, CAP), F32),
            jax.ShapeDtypeStruct((BATCH, ER, LANE), jnp.int32),
        ),
        grid=(BATCH,),
        in_specs=[pl.BlockSpec((N_EXPERTS, None, NGRP, LANE), lambda b: (0, b, 0, 0))],
        out_specs=(
            pl.BlockSpec((1, N_EXPERTS, CAP), lambda b: (b, 0, 0)),
            pl.BlockSpec((1, N_EXPERTS, CAP), lambda b: (b, 0, 0)),
            pl.BlockSpec((1, ER, LANE), lambda b: (b, 0, 0)),
        ),
        scratch_shapes=[
            pltpu.VMEM((ER, N_MCOL * LANE), BF16),
            pltpu.VMEM((CAP, LANE), F32),
            pltpu.VMEM((CAP, LANE), F32),
        ],
        compiler_params=_cp(("parallel",)),
        name="select_topc",
    )(lg4)


TF = 512
TD = 256
NF = D_EXPERT // TF
ND = D_MODEL // TD
M_CHUNK = 512
SUBLANES = 8
GATHER_ROWS = 256


def _ffn_kernel(idx_ref, meta_ref, h_hbm, wg_ref, wu_ref, wd_ref, o_ref, stage, xs, hid, sem):
    j = pl.program_id(1)

    @pl.when(j == 0)
    def _gather():
        def start_rows(u):
            buf = u % 2

            def issue(grp, c):
                base = pl.multiple_of(grp * SUBLANES, SUBLANES)
                for k in range(SUBLANES):
                    row = idx_ref[0, 0, u * GATHER_ROWS + base + k]
                    pltpu.make_async_copy(h_hbm.at[pl.ds(row, 1), :], stage.at[buf, pl.ds(base + k, 1), :],
                                          sem.at[buf]).start()
                return c

            lax.fori_loop(0, GATHER_ROWS // SUBLANES, issue, 0)

        n_units = SLOTS // GATHER_ROWS
        start_rows(0)
        for u in range(n_units):
            if u + 1 < n_units:
                start_rows(u + 1)
            buf = u % 2
            pltpu.make_async_copy(h_hbm.at[pl.ds(0, GATHER_ROWS), :], stage.at[buf], sem.at[buf]).wait()
            xs[u * GATHER_ROWS:(u + 1) * GATHER_ROWS, :] = stage[buf].astype(BF16)

    @pl.when(j < NF)
    def _up():
        wg = wg_ref[...].astype(BF16)
        wu = wu_ref[...].astype(BF16)
        for c in range(SLOTS // M_CHUNK):
            rows = slice(c * M_CHUNK, (c + 1) * M_CHUNK)
            x = xs[rows, :]
            g = _dot(x, wg)
            u = _dot(x, wu)
            hid[j, rows, :] = ((g * jax.nn.sigmoid(g)) * u).astype(BF16)

    @pl.when(jnp.logical_and(j >= NF, j < NF + ND))
    def _down():
        wd = wd_ref[...].astype(BF16)
        for c in range(SLOTS // M_CHUNK):
            rows = slice(c * M_CHUNK, (c + 1) * M_CHUNK)
            hrows = jnp.concatenate([hid[k, rows, :] for k in range(NF)], axis=1)
            o_ref[0, rows, :] = (_dot(hrows, wd) * meta_ref[0, rows, 0:1]).astype(o_ref.dtype)

    @pl.when(j == NF + ND)
    def _token_id():
        meta = meta_ref[0]
        lane = lax.broadcasted_iota(jnp.int32, (SLOTS, TD), 1)
        o_ref[0] = jnp.where(lane < TD // 2, meta[:, 1:2], meta[:, 2:3]).astype(o_ref.dtype)


Y_WIDTH = D_MODEL + TD
N_META = 4


def _ffn(idx_rows, meta, h2, w_gate, w_up, w_down, layer):
    up_blk = lambda e, j: (layer, e, 0, jnp.minimum(j, NF - 1))
    dn_blk = lambda e, j: (layer, e, 0, jnp.clip(j - NF, 0, ND - 1))
    return pl.pallas_call(
        _ffn_kernel,
        out_shape=jax.ShapeDtypeStruct((N_EXPERTS, SLOTS, Y_WIDTH), BF16),
        grid=(N_EXPERTS, NF + ND + 1),
        in_specs=[
            pl.BlockSpec((1, 1, SLOTS), lambda e, j: (e, 0, 0), memory_space=pltpu.SMEM),
            pl.BlockSpec((1, SLOTS, N_META), lambda e, j: (e, 0, 0)),
            pl.BlockSpec(memory_space=pl.ANY),
            pl.BlockSpec((None, None, D_MODEL, TF), up_blk),
            pl.BlockSpec((None, None, D_MODEL, TF), up_blk),
            pl.BlockSpec((None, None, D_EXPERT, TD), dn_blk),
        ],
        out_specs=pl.BlockSpec((1, SLOTS, TD), lambda e, j: (e, 0, jnp.maximum(j - NF, 0))),
        scratch_shapes=[
            pltpu.VMEM((2, GATHER_ROWS, D_MODEL), F32),
            pltpu.VMEM((SLOTS, D_MODEL), BF16),
            pltpu.VMEM((NF, SLOTS, TF), BF16),
            pltpu.SemaphoreType.DMA((2,)),
        ],
        compiler_params=_cp(("arbitrary", "arbitrary")),
        name="expert_swiglu",
    )(idx_rows, meta, h2, w_gate, w_up, w_down)


TB_C = 256
NB_C = SEQ // TB_C
ROW_ALIGN = 16
ROW_SHIFT = 4
KC = 512
KC_SHIFT = 9
assert 1 << ROW_SHIFT == ROW_ALIGN and 1 << KC_SHIFT == KC
N_PHASE = 2
E_PER_PHASE = N_EXPERTS // N_PHASE
STAGE_MAX = E_PER_PHASE * (TB_C + ROW_ALIGN)
STAGE_ROWS = -(-STAGE_MAX // KC) * KC
N_STEPS_C = BATCH * NB_C


def _combine_kernel(bnd_ref, x_ref, g_ref, y_hbm, o_ref, stage, sem, rows_sm, *, final_norm):
    b = pl.program_id(0)
    tb = pl.program_id(1)
    step = b * NB_C + tb

    def chunk_copy(e, src, buf, dst):
        return pltpu.make_async_copy(y_hbm.at[e, pl.ds(src, ROW_ALIGN), :],
                                     stage.at[buf, pl.ds(dst, ROW_ALIGN), :], sem.at[buf])

    def issue(bq, tq, ph):
        pos = jnp.int32(0)
        for e in range(ph * E_PER_PHASE, (ph + 1) * E_PER_PHASE):
            lo = bnd_ref[bq, tq, e]
            hi = bnd_ref[bq, tq + 1, e]
            lo_al = lo - (lo & (ROW_ALIGN - 1))
            nch = jnp.where(hi > lo, lax.shift_right_logical(hi - lo_al + (ROW_ALIGN - 1), ROW_SHIFT), 0)

            def start(c, carry, e=e, lo_al=lo_al, pos=pos):
                src = pl.multiple_of(bq * CAP + lo_al + c * ROW_ALIGN, ROW_ALIGN)
                dst = pl.multiple_of(pos + c * ROW_ALIGN, ROW_ALIGN)
                chunk_copy(e, src, ph, dst).start()
                return carry

            lax.fori_loop(0, nch, start, 0)
            pos = pos + nch * ROW_ALIGN
        rows_sm[ph] = pos

    @pl.when(step == 0)
    def _first():
        stage[...] = jnp.zeros_like(stage)
        issue(b, tb, 0)

    o_ref[...] = x_ref[...]
    t0 = (tb * TB_C).astype(F32)
    lane = lax.broadcasted_iota(jnp.int32, (KC, LANE), 1).astype(F32)
    krow = lax.broadcasted_iota(jnp.int32, (KC, LANE), 0)

    for ph in range(N_PHASE):
        if ph + 1 < N_PHASE:
            issue(b, tb, ph + 1)
        else:
            @pl.when(step + 1 < N_STEPS_C)
            def _prefetch():
                wrap = tb + 1 == NB_C
                issue(jnp.where(wrap, b + 1, b), jnp.where(wrap, 0, tb + 1), 0)

        rows = rows_sm[ph]

        def drain(c, carry, ph=ph):
            chunk_copy(0, 0, ph, 0).wait()
            return carry

        lax.fori_loop(0, lax.shift_right_logical(rows, ROW_SHIFT), drain, 0)

        def accumulate(kc, carry, ph=ph, rows=rows):
            k0 = pl.multiple_of(kc * KC, KC)
            blk = stage[ph, pl.ds(k0, KC), :]
            tok = blk[:, D_MODEL:D_MODEL + LANE].astype(F32) * 64.0 + blk[:, D_MODEL + LANE:].astype(F32) - t0
            live = krow + k0 < rows
            hits = [jnp.logical_and(live, tok - float(q * LANE) == lane) for q in range(TB_C // LANE)]
            onehot_t = jnp.where(jnp.concatenate(hits, axis=1), 1.0, 0.0).astype(BF16)
            o_ref[...] += lax.dot_general(onehot_t, blk[:, :D_MODEL], (((0,), (0,)), ((), ())),
                                          preferred_element_type=F32)
            return carry

        lax.fori_loop(0, lax.shift_right_logical(rows + (KC - 1), KC_SHIFT), accumulate, 0)

    if final_norm:
        x = o_ref[...]
        inv = lax.rsqrt(jnp.mean(x * x, axis=-1, keepdims=True) + RMS_EPS)
        o_ref[...] = (x * inv) * g_ref[...]


def _combine(bnd, x1_2d, gamma, ysg, final_norm):
    grid_spec = pltpu.PrefetchScalarGridSpec(
        num_scalar_prefetch=1,
        grid=(BATCH, NB_C),
        in_specs=[
            pl.BlockSpec((TB_C, D_MODEL), lambda b, t, bnd: (b * NB_C + t, 0)),
            pl.BlockSpec((1, D_MODEL), lambda b, t, bnd: (0, 0)),
            pl.BlockSpec(memory_space=pl.ANY),
        ],
        out_specs=pl.BlockSpec((TB_C, D_MODEL), lambda b, t, bnd: (b * NB_C + t, 0)),
        scratch_shapes=[
            pltpu.VMEM((N_PHASE, STAGE_ROWS, Y_WIDTH), BF16),
            pltpu.SemaphoreType.DMA((N_PHASE,)),
            pltpu.SMEM((N_PHASE,), jnp.int32),
        ],
    )
    return pl.pallas_call(
        functools.partial(_combine_kernel, final_norm=final_norm),
        out_shape=jax.ShapeDtypeStruct((ROWS, D_MODEL), F32),
        grid_spec=grid_spec,
        compiler_params=_cp(("arbitrary", "arbitrary")),
        name="combine_final" if final_norm else "combine",
    )(bnd, x1_2d, gamma.reshape(1, D_MODEL), ysg)


def kernel(x, norm1_g, w_in, w_fourier, w_pool, pool_scale, w_out, norm2_g, w_router, w_gate, w_up, w_down, final_g):
    wcat = _fold_in_weights(w_in, w_pool, pool_scale)
    mc, ms = _fold_head_weights(w_fourier)
    ctab, stab = _dft_tables()
    w_out_bf = w_out.astype(BF16)
    w_router_t = jnp.swapaxes(w_router, 1, 2)
    wr_hi = w_router_t.astype(BF16)
    wr_lo = (w_router_t - wr_hi.astype(F32)).astype(BF16)
    wr_split = jnp.concatenate([wr_hi, wr_lo], axis=1)
    batch_base = (jnp.arange(BATCH, dtype=jnp.int32) * SEQ)[:, None, None]

    xc = x.reshape(ROWS, D_MODEL)
    for layer in range(DEPTH):
        z = _norm_mm(xc, norm1_g[layer], wcat, layer)
        z3 = z.reshape(BATCH, SEQ, Z_WIDTH)
        ya_lo, ya_hi = _dft(ctab, stab, z3, mc, ms, layer)
        yb = _pool(z3)
        x1, h2, lg = _out_proj(ya_lo, ya_hi, yb.reshape(ROWS, POOL_WIDTH), w_out_bf, xc,
                               norm2_g[layer], wr_split[layer], wr_hi[layer], layer)
        idx, gates, enc = _select(lg.reshape(N_EXPERTS, BATCH, NGRP, LANE))

        idx_rows = jnp.swapaxes(idx + batch_base, 0, 1).reshape(N_EXPERTS, 1, SLOTS)
        meta = jnp.stack([gates, (idx >> 6).astype(F32), (idx & 63).astype(F32), jnp.zeros_like(gates)], axis=-1)
        meta = jnp.swapaxes(meta, 0, 1).reshape(N_EXPERTS, SLOTS, N_META)
        counts = jnp.abs(enc).reshape(BATCH, N_EXPERTS, SEQ)
        ends = counts[:, :, TB_C - 1::TB_C]
        bnd = jnp.concatenate([jnp.zeros((BATCH, N_EXPERTS, 1), jnp.int32), ends], axis=2)
        bnd = jnp.swapaxes(bnd, 1, 2)

        ysg = _ffn(idx_rows, meta, h2, w_gate, w_up, w_down, layer)
        xc = _combine(bnd, x1, final_g, ysg, layer == DEPTH - 1)
    return xc.reshape(BATCH, SEQ, D_MODEL)
```

```python
import functools

import numpy as np
import jax
import jax.numpy as jnp
from jax import lax
from jax.experimental import pallas as pl
from jax.experimental.pallas import tpu as pltpu

D_MODEL = 2048
BATCH = 4
SEQ = 4096
DEPTH = 2
N_HEADS = 4
HEAD_DIM = 256
POOL_WINDOWS = (2, 4, 8, 16)
N_GROUPS = 4
GROUP_DIM = 256
FOURIER_WIDTH = N_HEADS * HEAD_DIM
POOL_WIDTH = N_GROUPS * GROUP_DIM
N_EXPERTS = 16
CAP = 2 * SEQ // N_EXPERTS
SLOTS = BATCH * CAP
D_EXPERT = D_MODEL
RMS_EPS = 1e-6
ROWS = BATCH * SEQ

F32 = jnp.float32
BF16 = jnp.bfloat16
HIGHEST = lax.Precision.HIGHEST

VMEM_LIMIT = 54 * 1024 * 1024


def _cp(sem, vmem=VMEM_LIMIT):
    return pltpu.CompilerParams(dimension_semantics=sem, vmem_limit_bytes=vmem)


def _dot(a, b):
    return jnp.dot(a, b, preferred_element_type=F32)


N_FOLD = N_HEADS + N_GROUPS
Z_WIDTH = N_FOLD * HEAD_DIM


def _fold_in_kernel(win_ref, r_ref, sc_ref, o_ref):
    j = pl.program_id(1)

    @pl.when(j < N_HEADS)
    def _fourier():
        o_ref[0] = win_ref[0].astype(BF16)

    @pl.when(j >= N_HEADS)
    def _pool():
        t = r_ref[0, 0] * sc_ref[0, 0]
        o_ref[0] = jnp.dot(win_ref[0], t, precision=HIGHEST, preferred_element_type=F32).astype(BF16)


def _fold_in_weights(w_in, w_pool, pool_scale):
    grp = lambda j: jnp.maximum(j - N_HEADS, 0)
    return pl.pallas_call(
        _fold_in_kernel,
        out_shape=jax.ShapeDtypeStruct((DEPTH, D_MODEL, Z_WIDTH), BF16),
        grid=(DEPTH, N_FOLD),
        in_specs=[
            pl.BlockSpec((1, D_MODEL, HEAD_DIM), lambda l, j: (l, 0, j)),
            pl.BlockSpec((1, 1, GROUP_DIM, GROUP_DIM), lambda l, j: (l, grp(j), 0, 0)),
            pl.BlockSpec((1, 1, 1, GROUP_DIM), lambda l, j: (l, grp(j), 0, 0)),
        ],
        out_specs=pl.BlockSpec((1, D_MODEL, HEAD_DIM), lambda l, j: (l, 0, j)),
        compiler_params=_cp(("parallel", "parallel")),
        name="fold_in_weights",
    )(w_in, w_pool, pool_scale.reshape(DEPTH, N_GROUPS, 1, GROUP_DIM))


def _fold_head_kernel(ct_ref, st_ref, wf_ref, mc_ref, ms_ref):
    wf = wf_ref[0, 0]
    mc_ref[0, 0] = jnp.dot(ct_ref[...], wf, precision=HIGHEST, preferred_element_type=F32).astype(BF16)
    ms_ref[0, 0] = jnp.dot(st_ref[...], wf, precision=HIGHEST, preferred_element_type=F32).astype(BF16)


def _fold_head_weights(w_fourier):
    c = np.arange(HEAD_DIM)
    ang = 2.0 * np.pi * ((c[:, None] * c[None, :]) % HEAD_DIM) / HEAD_DIM
    ctab = jnp.asarray((np.cos(ang) / np.sqrt(HEAD_DIM)).astype(np.float32))
    stab = jnp.asarray((np.sin(ang) / np.sqrt(HEAD_DIM)).astype(np.float32))
    tab = pl.BlockSpec((HEAD_DIM, HEAD_DIM), lambda l, h: (0, 0))
    blk = pl.BlockSpec((1, 1, HEAD_DIM, HEAD_DIM), lambda l, h: (l, h, 0, 0))
    out = jax.ShapeDtypeStruct((DEPTH, N_HEADS, HEAD_DIM, HEAD_DIM), BF16)
    return pl.pallas_call(
        _fold_head_kernel,
        out_shape=(out, out),
        grid=(DEPTH, N_HEADS),
        in_specs=[tab, tab, blk],
        out_specs=(blk, blk),
        compiler_params=_cp(("parallel", "parallel")),
        name="fold_head_weights",
    )(ctab, stab, w_fourier)


TM_IN = 512


def _norm_mm_kernel(x_ref, g_ref, w_ref, o_ref):
    x = x_ref[...]
    inv = lax.rsqrt(jnp.mean(x * x, axis=-1, keepdims=True) + RMS_EPS)
    h = ((x * inv) * g_ref[...]).astype(BF16)
    o_ref[...] = _dot(h, w_ref[0]).astype(o_ref.dtype)


def _norm_mm(x2d, gamma, wcat, layer):
    return pl.pallas_call(
        _norm_mm_kernel,
        out_shape=jax.ShapeDtypeStruct((ROWS, Z_WIDTH), BF16),
        grid=(ROWS // TM_IN,),
        in_specs=[
            pl.BlockSpec((TM_IN, D_MODEL), lambda i: (i, 0)),
            pl.BlockSpec((1, D_MODEL), lambda i: (0, 0)),
            pl.BlockSpec((1, D_MODEL, Z_WIDTH), lambda i: (layer, 0, 0)),
        ],
        out_specs=pl.BlockSpec((TM_IN, Z_WIDTH), lambda i: (i, 0)),
        compiler_params=_cp(("parallel",)),
        name="norm_in_proj",
    )(x2d, gamma.reshape(1, D_MODEL), wcat)


TM_DFT = 512
TN_DFT = 512
HALF_SEQ = SEQ // 2
N_KBLK = HALF_SEQ // TM_DFT
TAB_GROUP = 16
KX = TM_DFT + TAB_GROUP
COARSE = 64
N_KH = 40


def _table_kernel(ac_ref, as_ref, bc_ref, bs_ref, c_ref, s_ref):
    i = pl.program_id(0)

    def group(gi, carry):
        r0 = pl.multiple_of(gi * TAB_GROUP, TAB_GROUP)
        k0 = i * TM_DFT + r0
        kh = lax.shift_right_logical(k0, 6)
        kl = pl.multiple_of(k0 & (COARSE - 1), TAB_GROUP)
        ca = ac_ref[pl.ds(kh, 1), :]
        sa = as_ref[pl.ds(kh, 1), :]
        cb = bc_ref[pl.ds(kl, TAB_GROUP), :]
        sb = bs_ref[pl.ds(kl, TAB_GROUP), :]
        c_ref[0, pl.ds(r0, TAB_GROUP), :] = (ca * cb - sa * sb).astype(BF16)
        s_ref[0, pl.ds(r0, TAB_GROUP), :] = (sa * cb + ca * sb).astype(BF16)
        return carry

    lax.fori_loop(0, KX // TAB_GROUP, group, 0)


def _dft_tables():
    n = jnp.arange(SEQ, dtype=jnp.int32)[None, :]
    kh = jnp.arange(N_KH, dtype=jnp.int32)[:, None]
    kl = jnp.arange(COARSE, dtype=jnp.int32)[:, None]
    alpha = ((kh * n) % (SEQ // COARSE)).astype(F32) * (2.0 * np.pi * COARSE / SEQ)
    beta = ((kl * n) % SEQ).astype(F32) * (2.0 * np.pi / SEQ)
    scale = 1.0 / np.sqrt(SEQ)
    full = lambda rows: pl.BlockSpec((rows, SEQ), lambda i: (0, 0))
    out = pl.BlockSpec((1, KX, SEQ), lambda i: (i, 0, 0))
    return pl.pallas_call(
        _table_kernel,
        out_shape=(jax.ShapeDtypeStruct((N_KBLK, KX, SEQ), BF16),) * 2,
        grid=(N_KBLK,),
        in_specs=[full(N_KH), full(N_KH), full(COARSE), full(COARSE)],
        out_specs=(out, out),
        compiler_params=_cp(("parallel",)),
        name="dft_tables",
    )(jnp.cos(alpha), jnp.sin(alpha), jnp.cos(beta) * scale, jnp.sin(beta) * scale)


HEADS_PER_BLK = TN_DFT // HEAD_DIM


def _dft_kernel(c_ref, s_ref, u_ref, mc_ref, ms_ref, lo_ref, hi_ref):
    u = u_ref[0]
    pc = _dot(c_ref[0], u).astype(BF16)
    ps = _dot(s_ref[0], u).astype(BF16)
    heads = lambda v, m_ref: jnp.concatenate(
        [_dot(v[:, h * HEAD_DIM:(h + 1) * HEAD_DIM], m_ref[h]) for h in range(HEADS_PER_BLK)], axis=1)
    a = heads(pc, mc_ref)
    bq = heads(ps, ms_ref)
    lo_ref[0] = (a[:TM_DFT] - bq[:TM_DFT]).astype(BF16)
    mirrored = (a + bq).astype(BF16)
    u = lax.broadcasted_iota(jnp.int32, (TM_DFT, KX), 0)
    r = lax.broadcasted_iota(jnp.int32, (TM_DFT, KX), 1)
    flip = jnp.where(r == TM_DFT - u, 1.0, 0.0).astype(BF16)
    hi_ref[0] = _dot(flip, mirrored).astype(BF16)


def _dft(ctab, stab, z3, mc, ms, layer):
    nq = FOURIER_WIDTH // TN_DFT
    half = jax.ShapeDtypeStruct((BATCH, HALF_SEQ, FOURIER_WIDTH), BF16)
    head_blk = pl.BlockSpec((None, HEADS_PER_BLK, HEAD_DIM, HEAD_DIM), lambda i, b, n: (layer, n, 0, 0))
    return pl.pallas_call(
        _dft_kernel,
        out_shape=(half, half),
        grid=(N_KBLK, BATCH, nq),
        in_specs=[
            pl.BlockSpec((1, KX, SEQ), lambda i, b, n: (i, 0, 0)),
            pl.BlockSpec((1, KX, SEQ), lambda i, b, n: (i, 0, 0)),
            pl.BlockSpec((1, SEQ, TN_DFT), lambda i, b, n: (b, 0, n)),
            head_blk,
            head_blk,
        ],
        out_specs=(
            pl.BlockSpec((1, TM_DFT, TN_DFT), lambda i, b, n: (b, i, n)),
            pl.BlockSpec((1, TM_DFT, TN_DFT), lambda i, b, n: (b, N_KBLK - 1 - i, n)),
        ),
        compiler_params=_cp(("parallel", "parallel", "parallel")),
        name="position_dft",
    )(ctab, stab, z3, mc, ms)


TM_OUT = 512


BLK_PER_HALF = HALF_SEQ // TM_OUT
BLK_PER_SEQ = SEQ // TM_OUT


HALO = 16
assert HALO >= max(POOL_WINDOWS) // 2


def _pool_minus_identity(prev, cur, nxt, t0):
    xp = jnp.concatenate([prev, cur, nxt], axis=0)
    t = t0 + lax.broadcasted_iota(jnp.int32, (TM_OUT, 1), 0)
    parts = []
    for gi, w in enumerate(POOL_WINDOWS):
        cols = slice(gi * GROUP_DIM, (gi + 1) * GROUP_DIM)
        a, span = xp[:, cols], 1
        while span < w:
            n = a.shape[0] - span
            a = a[:n] + a[span:span + n]
            span *= 2
        start = HALO - w // 2
        cnt = (jnp.minimum(t + (w - w // 2), SEQ) - jnp.maximum(t - w // 2, 0)).astype(F32)
        parts.append(a[start:start + TM_OUT] / cnt - cur[:, cols])
    return jnp.concatenate(parts, axis=1)


def _out_proj_kernel(ylo_ref, yhi_ref, v_ref, vp_ref, vn_ref, wa_ref, wb_ref, x_ref, g_ref, wrs_ref, wrh_ref,
                     o_ref, h_ref, lg_ref):
    blk = pl.program_id(0) % BLK_PER_SEQ
    ya = jnp.where(blk >= BLK_PER_HALF, yhi_ref[...], ylo_ref[...])
    prev = jnp.where(blk > 0, vp_ref[...].astype(F32), 0.0)
    nxt = jnp.where(blk < BLK_PER_SEQ - 1, vn_ref[...].astype(F32), 0.0)
    yb = _pool_minus_identity(prev, v_ref[...].astype(F32), nxt, blk * TM_OUT).astype(BF16)
    x1 = x_ref[...] + _dot(ya, wa_ref[0]) + _dot(yb, wb_ref[0])
    o_ref[...] = x1
    inv = lax.rsqrt(jnp.mean(x1 * x1, axis=-1, keepdims=True) + RMS_EPS)
    h = (x1 * inv) * g_ref[...]
    h_ref[...] = h
    h_hi = h.astype(BF16)
    h_lo = (h - h_hi.astype(F32)).astype(BF16)
    contract_last = (((1,), (1,)), ((), ()))
    a = lax.dot_general(wrs_ref[...], h_hi, contract_last, preferred_element_type=F32)
    b = lax.dot_general(wrh_ref[...], h_lo, contract_last, preferred_element_type=F32)
    lg_ref[...] = a[:N_EXPERTS] + (a[N_EXPERTS:] + b)


def _out_proj(ya_lo, ya_hi, z2d, w_out_bf, x2d, gamma, wr_split, wr_hi, layer):
    row_blk = lambda i: (i, 0)
    fixed = lambda i: (0, 0)
    half_blk = lambda i: ((i // BLK_PER_SEQ) * BLK_PER_HALF + i % BLK_PER_HALF, 0)
    v_col = FOURIER_WIDTH // POOL_WIDTH
    halo_per_blk = TM_OUT // HALO
    halo_prev = lambda i: (jnp.maximum(i * halo_per_blk - 1, 0), v_col)
    halo_next = lambda i: (jnp.minimum((i + 1) * halo_per_blk, ROWS // HALO - 1), v_col)
    return pl.pallas_call(
        _out_proj_kernel,
        out_shape=(
            jax.ShapeDtypeStruct((ROWS, D_MODEL), F32),
            jax.ShapeDtypeStruct((ROWS, D_MODEL), F32),
            jax.ShapeDtypeStruct((N_EXPERTS, ROWS), F32),
        ),
        grid=(ROWS // TM_OUT,),
        in_specs=[
            pl.BlockSpec((TM_OUT, FOURIER_WIDTH), half_blk),
            pl.BlockSpec((TM_OUT, FOURIER_WIDTH), half_blk),
            pl.BlockSpec((TM_OUT, POOL_WIDTH), lambda i: (i, v_col)),
            pl.BlockSpec((HALO, POOL_WIDTH), halo_prev),
            pl.BlockSpec((HALO, POOL_WIDTH), halo_next),
            pl.BlockSpec((1, FOURIER_WIDTH, D_MODEL), lambda i: (layer, 0, 0)),
            pl.BlockSpec((1, POOL_WIDTH, D_MODEL), lambda i: (layer, 1, 0)),
            pl.BlockSpec((TM_OUT, D_MODEL), row_blk),
            pl.BlockSpec((1, D_MODEL), fixed),
            pl.BlockSpec((2 * N_EXPERTS, D_MODEL), fixed),
            pl.BlockSpec((N_EXPERTS, D_MODEL), fixed),
        ],
        out_specs=(
            pl.BlockSpec((TM_OUT, D_MODEL), row_blk),
            pl.BlockSpec((TM_OUT, D_MODEL), row_blk),
            pl.BlockSpec((N_EXPERTS, TM_OUT), lambda i: (0, i)),
        ),
        compiler_params=_cp(("parallel",)),
        name="out_proj_residual",
    )(ya_lo.reshape(BATCH * HALF_SEQ, FOURIER_WIDTH), ya_hi.reshape(BATCH * HALF_SEQ, FOURIER_WIDTH),
      z2d, z2d, z2d, w_out_bf, w_out_bf, x2d, gamma.reshape(1, D_MODEL), wr_split, wr_hi)


LANE = 128
NGRP = SEQ // LANE
GRP_SHIFT = 5
assert 1 << GRP_SHIFT == NGRP
ER = N_EXPERTS * NGRP
TINY = float(np.finfo(np.float32).tiny)
N_BISECT = 36
N_MCOL = 8
EXPERTS_PER_ITER = 4


def _select_kernel(lg_ref, idx_ref, gate_ref, enc_ref, m_sc, ci_sc, cg_sc):
    lg = lg_ref[...]
    ex = jnp.exp(lg - jnp.max(lg, axis=0, keepdims=True))
    p = ex / jnp.sum(ex, axis=0, keepdims=True)

    def total(v):
        return jnp.sum(jnp.sum(v, axis=1, keepdims=True), axis=2, keepdims=True)

    def bisect(_, lohi):
        lo, hi = lohi
        mid = jnp.sqrt(jnp.maximum(lo, TINY)) * jnp.sqrt(hi)
        ok = total(jnp.where(p >= mid, 1.0, 0.0)) >= CAP
        return jnp.where(ok, mid, lo), jnp.where(ok, hi, mid)

    lo0 = jnp.zeros((N_EXPERTS, 1, 1), F32)
    hi0 = jnp.full((N_EXPERTS, 1, 1), 2.0, F32)
    _, hi = lax.fori_loop(0, N_BISECT, bisect, (lo0, hi0))
    below = jnp.where(p < hi, p, -1.0)
    thr = jnp.max(jnp.max(below, axis=2, keepdims=True), axis=1, keepdims=True)
    gt = p > thr
    eq = p == thr
    need = CAP - total(jnp.where(gt, 1.0, 0.0))

    r_i = lax.broadcasted_iota(jnp.int32, (ER, ER), 0)
    c_i = lax.broadcasted_iota(jnp.int32, (ER, ER), 1)
    same_expert = (c_i >> GRP_SHIFT) == (r_i >> GRP_SHIFT)
    rows_before = jnp.where(jnp.logical_and(c_i < r_i, same_expert), 1.0, 0.0).astype(BF16)
    j_i = lax.broadcasted_iota(jnp.int32, (LANE, LANE), 0)
    l_i = lax.broadcasted_iota(jnp.int32, (LANE, LANE), 1)
    lanes_upto = jnp.where(j_i <= l_i, 1.0, 0.0).astype(BF16)
    ones = jnp.ones((LANE, LANE), BF16)

    def prefix(mask_b):
        within = _dot(mask_b, lanes_upto)
        rowtot = _dot(mask_b, ones)
        rowoff = _dot(rows_before, rowtot.astype(BF16))
        return within, rowoff

    eq_f = jnp.where(eq, 1.0, 0.0)
    w_eq, ro_eq = prefix(eq_f.astype(BF16).reshape(ER, LANE))
    eq_before = (w_eq + ro_eq).reshape(N_EXPERTS, NGRP, LANE) - eq_f
    sel = jnp.logical_or(gt, jnp.logical_and(eq, eq_before < need))
    sel_b = jnp.where(sel, 1.0, 0.0).astype(BF16).reshape(ER, LANE)
    within, rowoff = prefix(sel_b)
    count = within + rowoff
    enc_ref[0] = jnp.where(sel_b > 0, count, -count).astype(jnp.int32)

    half = jnp.floor(rowoff * 0.5)
    p2d = p.reshape(ER, LANE)
    p_1 = p2d.astype(BF16)
    rem = p2d - p_1.astype(F32)
    p_2 = rem.astype(BF16)
    p_3 = (rem - p_2.astype(F32)).astype(BF16)
    grp = (lax.broadcasted_iota(jnp.int32, (ER, LANE), 0) & (NGRP - 1)).astype(F32)
    blocks = [within.astype(BF16), grp.astype(BF16), half.astype(BF16), (rowoff - 2.0 * half).astype(BF16),
              p_1, p_2, p_3, sel_b]
    for k, blk in enumerate(blocks):
        m_sc[:, k * LANE:(k + 1) * LANE] = blk
    ci_sc[...] = jnp.zeros_like(ci_sc)
    cg_sc[...] = jnp.zeros_like(cg_sc)

    s_col = lax.broadcasted_iota(jnp.int32, (CAP, 1), 0).astype(F32)
    lane = lax.broadcasted_iota(jnp.int32, (CAP, LANE), 1)
    lane_f = lane.astype(F32)
    ones8 = jnp.ones((8, LANE), BF16)
    g_r = lax.broadcasted_iota(jnp.int32, (NGRP, NGRP), 0)
    g_c = lax.broadcasted_iota(jnp.int32, (NGRP, NGRP), 1)
    groups_before = jnp.where(g_r < g_c, 1.0, 0.0).astype(BF16)

    def one_expert(e):
        r0 = pl.multiple_of(e * NGRP, NGRP)
        table = m_sc[pl.ds(r0, NGRP), :]
        sel_e = table[:, 7 * LANE:]
        rt = lax.dot_general(ones8, sel_e, (((1,), (1,)), ((), ())), preferred_element_type=F32)
        ro = _dot(rt.astype(BF16), groups_before)
        start = ro[0:1]
        stop = start + rt[0:1]
        in_grp = jnp.logical_and(start <= s_col, s_col < stop)
        got = _dot(jnp.where(in_grp, 1.0, 0.0).astype(BF16), table[:, :7 * LANE])
        s_loc = s_col - (2.0 * got[:, 2 * LANE:3 * LANE] + got[:, 3 * LANE:4 * LANE])
        off = _dot(jnp.where(got[:, :LANE] <= s_loc, 1.0, 0.0).astype(BF16), ones)
        tok = got[:, LANE:2 * LANE] * LANE + off
        hit = lane_f == off
        gate = jnp.zeros((CAP, LANE), F32)
        for k in (4, 5, 6):
            gate = gate + _dot(jnp.where(hit, got[:, k * LANE:(k + 1) * LANE], 0.0).astype(BF16), ones)
        return tok, gate

    def per_group(i, carry):
        ci = ci_sc[...]
        cg = cg_sc[...]
        for k in range(EXPERTS_PER_ITER):
            e = i * EXPERTS_PER_ITER + k
            tok, gate = one_expert(e)
            ci = jnp.where(lane == e, tok, ci)
            cg = jnp.where(lane == e, gate, cg)
        ci_sc[...] = ci
        cg_sc[...] = cg
        return carry

    lax.fori_loop(0, N_EXPERTS // EXPERTS_PER_ITER, per_group, 0)
    idx_ref[0] = ci_sc[...].T[:N_EXPERTS].astype(jnp.int32)
    gate_ref[0] = cg_sc[...].T[:N_EXPERTS]


def _select(lg4):
    return pl.pallas_call(
        _select_kernel,
        out_shape=(
            jax.ShapeDtypeStruct((BATCH, N_EXPERTS, CAP), jnp.int32),
            jax.ShapeDtypeStruct((BATCH, N_EXPERTS, CAP), F32),
            jax.ShapeDtypeStruct((BATCH, ER, LANE), jnp.int32),
        ),
        grid=(BATCH,),
        in_specs=[pl.BlockSpec((N_EXPERTS, None, NGRP, LANE), lambda b: (0, b, 0, 0))],
        out_specs=(
            pl.BlockSpec((1, N_EXPERTS, CAP), lambda b: (b, 0, 0)),
            pl.BlockSpec((1, N_EXPERTS, CAP), lambda b: (b, 0, 0)),
            pl.BlockSpec((1, ER, LANE), lambda b: (b, 0, 0)),
        ),
        scratch_shapes=[
            pltpu.VMEM((ER, N_MCOL * LANE), BF16),
            pltpu.VMEM((CAP, LANE), F32),
            pltpu.VMEM((CAP, LANE), F32),
        ],
        compiler_params=_cp(("parallel",)),
        name="select_topc",
    )(lg4)


TF = 256
TD = 256
NF = D_EXPERT // TF
ND = D_MODEL // TD
M_CHUNK = 512
SUBLANES = 8
GATHER_ROWS = 512


def _ffn_kernel(idx_ref, meta_ref, h_hbm, wg_ref, wu_ref, wd_ref, o_ref, stage, xs, hid, sem):
    j = pl.program_id(1)

    @pl.when(j == 0)
    def _gather():
        def start_rows(u):
            buf = u % 2

            def issue(grp, c):
                base = pl.multiple_of(grp * SUBLANES, SUBLANES)
                for k in range(SUBLANES):
                    row = idx_ref[0, 0, u * GATHER_ROWS + base + k]
                    pltpu.make_async_copy(h_hbm.at[pl.ds(row, 1), :], stage.at[buf, pl.ds(base + k, 1), :],
                                          sem.at[buf]).start()
                return c

            lax.fori_loop(0, GATHER_ROWS // SUBLANES, issue, 0)

        n_units = SLOTS // GATHER_ROWS
        start_rows(0)
        for u in range(n_units):
            if u + 1 < n_units:
                start_rows(u + 1)
            buf = u % 2
            pltpu.make_async_copy(h_hbm.at[pl.ds(0, GATHER_ROWS), :], stage.at[buf], sem.at[buf]).wait()
            xs[u * GATHER_ROWS:(u + 1) * GATHER_ROWS, :] = stage[buf].astype(BF16)

        meta = meta_ref[0]
        lane = lax.broadcasted_iota(jnp.int32, (SLOTS, TD), 1)
        o_ref[0] = jnp.where(lane < TD // 2, meta[:, 1:2], meta[:, 2:3]).astype(o_ref.dtype)

    @pl.when(j < NF)
    def _up():
        wg = wg_ref[...].astype(BF16)
        wu = wu_ref[...].astype(BF16)
        for c in range(SLOTS // M_CHUNK):
            rows = slice(c * M_CHUNK, (c + 1) * M_CHUNK)
            x = xs[rows, :]
            g = _dot(x, wg)
            u = _dot(x, wu)
            hid[j, rows, :] = ((g * jax.nn.sigmoid(g)) * u).astype(BF16)

    @pl.when(j >= NF)
    def _down():
        wd = wd_ref[...].astype(BF16)
        for c in range(SLOTS // M_CHUNK):
            rows = slice(c * M_CHUNK, (c + 1) * M_CHUNK)
            hrows = jnp.concatenate([hid[k, rows, :] for k in range(NF)], axis=1)
            o_ref[0, rows, :] = (_dot(hrows, wd) * meta_ref[0, rows, 0:1]).astype(o_ref.dtype)


Y_WIDTH = D_MODEL + TD
N_META = 4


def _ffn(idx_rows, meta, h2, w_gate, w_up, w_down, layer):
    up_blk = lambda e, j: (layer, e, 0, jnp.minimum(j, NF - 1))
    dn_blk = lambda e, j: (layer, e, 0, jnp.maximum(j - NF, 0))
    out_blk = lambda e, j: (e, 0, jnp.where(j < NF, ND, j - NF))
    return pl.pallas_call(
        _ffn_kernel,
        out_shape=jax.ShapeDtypeStruct((N_EXPERTS, SLOTS, Y_WIDTH), BF16),
        grid=(N_EXPERTS, NF + ND),
        in_specs=[
            pl.BlockSpec((1, 1, SLOTS), lambda e, j: (e, 0, 0), memory_space=pltpu.SMEM),
            pl.BlockSpec((1, SLOTS, N_META), lambda e, j: (e, 0, 0)),
            pl.BlockSpec(memory_space=pl.ANY),
            pl.BlockSpec((None, None, D_MODEL, TF), up_blk),
            pl.BlockSpec((None, None, D_MODEL, TF), up_blk),
            pl.BlockSpec((None, None, D_EXPERT, TD), dn_blk),
        ],
        out_specs=pl.BlockSpec((1, SLOTS, TD), out_blk),
        scratch_shapes=[
            pltpu.VMEM((2, GATHER_ROWS, D_MODEL), F32),
            pltpu.VMEM((SLOTS, D_MODEL), BF16),
            pltpu.VMEM((NF, SLOTS, TF), BF16),
            pltpu.SemaphoreType.DMA((2,)),
        ],
        compiler_params=_cp(("arbitrary", "arbitrary")),
        name="expert_swiglu",
    )(idx_rows, meta, h2, w_gate, w_up, w_down)


TB_C = 256
NB_C = SEQ // TB_C
ROW_ALIGN = 16
ROW_SHIFT = 4
KC = 512
KC_SHIFT = 9
assert 1 << ROW_SHIFT == ROW_ALIGN and 1 << KC_SHIFT == KC
N_PHASE = 2
E_PER_PHASE = N_EXPERTS // N_PHASE
STAGE_MAX = E_PER_PHASE * (TB_C + ROW_ALIGN)
STAGE_ROWS = -(-STAGE_MAX // KC) * KC
N_STEPS_C = BATCH * NB_C


def _combine_kernel(bnd_ref, x_ref, g_ref, y_hbm, o_ref, stage, sem, rows_sm, *, final_norm):
    b = pl.program_id(0)
    tb = pl.program_id(1)
    step = b * NB_C + tb

    def chunk_copy(e, src, buf, dst):
        return pltpu.make_async_copy(y_hbm.at[e, pl.ds(src, ROW_ALIGN), :],
                                     stage.at[buf, pl.ds(dst, ROW_ALIGN), :], sem.at[buf])

    def issue(bq, tq, ph):
        pos = jnp.int32(0)
        for e in range(ph * E_PER_PHASE, (ph + 1) * E_PER_PHASE):
            lo = bnd_ref[bq, tq, e]
            hi = bnd_ref[bq, tq + 1, e]
            lo_al = lo - (lo & (ROW_ALIGN - 1))
            nch = jnp.where(hi > lo, lax.shift_right_logical(hi - lo_al + (ROW_ALIGN - 1), ROW_SHIFT), 0)

            def start(c, carry, e=e, lo_al=lo_al, pos=pos):
                src = pl.multiple_of(bq * CAP + lo_al + c * ROW_ALIGN, ROW_ALIGN)
                dst = pl.multiple_of(pos + c * ROW_ALIGN, ROW_ALIGN)
                chunk_copy(e, src, ph, dst).start()
                return carry

            lax.fori_loop(0, nch, start, 0)
            pos = pos + nch * ROW_ALIGN
        rows_sm[ph] = pos

    @pl.when(step == 0)
    def _first():
        stage[...] = jnp.zeros_like(stage)
        issue(b, tb, 0)

    o_ref[...] = x_ref[...]
    t0 = (tb * TB_C).astype(F32)
    lane = lax.broadcasted_iota(jnp.int32, (KC, LANE), 1).astype(F32)
    krow = lax.broadcasted_iota(jnp.int32, (KC, LANE), 0)

    for ph in range(N_PHASE):
        if ph + 1 < N_PHASE:
            issue(b, tb, ph + 1)
        else:
            @pl.when(step + 1 < N_STEPS_C)
            def _prefetch():
                wrap = tb + 1 == NB_C
                issue(jnp.where(wrap, b + 1, b), jnp.where(wrap, 0, tb + 1), 0)

        rows = rows_sm[ph]

        def drain(c, carry, ph=ph):
            chunk_copy(0, 0, ph, 0).wait()
            return carry

        lax.fori_loop(0, lax.shift_right_logical(rows, ROW_SHIFT), drain, 0)

        def accumulate(kc, carry, ph=ph, rows=rows):
            k0 = pl.multiple_of(kc * KC, KC)
            blk = stage[ph, pl.ds(k0, KC), :]
            tok = blk[:, D_MODEL:D_MODEL + LANE].astype(F32) * 64.0 + blk[:, D_MODEL + LANE:].astype(F32) - t0
            live = krow + k0 < rows
            hits = [jnp.logical_and(live, tok - float(q * LANE) == lane) for q in range(TB_C // LANE)]
            onehot_t = jnp.where(jnp.concatenate(hits, axis=1), 1.0, 0.0).astype(BF16)
            o_ref[...] += lax.dot_general(onehot_t, blk[:, :D_MODEL], (((0,), (0,)), ((), ())),
                                          preferred_element_type=F32)
            return carry

        lax.fori_loop(0, lax.shift_right_logical(rows + (KC - 1), KC_SHIFT), accumulate, 0)

    if final_norm:
        x = o_ref[...]
        inv = lax.rsqrt(jnp.mean(x * x, axis=-1, keepdims=True) + RMS_EPS)
        o_ref[...] = (x * inv) * g_ref[...]


def _combine(bnd, x1_2d, gamma, ysg, final_norm):
    grid_spec = pltpu.PrefetchScalarGridSpec(
        num_scalar_prefetch=1,
        grid=(BATCH, NB_C),
        in_specs=[
            pl.BlockSpec((TB_C, D_MODEL), lambda b, t, bnd: (b * NB_C + t, 0)),
            pl.BlockSpec((1, D_MODEL), lambda b, t, bnd: (0, 0)),
            pl.BlockSpec(memory_space=pl.ANY),
        ],
        out_specs=pl.BlockSpec((TB_C, D_MODEL), lambda b, t, bnd: (b * NB_C + t, 0)),
        scratch_shapes=[
            pltpu.VMEM((N_PHASE, STAGE_ROWS, Y_WIDTH), BF16),
            pltpu.SemaphoreType.DMA((N_PHASE,)),
            pltpu.SMEM((N_PHASE,), jnp.int32),
        ],
    )
    return pl.pallas_call(
        functools.partial(_combine_kernel, final_norm=final_norm),
        out_shape=jax.ShapeDtypeStruct((ROWS, D_MODEL), F32),
        grid_spec=grid_spec,
        compiler_params=_cp(("arbitrary", "arbitrary")),
        name="combine_final" if final_norm else "combine",
    )(bnd, x1_2d, gamma.reshape(1, D_MODEL), ysg)


def kernel(x, norm1_g, w_in, w_fourier, w_pool, pool_scale, w_out, norm2_g, w_router, w_gate, w_up, w_down, final_g):
    wcat = _fold_in_weights(w_in, w_pool, pool_scale)
    mc, ms = _fold_head_weights(w_fourier)
    ctab, stab = _dft_tables()
    w_out_bf = w_out.astype(BF16)
    w_router_t = jnp.swapaxes(w_router, 1, 2)
    wr_hi = w_router_t.astype(BF16)
    wr_lo = (w_router_t - wr_hi.astype(F32)).astype(BF16)
    wr_split = jnp.concatenate([wr_hi, wr_lo], axis=1)
    batch_base = (jnp.arange(BATCH, dtype=jnp.int32) * SEQ)[:, None, None]

    xc = x.reshape(ROWS, D_MODEL)
    for layer in range(DEPTH):
        z = _norm_mm(xc, norm1_g[layer], wcat, layer)
        z3 = z.reshape(BATCH, SEQ, Z_WIDTH)
        ya_lo, ya_hi = _dft(ctab, stab, z3, mc, ms, layer)
        x1, h2, lg = _out_proj(ya_lo, ya_hi, z, w_out_bf, xc,
                               norm2_g[layer], wr_split[layer], wr_hi[layer], layer)
        idx, gates, enc = _select(lg.reshape(N_EXPERTS, BATCH, NGRP, LANE))

        idx_rows = jnp.swapaxes(idx + batch_base, 0, 1).reshape(N_EXPERTS, 1, SLOTS)
        meta = jnp.stack([gates, (idx >> 6).astype(F32), (idx & 63).astype(F32), jnp.zeros_like(gates)], axis=-1)
        meta = jnp.swapaxes(meta, 0, 1).reshape(N_EXPERTS, SLOTS, N_META)
        counts = jnp.abs(enc).reshape(BATCH, N_EXPERTS, SEQ)
        ends = counts[:, :, TB_C - 1::TB_C]
        bnd = jnp.concatenate([jnp.zeros((BATCH, N_EXPERTS, 1), jnp.int32), ends], axis=2)
        bnd = jnp.swapaxes(bnd, 1, 2)

        ysg = _ffn(idx_rows, meta, h2, w_gate, w_up, w_down, layer)
        xc = _combine(bnd, x1, final_g, ysg, layer == DEPTH - 1)
    return xc.reshape(BATCH, SEQ, D_MODEL)
```

```python
import functools

import numpy as np
import jax
import jax.numpy as jnp
from jax import lax
from jax.experimental import pallas as pl
from jax.experimental.pallas import tpu as pltpu

D_MODEL = 2048
BATCH = 4
SEQ = 4096
DEPTH = 2
N_HEADS = 4
HEAD_DIM = 256
POOL_WINDOWS = (2, 4, 8, 16)
N_GROUPS = 4
GROUP_DIM = 256
FOURIER_WIDTH = N_HEADS * HEAD_DIM
POOL_WIDTH = N_GROUPS * GROUP_DIM
N_EXPERTS = 16
CAP = 2 * SEQ // N_EXPERTS
SLOTS = BATCH * CAP
D_EXPERT = D_MODEL
RMS_EPS = 1e-6
ROWS = BATCH * SEQ

F32 = jnp.float32
BF16 = jnp.bfloat16
HIGHEST = lax.Precision.HIGHEST

VMEM_LIMIT = 54 * 1024 * 1024


def _cp(sem, vmem=VMEM_LIMIT):
    return pltpu.CompilerParams(dimension_semantics=sem, vmem_limit_bytes=vmem)


def _dot(a, b):
    return jnp.dot(a, b, preferred_element_type=F32)


N_FOLD = N_HEADS + N_GROUPS
Z_WIDTH = N_FOLD * HEAD_DIM


def _fold_in_kernel(win_ref, r_ref, sc_ref, o_ref):
    j = pl.program_id(1)

    @pl.when(j < N_HEADS)
    def _fourier():
        o_ref[0] = win_ref[0].astype(BF16)

    @pl.when(j >= N_HEADS)
    def _pool():
        t = r_ref[0, 0] * sc_ref[0, 0]
        o_ref[0] = jnp.dot(win_ref[0], t, precision=HIGHEST, preferred_element_type=F32).astype(BF16)


def _fold_in_weights(w_in, w_pool, pool_scale):
    grp = lambda j: jnp.maximum(j - N_HEADS, 0)
    return pl.pallas_call(
        _fold_in_kernel,
        out_shape=jax.ShapeDtypeStruct((DEPTH, D_MODEL, Z_WIDTH), BF16),
        grid=(DEPTH, N_FOLD),
        in_specs=[
            pl.BlockSpec((1, D_MODEL, HEAD_DIM), lambda l, j: (l, 0, j)),
            pl.BlockSpec((1, 1, GROUP_DIM, GROUP_DIM), lambda l, j: (l, grp(j), 0, 0)),
            pl.BlockSpec((1, 1, 1, GROUP_DIM), lambda l, j: (l, grp(j), 0, 0)),
        ],
        out_specs=pl.BlockSpec((1, D_MODEL, HEAD_DIM), lambda l, j: (l, 0, j)),
        compiler_params=_cp(("parallel", "parallel")),
        name="fold_in_weights",
    )(w_in, w_pool, pool_scale.reshape(DEPTH, N_GROUPS, 1, GROUP_DIM))


def _fold_head_kernel(ct_ref, st_ref, wf_ref, mc_ref, ms_ref):
    wf = wf_ref[0, 0]
    mc_ref[0, 0] = jnp.dot(ct_ref[...], wf, precision=HIGHEST, preferred_element_type=F32).astype(BF16)
    ms_ref[0, 0] = jnp.dot(st_ref[...], wf, precision=HIGHEST, preferred_element_type=F32).astype(BF16)


def _fold_head_weights(w_fourier):
    c = np.arange(HEAD_DIM)
    ang = 2.0 * np.pi * ((c[:, None] * c[None, :]) % HEAD_DIM) / HEAD_DIM
    ctab = jnp.asarray((np.cos(ang) / np.sqrt(HEAD_DIM)).astype(np.float32))
    stab = jnp.asarray((np.sin(ang) / np.sqrt(HEAD_DIM)).astype(np.float32))
    tab = pl.BlockSpec((HEAD_DIM, HEAD_DIM), lambda l, h: (0, 0))
    blk = pl.BlockSpec((1, 1, HEAD_DIM, HEAD_DIM), lambda l, h: (l, h, 0, 0))
    out = jax.ShapeDtypeStruct((DEPTH, N_HEADS, HEAD_DIM, HEAD_DIM), BF16)
    return pl.pallas_call(
        _fold_head_kernel,
        out_shape=(out, out),
        grid=(DEPTH, N_HEADS),
        in_specs=[tab, tab, blk],
        out_specs=(blk, blk),
        compiler_params=_cp(("parallel", "parallel")),
        name="fold_head_weights",
    )(ctab, stab, w_fourier)


TM_IN = 512


def _norm_mm_kernel(x_ref, g_ref, w_ref, o_ref):
    x = x_ref[...]
    inv = lax.rsqrt(jnp.mean(x * x, axis=-1, keepdims=True) + RMS_EPS)
    h = ((x * inv) * g_ref[...]).astype(BF16)
    o_ref[...] = _dot(h, w_ref[0]).astype(o_ref.dtype)


def _norm_mm(x2d, gamma, wcat, layer):
    return pl.pallas_call(
        _norm_mm_kernel,
        out_shape=jax.ShapeDtypeStruct((ROWS, Z_WIDTH), BF16),
        grid=(ROWS // TM_IN,),
        in_specs=[
            pl.BlockSpec((TM_IN, D_MODEL), lambda i: (i, 0)),
            pl.BlockSpec((1, D_MODEL), lambda i: (0, 0)),
            pl.BlockSpec((1, D_MODEL, Z_WIDTH), lambda i: (layer, 0, 0)),
        ],
        out_specs=pl.BlockSpec((TM_IN, Z_WIDTH), lambda i: (i, 0)),
        compiler_params=_cp(("parallel",)),
        name="norm_in_proj",
    )(x2d, gamma.reshape(1, D_MODEL), wcat)


TM_DFT = 512
TN_DFT = 512
HALF_SEQ = SEQ // 2
N_KBLK = HALF_SEQ // TM_DFT
TAB_GROUP = 16
KX = TM_DFT + TAB_GROUP
COARSE = 64
N_KH = 40


def _table_kernel(ac_ref, as_ref, bc_ref, bs_ref, c_ref, s_ref):
    i = pl.program_id(0)

    def group(gi, carry):
        r0 = pl.multiple_of(gi * TAB_GROUP, TAB_GROUP)
        k0 = i * TM_DFT + r0
        kh = lax.shift_right_logical(k0, 6)
        kl = pl.multiple_of(k0 & (COARSE - 1), TAB_GROUP)
        ca = ac_ref[pl.ds(kh, 1), :]
        sa = as_ref[pl.ds(kh, 1), :]
        cb = bc_ref[pl.ds(kl, TAB_GROUP), :]
        sb = bs_ref[pl.ds(kl, TAB_GROUP), :]
        c_ref[0, pl.ds(r0, TAB_GROUP), :] = (ca * cb - sa * sb).astype(BF16)
        s_ref[0, pl.ds(r0, TAB_GROUP), :] = (sa * cb + ca * sb).astype(BF16)
        return carry

    lax.fori_loop(0, KX // TAB_GROUP, group, 0)


def _dft_tables():
    n = jnp.arange(SEQ, dtype=jnp.int32)[None, :]
    kh = jnp.arange(N_KH, dtype=jnp.int32)[:, None]
    kl = jnp.arange(COARSE, dtype=jnp.int32)[:, None]
    alpha = ((kh * n) % (SEQ // COARSE)).astype(F32) * (2.0 * np.pi * COARSE / SEQ)
    beta = ((kl * n) % SEQ).astype(F32) * (2.0 * np.pi / SEQ)
    scale = 1.0 / np.sqrt(SEQ)
    full = lambda rows: pl.BlockSpec((rows, SEQ), lambda i: (0, 0))
    out = pl.BlockSpec((1, KX, SEQ), lambda i: (i, 0, 0))
    return pl.pallas_call(
        _table_kernel,
        out_shape=(jax.ShapeDtypeStruct((N_KBLK, KX, SEQ), BF16),) * 2,
        grid=(N_KBLK,),
        in_specs=[full(N_KH), full(N_KH), full(COARSE), full(COARSE)],
        out_specs=(out, out),
        compiler_params=_cp(("parallel",)),
        name="dft_tables",
    )(jnp.cos(alpha), jnp.sin(alpha), jnp.cos(beta) * scale, jnp.sin(beta) * scale)


HEADS_PER_BLK = TN_DFT // HEAD_DIM


def _dft_kernel(c_ref, s_ref, u_ref, mc_ref, ms_ref, lo_ref, hi_ref):
    u = u_ref[0]
    pc = _dot(c_ref[0], u).astype(BF16)
    ps = _dot(s_ref[0], u).astype(BF16)
    heads = lambda v, m_ref: jnp.concatenate(
        [_dot(v[:, h * HEAD_DIM:(h + 1) * HEAD_DIM], m_ref[h]) for h in range(HEADS_PER_BLK)], axis=1)
    a = heads(pc, mc_ref)
    bq = heads(ps, ms_ref)
    lo_ref[0] = (a[:TM_DFT] - bq[:TM_DFT]).astype(BF16)
    mirrored = (a + bq).astype(BF16)
    u = lax.broadcasted_iota(jnp.int32, (TM_DFT, KX), 0)
    r = lax.broadcasted_iota(jnp.int32, (TM_DFT, KX), 1)
    flip = jnp.where(r == TM_DFT - u, 1.0, 0.0).astype(BF16)
    hi_ref[0] = _dot(flip, mirrored).astype(BF16)


def _dft(ctab, stab, z3, mc, ms, layer):
    nq = FOURIER_WIDTH // TN_DFT
    half = jax.ShapeDtypeStruct((BATCH, HALF_SEQ, FOURIER_WIDTH), BF16)
    head_blk = pl.BlockSpec((None, HEADS_PER_BLK, HEAD_DIM, HEAD_DIM), lambda i, b, n: (layer, n, 0, 0))
    return pl.pallas_call(
        _dft_kernel,
        out_shape=(half, half),
        grid=(N_KBLK, BATCH, nq),
        in_specs=[
            pl.BlockSpec((1, KX, SEQ), lambda i, b, n: (i, 0, 0)),
            pl.BlockSpec((1, KX, SEQ), lambda i, b, n: (i, 0, 0)),
            pl.BlockSpec((1, SEQ, TN_DFT), lambda i, b, n: (b, 0, n)),
            head_blk,
            head_blk,
        ],
        out_specs=(
            pl.BlockSpec((1, TM_DFT, TN_DFT), lambda i, b, n: (b, i, n)),
            pl.BlockSpec((1, TM_DFT, TN_DFT), lambda i, b, n: (b, N_KBLK - 1 - i, n)),
        ),
        compiler_params=_cp(("parallel", "parallel", "parallel")),
        name="position_dft",
    )(ctab, stab, z3, mc, ms)


TM_OUT = 512


BLK_PER_HALF = HALF_SEQ // TM_OUT
BLK_PER_SEQ = SEQ // TM_OUT


HALO = 16
assert HALO >= max(POOL_WINDOWS) // 2


def _pool_minus_identity(prev, cur, nxt, t0):
    xp = jnp.concatenate([prev, cur, nxt], axis=0)
    t = t0 + lax.broadcasted_iota(jnp.int32, (TM_OUT, 1), 0)
    parts = []
    for gi, w in enumerate(POOL_WINDOWS):
        cols = slice(gi * GROUP_DIM, (gi + 1) * GROUP_DIM)
        a, span = xp[:, cols], 1
        while span < w:
            n = a.shape[0] - span
            a = a[:n] + a[span:span + n]
            span *= 2
        start = HALO - w // 2
        cnt = (jnp.minimum(t + (w - w // 2), SEQ) - jnp.maximum(t - w // 2, 0)).astype(F32)
        parts.append(a[start:start + TM_OUT] / cnt - cur[:, cols])
    return jnp.concatenate(parts, axis=1)


def _out_proj_kernel(ylo_ref, yhi_ref, v_ref, vp_ref, vn_ref, wa_ref, wb_ref, x_ref, g_ref, wrs_ref, wrh_ref,
                     o_ref, h_ref, lg_ref):
    blk = pl.program_id(0) % BLK_PER_SEQ
    ya = jnp.where(blk >= BLK_PER_HALF, yhi_ref[...], ylo_ref[...])
    prev = jnp.where(blk > 0, vp_ref[...].astype(F32), 0.0)
    nxt = jnp.where(blk < BLK_PER_SEQ - 1, vn_ref[...].astype(F32), 0.0)
    yb = _pool_minus_identity(prev, v_ref[...].astype(F32), nxt, blk * TM_OUT).astype(BF16)
    x1 = x_ref[...] + _dot(ya, wa_ref[0]) + _dot(yb, wb_ref[0])
    o_ref[...] = x1
    inv = lax.rsqrt(jnp.mean(x1 * x1, axis=-1, keepdims=True) + RMS_EPS)
    h = (x1 * inv) * g_ref[...]
    h_ref[...] = h
    h_hi = h.astype(BF16)
    h_lo = (h - h_hi.astype(F32)).astype(BF16)
    contract_last = (((1,), (1,)), ((), ()))
    a = lax.dot_general(wrs_ref[...], h_hi, contract_last, preferred_element_type=F32)
    b = lax.dot_general(wrh_ref[...], h_lo, contract_last, preferred_element_type=F32)
    lg_ref[...] = a[:N_EXPERTS] + (a[N_EXPERTS:] + b)


def _out_proj(ya_lo, ya_hi, z2d, w_out_bf, x2d, gamma, wr_split, wr_hi, layer):
    row_blk = lambda i: (i, 0)
    fixed = lambda i: (0, 0)
    half_blk = lambda i: ((i // BLK_PER_SEQ) * BLK_PER_HALF + i % BLK_PER_HALF, 0)
    v_col = FOURIER_WIDTH // POOL_WIDTH
    halo_per_blk = TM_OUT // HALO
    halo_prev = lambda i: (jnp.maximum(i * halo_per_blk - 1, 0), v_col)
    halo_next = lambda i: (jnp.minimum((i + 1) * halo_per_blk, ROWS // HALO - 1), v_col)
    return pl.pallas_call(
        _out_proj_kernel,
        out_shape=(
            jax.ShapeDtypeStruct((ROWS, D_MODEL), F32),
            jax.ShapeDtypeStruct((ROWS, D_MODEL), F32),
            jax.ShapeDtypeStruct((N_EXPERTS, ROWS), F32),
        ),
        grid=(ROWS // TM_OUT,),
        in_specs=[
            pl.BlockSpec((TM_OUT, FOURIER_WIDTH), half_blk),
            pl.BlockSpec((TM_OUT, FOURIER_WIDTH), half_blk),
            pl.BlockSpec((TM_OUT, POOL_WIDTH), lambda i: (i, v_col)),
            pl.BlockSpec((HALO, POOL_WIDTH), halo_prev),
            pl.BlockSpec((HALO, POOL_WIDTH), halo_next),
            pl.BlockSpec((1, FOURIER_WIDTH, D_MODEL), lambda i: (layer, 0, 0)),
            pl.BlockSpec((1, POOL_WIDTH, D_MODEL), lambda i: (layer, 1, 0)),
            pl.BlockSpec((TM_OUT, D_MODEL), row_blk),
            pl.BlockSpec((1, D_MODEL), fixed),
            pl.BlockSpec((2 * N_EXPERTS, D_MODEL), fixed),
            pl.BlockSpec((N_EXPERTS, D_MODEL), fixed),
        ],
        out_specs=(
            pl.BlockSpec((TM_OUT, D_MODEL), row_blk),
            pl.BlockSpec((TM_OUT, D_MODEL), row_blk),
            pl.BlockSpec((N_EXPERTS, TM_OUT), lambda i: (0, i)),
        ),
        compiler_params=_cp(("parallel",)),
        name="out_proj_residual",
    )(ya_lo.reshape(BATCH * HALF_SEQ, FOURIER_WIDTH), ya_hi.reshape(BATCH * HALF_SEQ, FOURIER_WIDTH),
      z2d, z2d, z2d, w_out_bf, w_out_bf, x2d, gamma.reshape(1, D_MODEL), wr_split, wr_hi)


LANE = 128
NGRP = SEQ // LANE
GRP_SHIFT = 5
assert 1 << GRP_SHIFT == NGRP
ER = N_EXPERTS * NGRP
TINY = float(np.finfo(np.float32).tiny)
N_BISECT = 36
N_MCOL = 8
EXPERTS_PER_ITER = 4


def _select_kernel(lg_ref, idx_ref, gate_ref, enc_ref, m_sc, ci_sc, cg_sc):
    lg = lg_ref[...]
    ex = jnp.exp(lg - jnp.max(lg, axis=0, keepdims=True))
    p = ex / jnp.sum(ex, axis=0, keepdims=True)

    def total(v):
        return jnp.sum(jnp.sum(v, axis=1, keepdims=True), axis=2, keepdims=True)

    def bisect(_, lohi):
        lo, hi = lohi
        mid = jnp.sqrt(jnp.maximum(lo, TINY)) * jnp.sqrt(hi)
        ok = total(jnp.where(p >= mid, 1.0, 0.0)) >= CAP
        return jnp.where(ok, mid, lo), jnp.where(ok, hi, mid)

    lo0 = jnp.zeros((N_EXPERTS, 1, 1), F32)
    hi0 = jnp.full((N_EXPERTS, 1, 1), 2.0, F32)
    _, hi = lax.fori_loop(0, N_BISECT, bisect, (lo0, hi0))
    below = jnp.where(p < hi, p, -1.0)
    thr = jnp.max(jnp.max(below, axis=2, keepdims=True), axis=1, keepdims=True)
    gt = p > thr
    eq = p == thr
    need = CAP - total(jnp.where(gt, 1.0, 0.0))

    r_i = lax.broadcasted_iota(jnp.int32, (ER, ER), 0)
    c_i = lax.broadcasted_iota(jnp.int32, (ER, ER), 1)
    same_expert = (c_i >> GRP_SHIFT) == (r_i >> GRP_SHIFT)
    rows_before = jnp.where(jnp.logical_and(c_i < r_i, same_expert), 1.0, 0.0).astype(BF16)
    j_i = lax.broadcasted_iota(jnp.int32, (LANE, LANE), 0)
    l_i = lax.broadcasted_iota(jnp.int32, (LANE, LANE), 1)
    lanes_upto = jnp.where(j_i <= l_i, 1.0, 0.0).astype(BF16)
    ones = jnp.ones((LANE, LANE), BF16)

    def prefix(mask_b):
        within = _dot(mask_b, lanes_upto)
        rowtot = _dot(mask_b, ones)
        rowoff = _dot(rows_before, rowtot.astype(BF16))
        return within, rowoff

    eq_f = jnp.where(eq, 1.0, 0.0)
    w_eq, ro_eq = prefix(eq_f.astype(BF16).reshape(ER, LANE))
    eq_before = (w_eq + ro_eq).reshape(N_EXPERTS, NGRP, LANE) - eq_f
    sel = jnp.logical_or(gt, jnp.logical_and(eq, eq_before < need))
    sel_b = jnp.where(sel, 1.0, 0.0).astype(BF16).reshape(ER, LANE)
    within, rowoff = prefix(sel_b)
    count = within + rowoff
    enc_ref[0] = jnp.where(sel_b > 0, count, -count).astype(jnp.int32)

    half = jnp.floor(rowoff * 0.5)
    p2d = p.reshape(ER, LANE)
    p_1 = p2d.astype(BF16)
    rem = p2d - p_1.astype(F32)
    p_2 = rem.astype(BF16)
    p_3 = (rem - p_2.astype(F32)).astype(BF16)
    grp = (lax.broadcasted_iota(jnp.int32, (ER, LANE), 0) & (NGRP - 1)).astype(F32)
    blocks = [within.astype(BF16), grp.astype(BF16), half.astype(BF16), (rowoff - 2.0 * half).astype(BF16),
              p_1, p_2, p_3, sel_b]
    for k, blk in enumerate(blocks):
        m_sc[:, k * LANE:(k + 1) * LANE] = blk
    ci_sc[...] = jnp.zeros_like(ci_sc)
    cg_sc[...] = jnp.zeros_like(cg_sc)

    s_col = lax.broadcasted_iota(jnp.int32, (CAP, 1), 0).astype(F32)
    lane = lax.broadcasted_iota(jnp.int32, (CAP, LANE), 1)
    lane_f = lane.astype(F32)
    ones8 = jnp.ones((8, LANE), BF16)
    g_r = lax.broadcasted_iota(jnp.int32, (NGRP, NGRP), 0)
    g_c = lax.broadcasted_iota(jnp.int32, (NGRP, NGRP), 1)
    groups_before = jnp.where(g_r < g_c, 1.0, 0.0).astype(BF16)

    def one_expert(e):
        r0 = pl.multiple_of(e * NGRP, NGRP)
        table = m_sc[pl.ds(r0, NGRP), :]
        sel_e = table[:, 7 * LANE:]
        rt = lax.dot_general(ones8, sel_e, (((1,), (1,)), ((), ())), preferred_element_type=F32)
        ro = _dot(rt.astype(BF16), groups_before)
        start = ro[0:1]
        stop = start + rt[0:1]
        in_grp = jnp.logical_and(start <= s_col, s_col < stop)
        got = _dot(jnp.where(in_grp, 1.0, 0.0).astype(BF16), table[:, :7 * LANE])
        s_loc = s_col - (2.0 * got[:, 2 * LANE:3 * LANE] + got[:, 3 * LANE:4 * LANE])
        off = _dot(jnp.where(got[:, :LANE] <= s_loc, 1.0, 0.0).astype(BF16), ones)
        tok = got[:, LANE:2 * LANE] * LANE + off
        hit = lane_f == off
        gate = jnp.zeros((CAP, LANE), F32)
        for k in (4, 5, 6):
            gate = gate + _dot(jnp.where(hit, got[:, k * LANE:(k + 1) * LANE], 0.0).astype(BF16), ones)
        return tok, gate

    def per_group(i, carry):
        ci = ci_sc[...]
        cg = cg_sc[...]
        for k in range(EXPERTS_PER_ITER):
            e = i * EXPERTS_PER_ITER + k
            tok, gate = one_expert(e)
            ci = jnp.where(lane == e, tok, ci)
            cg = jnp.where(lane == e, gate, cg)
        ci_sc[...] = ci
        cg_sc[...] = cg
        return carry

    lax.fori_loop(0, N_EXPERTS // EXPERTS_PER_ITER, per_group, 0)
    idx_ref[0] = ci_sc[...].T[:N_EXPERTS].astype(jnp.int32)
    gate_ref[0] = cg_sc[...].T[:N_EXPERTS]


def _select(lg4):
    return pl.pallas_call(
        _select_kernel,
        out_shape=(
            jax.ShapeDtypeStruct((BATCH, N_EXPERTS, CAP), jnp.int32),
            jax.ShapeDtypeStruct((BATCH, N_EXPERTS, CAP), F32),
            jax.ShapeDtypeStruct((BATCH, ER, LANE), jnp.int32),
        ),
        grid=(BATCH,),
        in_specs=[pl.BlockSpec((N_EXPERTS, None, NGRP, LANE), lambda b: (0, b, 0, 0))],
        out_specs=(
            pl.BlockSpec((1, N_EXPERTS, CAP), lambda b: (b, 0, 0)),
            pl.BlockSpec((1, N_EXPERTS, CAP), lambda b: (b, 0, 0)),
            pl.BlockSpec((1, ER, LANE), lambda b: (b, 0, 0)),
        ),
        scratch_shapes=[
            pltpu.VMEM((ER, N_MCOL * LANE), BF16),
            pltpu.VMEM((CAP, LANE), F32),
            pltpu.VMEM((CAP, LANE), F32),
        ],
        compiler_params=_cp(("parallel",)),
        name="select_topc",
    )(lg4)


TF = 256
TD = 256
NF = D_EXPERT // TF
ND = D_MODEL // TD
M_CHUNK = 512
SUBLANES = 8
SUBLANE_SHIFT = 3
GATHER_ROWS = 512


def _ffn_kernel(idx_ref, meta_ref, h_hbm, wg_ref, wu_ref, wd_ref, o_ref, stage, xs, hid, sem):
    j = pl.program_id(1)

    @pl.when(j == 0)
    def _gather():
        def start_rows(u):
            buf = u % 2

            def issue(grp, c):
                base = pl.multiple_of(grp * SUBLANES, SUBLANES)
                for k in range(SUBLANES):
                    row = idx_ref[0, 0, u * GATHER_ROWS + base + k]
                    src = h_hbm.at[lax.shift_right_logical(row, SUBLANE_SHIFT), pl.ds(row & (SUBLANES - 1), 1), :]
                    pltpu.make_async_copy(src, stage.at[buf, grp, pl.ds(k, 1), :], sem.at[buf]).start()
                return c

            lax.fori_loop(0, GATHER_ROWS // SUBLANES, issue, 0)

        n_units = SLOTS // GATHER_ROWS
        start_rows(0)
        for u in range(n_units):
            if u + 1 < n_units:
                start_rows(u + 1)
            buf = u % 2
            pltpu.make_async_copy(h_hbm.at[pl.ds(0, GATHER_ROWS // SUBLANES)], stage.at[buf], sem.at[buf]).wait()
            xs[u * GATHER_ROWS:(u + 1) * GATHER_ROWS, :] = stage[buf].reshape(GATHER_ROWS, D_MODEL).astype(BF16)

        meta = meta_ref[0]
        lane = lax.broadcasted_iota(jnp.int32, (SLOTS, TD), 1)
        o_ref[0] = jnp.where(lane < TD // 2, meta[:, 1:2], meta[:, 2:3]).astype(o_ref.dtype)

    @pl.when(j < NF)
    def _up():
        wg = wg_ref[...].astype(BF16)
        wu = wu_ref[...].astype(BF16)
        for c in range(SLOTS // M_CHUNK):
            rows = slice(c * M_CHUNK, (c + 1) * M_CHUNK)
            x = xs[rows, :]
            g = _dot(x, wg)
            u = _dot(x, wu)
            hid[j, rows, :] = ((g * jax.nn.sigmoid(g)) * u).astype(BF16)

    @pl.when(j >= NF)
    def _down():
        wd = wd_ref[...].astype(BF16)
        for c in range(SLOTS // M_CHUNK):
            rows = slice(c * M_CHUNK, (c + 1) * M_CHUNK)
            hrows = jnp.concatenate([hid[k, rows, :] for k in range(NF)], axis=1)
            o_ref[0, rows, :] = (_dot(hrows, wd) * meta_ref[0, rows, 0:1]).astype(o_ref.dtype)


Y_WIDTH = D_MODEL + TD
N_META = 4


def _ffn(idx_rows, meta, h2, w_gate, w_up, w_down, layer):
    up_blk = lambda e, j: (layer, e, 0, jnp.minimum(j, NF - 1))
    dn_blk = lambda e, j: (layer, e, 0, jnp.maximum(j - NF, 0))
    out_blk = lambda e, j: (e, 0, jnp.where(j < NF, ND, j - NF))
    return pl.pallas_call(
        _ffn_kernel,
        out_shape=jax.ShapeDtypeStruct((N_EXPERTS, SLOTS, Y_WIDTH), BF16),
        grid=(N_EXPERTS, NF + ND),
        in_specs=[
            pl.BlockSpec((1, 1, SLOTS), lambda e, j: (e, 0, 0), memory_space=pltpu.SMEM),
            pl.BlockSpec((1, SLOTS, N_META), lambda e, j: (e, 0, 0)),
            pl.BlockSpec(memory_space=pl.ANY),
            pl.BlockSpec((None, None, D_MODEL, TF), up_blk),
            pl.BlockSpec((None, None, D_MODEL, TF), up_blk),
            pl.BlockSpec((None, None, D_EXPERT, TD), dn_blk),
        ],
        out_specs=pl.BlockSpec((1, SLOTS, TD), out_blk),
        scratch_shapes=[
            pltpu.VMEM((2, GATHER_ROWS // SUBLANES, SUBLANES, D_MODEL), F32),
            pltpu.VMEM((SLOTS, D_MODEL), BF16),
            pltpu.VMEM((NF, SLOTS, TF), BF16),
            pltpu.SemaphoreType.DMA((2,)),
        ],
        compiler_params=_cp(("arbitrary", "arbitrary")),
        name="expert_swiglu",
    )(idx_rows, meta, h2.reshape(ROWS // SUBLANES, SUBLANES, D_MODEL), w_gate, w_up, w_down)


TB_C = 256
NB_C = SEQ // TB_C
ROW_ALIGN = 16
ROW_SHIFT = 4
KC = 512
KC_SHIFT = 9
assert 1 << ROW_SHIFT == ROW_ALIGN and 1 << KC_SHIFT == KC
N_PHASE = 2
E_PER_PHASE = N_EXPERTS // N_PHASE
STAGE_MAX = E_PER_PHASE * (TB_C + ROW_ALIGN)
STAGE_ROWS = -(-STAGE_MAX // KC) * KC
N_STEPS_C = BATCH * NB_C


def _combine_kernel(bnd_ref, x_ref, g_ref, y_hbm, o_ref, stage, sem, rows_sm, *, final_norm):
    b = pl.program_id(0)
    tb = pl.program_id(1)
    step = b * NB_C + tb

    def chunk_copy(e, src, buf, dst):
        return pltpu.make_async_copy(y_hbm.at[e, pl.ds(src, ROW_ALIGN), :],
                                     stage.at[buf, pl.ds(dst, ROW_ALIGN), :], sem.at[buf])

    def issue(bq, tq, ph):
        pos = jnp.int32(0)
        for e in range(ph * E_PER_PHASE, (ph + 1) * E_PER_PHASE):
            lo = bnd_ref[bq, tq, e]
            hi = bnd_ref[bq, tq + 1, e]
            lo_al = lo - (lo & (ROW_ALIGN - 1))
            nch = jnp.where(hi > lo, lax.shift_right_logical(hi - lo_al + (ROW_ALIGN - 1), ROW_SHIFT), 0)

            def start(c, carry, e=e, lo_al=lo_al, pos=pos):
                src = pl.multiple_of(bq * CAP + lo_al + c * ROW_ALIGN, ROW_ALIGN)
                dst = pl.multiple_of(pos + c * ROW_ALIGN, ROW_ALIGN)
                chunk_copy(e, src, ph, dst).start()
                return carry

            lax.fori_loop(0, nch, start, 0)
            pos = pos + nch * ROW_ALIGN
        rows_sm[ph] = pos

    @pl.when(step == 0)
    def _first():
        stage[...] = jnp.zeros_like(stage)
        issue(b, tb, 0)

    t0 = (tb * TB_C).astype(F32)
    lane = lax.broadcasted_iota(jnp.int32, (KC, LANE), 1).astype(F32)
    krow = lax.broadcasted_iota(jnp.int32, (KC, LANE), 0)

    for ph in range(N_PHASE):
        if ph + 1 < N_PHASE:
            issue(b, tb, ph + 1)
        else:
            @pl.when(step + 1 < N_STEPS_C)
            def _prefetch():
                wrap = tb + 1 == NB_C
                issue(jnp.where(wrap, b + 1, b), jnp.where(wrap, 0, tb + 1), 0)

        rows = rows_sm[ph]

        def drain(c, carry, ph=ph):
            chunk_copy(0, 0, ph, 0).wait()
            return carry

        lax.fori_loop(0, lax.shift_right_logical(rows, ROW_SHIFT), drain, 0)

        def scattered(kc, ph=ph, rows=rows):
            k0 = pl.multiple_of(kc * KC, KC)
            blk = stage[ph, pl.ds(k0, KC), :]
            tok = blk[:, D_MODEL:D_MODEL + LANE].astype(F32) * 64.0 + blk[:, D_MODEL + LANE:].astype(F32) - t0
            live = krow + k0 < rows
            hits = [jnp.logical_and(live, tok - float(q * LANE) == lane) for q in range(TB_C // LANE)]
            onehot_t = jnp.where(jnp.concatenate(hits, axis=1), 1.0, 0.0).astype(BF16)
            return lax.dot_general(onehot_t, blk[:, :D_MODEL], (((0,), (0,)), ((), ())), preferred_element_type=F32)

        def accumulate(kc, carry):
            o_ref[...] += scattered(kc)
            return carry

        n_chunks = lax.shift_right_logical(rows + (KC - 1), KC_SHIFT)
        if ph == 0:
            o_ref[...] = x_ref[...] + scattered(0)
            lax.fori_loop(1, n_chunks, accumulate, 0)
        else:
            lax.fori_loop(0, n_chunks, accumulate, 0)

    if final_norm:
        x = o_ref[...]
        inv = lax.rsqrt(jnp.mean(x * x, axis=-1, keepdims=True) + RMS_EPS)
        o_ref[...] = (x * inv) * g_ref[...]


def _combine(bnd, x1_2d, gamma, ysg, final_norm):
    grid_spec = pltpu.PrefetchScalarGridSpec(
        num_scalar_prefetch=1,
        grid=(BATCH, NB_C),
        in_specs=[
            pl.BlockSpec((TB_C, D_MODEL), lambda b, t, bnd: (b * NB_C + t, 0)),
            pl.BlockSpec((1, D_MODEL), lambda b, t, bnd: (0, 0)),
            pl.BlockSpec(memory_space=pl.ANY),
        ],
        out_specs=pl.BlockSpec((TB_C, D_MODEL), lambda b, t, bnd: (b * NB_C + t, 0)),
        scratch_shapes=[
            pltpu.VMEM((N_PHASE, STAGE_ROWS, Y_WIDTH), BF16),
            pltpu.SemaphoreType.DMA((N_PHASE,)),
            pltpu.SMEM((N_PHASE,), jnp.int32),
        ],
    )
    return pl.pallas_call(
        functools.partial(_combine_kernel, final_norm=final_norm),
        out_shape=jax.ShapeDtypeStruct((ROWS, D_MODEL), F32),
        grid_spec=grid_spec,
        compiler_params=_cp(("arbitrary", "arbitrary")),
        name="combine_final" if final_norm else "combine",
    )(bnd, x1_2d, gamma.reshape(1, D_MODEL), ysg)


def kernel(x, norm1_g, w_in, w_fourier, w_pool, pool_scale, w_out, norm2_g, w_router, w_gate, w_up, w_down, final_g):
    wcat = _fold_in_weights(w_in, w_pool, pool_scale)
    mc, ms = _fold_head_weights(w_fourier)
    ctab, stab = _dft_tables()
    w_out_bf = w_out.astype(BF16)
    w_router_t = jnp.swapaxes(w_router, 1, 2)
    wr_hi = w_router_t.astype(BF16)
    wr_lo = (w_router_t - wr_hi.astype(F32)).astype(BF16)
    wr_split = jnp.concatenate([wr_hi, wr_lo], axis=1)
    batch_base = (jnp.arange(BATCH, dtype=jnp.int32) * SEQ)[:, None, None]

    xc = x.reshape(ROWS, D_MODEL)
    for layer in range(DEPTH):
        z = _norm_mm(xc, norm1_g[layer], wcat, layer)
        z3 = z.reshape(BATCH, SEQ, Z_WIDTH)
        ya_lo, ya_hi = _dft(ctab, stab, z3, mc, ms, layer)
        x1, h2, lg = _out_proj(ya_lo, ya_hi, z, w_out_bf, xc,
                               norm2_g[layer], wr_split[layer], wr_hi[layer], layer)
        idx, gates, enc = _select(lg.reshape(N_EXPERTS, BATCH, NGRP, LANE))

        idx_rows = jnp.swapaxes(idx + batch_base, 0, 1).reshape(N_EXPERTS, 1, SLOTS)
        meta = jnp.stack([gates, (idx >> 6).astype(F32), (idx & 63).astype(F32), jnp.zeros_like(gates)], axis=-1)
        meta = jnp.swapaxes(meta, 0, 1).reshape(N_EXPERTS, SLOTS, N_META)
        counts = jnp.abs(enc).reshape(BATCH, N_EXPERTS, SEQ)
        ends = counts[:, :, TB_C - 1::TB_C]
        bnd = jnp.concatenate([jnp.zeros((BATCH, N_EXPERTS, 1), jnp.int32), ends], axis=2)
        bnd = jnp.swapaxes(bnd, 1, 2)

        ysg = _ffn(idx_rows, meta, h2, w_gate, w_up, w_down, layer)
        xc = _combine(bnd, x1, final_g, ysg, layer == DEPTH - 1)
    return xc.reshape(BATCH, SEQ, D_MODEL)
```

```python
import functools

import numpy as np
import jax
import jax.numpy as jnp
from jax import lax
from jax.experimental import pallas as pl
from jax.experimental.pallas import tpu as pltpu

D_MODEL = 2048
BATCH = 4
SEQ = 4096
DEPTH = 2
N_HEADS = 4
HEAD_DIM = 256
POOL_WINDOWS = (2, 4, 8, 16)
N_GROUPS = 4
GROUP_DIM = 256
FOURIER_WIDTH = N_HEADS * HEAD_DIM
POOL_WIDTH = N_GROUPS * GROUP_DIM
N_EXPERTS = 16
CAP = 2 * SEQ // N_EXPERTS
SLOTS = BATCH * CAP
D_EXPERT = D_MODEL
RMS_EPS = 1e-6
ROWS = BATCH * SEQ

F32 = jnp.float32
BF16 = jnp.bfloat16
HIGHEST = lax.Precision.HIGHEST

VMEM_LIMIT = 54 * 1024 * 1024


def _cp(sem, vmem=VMEM_LIMIT):
    return pltpu.CompilerParams(dimension_semantics=sem, vmem_limit_bytes=vmem)


def _dot(a, b):
    return jnp.dot(a, b, preferred_element_type=F32)


N_FOLD = N_HEADS + N_GROUPS
Z_WIDTH = N_FOLD * HEAD_DIM


def _fold_in_kernel(win_ref, r_ref, sc_ref, o_ref):
    j = pl.program_id(1)

    @pl.when(j < N_HEADS)
    def _fourier():
        o_ref[0] = win_ref[0].astype(BF16)

    @pl.when(j >= N_HEADS)
    def _pool():
        t = r_ref[0, 0] * sc_ref[0, 0]
        o_ref[0] = jnp.dot(win_ref[0], t, precision=HIGHEST, preferred_element_type=F32).astype(BF16)


def _fold_in_weights(w_in, w_pool, pool_scale):
    grp = lambda j: jnp.maximum(j - N_HEADS, 0)
    return pl.pallas_call(
        _fold_in_kernel,
        out_shape=jax.ShapeDtypeStruct((DEPTH, D_MODEL, Z_WIDTH), BF16),
        grid=(DEPTH, N_FOLD),
        in_specs=[
            pl.BlockSpec((1, D_MODEL, HEAD_DIM), lambda l, j: (l, 0, j)),
            pl.BlockSpec((1, 1, GROUP_DIM, GROUP_DIM), lambda l, j: (l, grp(j), 0, 0)),
            pl.BlockSpec((1, 1, 1, GROUP_DIM), lambda l, j: (l, grp(j), 0, 0)),
        ],
        out_specs=pl.BlockSpec((1, D_MODEL, HEAD_DIM), lambda l, j: (l, 0, j)),
        compiler_params=_cp(("parallel", "parallel")),
        name="fold_in_weights",
    )(w_in, w_pool, pool_scale.reshape(DEPTH, N_GROUPS, 1, GROUP_DIM))


def _fold_head_kernel(ct_ref, st_ref, wf_ref, mc_ref, ms_ref):
    wf = wf_ref[0, 0]
    mc_ref[0, 0] = jnp.dot(ct_ref[...], wf, precision=HIGHEST, preferred_element_type=F32).astype(BF16)
    ms_ref[0, 0] = jnp.dot(st_ref[...], wf, precision=HIGHEST, preferred_element_type=F32).astype(BF16)


def _fold_head_weights(w_fourier):
    c = np.arange(HEAD_DIM)
    ang = 2.0 * np.pi * ((c[:, None] * c[None, :]) % HEAD_DIM) / HEAD_DIM
    ctab = jnp.asarray((np.cos(ang) / np.sqrt(HEAD_DIM)).astype(np.float32))
    stab = jnp.asarray((np.sin(ang) / np.sqrt(HEAD_DIM)).astype(np.float32))
    tab = pl.BlockSpec((HEAD_DIM, HEAD_DIM), lambda l, h: (0, 0))
    blk = pl.BlockSpec((1, 1, HEAD_DIM, HEAD_DIM), lambda l, h: (l, h, 0, 0))
    out = jax.ShapeDtypeStruct((DEPTH, N_HEADS, HEAD_DIM, HEAD_DIM), BF16)
    return pl.pallas_call(
        _fold_head_kernel,
        out_shape=(out, out),
        grid=(DEPTH, N_HEADS),
        in_specs=[tab, tab, blk],
        out_specs=(blk, blk),
        compiler_params=_cp(("parallel", "parallel")),
        name="fold_head_weights",
    )(ctab, stab, w_fourier)


TM_IN = 512


def _norm_mm_kernel(x_ref, g_ref, w_ref, o_ref):
    x = x_ref[...]
    inv = lax.rsqrt(jnp.mean(x * x, axis=-1, keepdims=True) + RMS_EPS)
    h = ((x * inv) * g_ref[...]).astype(BF16)
    o_ref[...] = _dot(h, w_ref[0]).astype(o_ref.dtype)


def _norm_mm(x2d, gamma, wcat, layer):
    return pl.pallas_call(
        _norm_mm_kernel,
        out_shape=jax.ShapeDtypeStruct((ROWS, Z_WIDTH), BF16),
        grid=(ROWS // TM_IN,),
        in_specs=[
            pl.BlockSpec((TM_IN, D_MODEL), lambda i: (i, 0)),
            pl.BlockSpec((1, D_MODEL), lambda i: (0, 0)),
            pl.BlockSpec((1, D_MODEL, Z_WIDTH), lambda i: (layer, 0, 0)),
        ],
        out_specs=pl.BlockSpec((TM_IN, Z_WIDTH), lambda i: (i, 0)),
        compiler_params=_cp(("parallel",)),
        name="norm_in_proj",
    )(x2d, gamma.reshape(1, D_MODEL), wcat)


TM_DFT = 512
TN_DFT = 512
HALF_SEQ = SEQ // 2
N_KBLK = HALF_SEQ // TM_DFT
TAB_GROUP = 16
KX = TM_DFT + TAB_GROUP
COARSE = 64
N_KH = 40


def _table_kernel(ac_ref, as_ref, bc_ref, bs_ref, c_ref, s_ref):
    i = pl.program_id(0)

    def group(gi, carry):
        r0 = pl.multiple_of(gi * TAB_GROUP, TAB_GROUP)
        k0 = i * TM_DFT + r0
        kh = lax.shift_right_logical(k0, 6)
        kl = pl.multiple_of(k0 & (COARSE - 1), TAB_GROUP)
        ca = ac_ref[pl.ds(kh, 1), :]
        sa = as_ref[pl.ds(kh, 1), :]
        cb = bc_ref[pl.ds(kl, TAB_GROUP), :]
        sb = bs_ref[pl.ds(kl, TAB_GROUP), :]
        c_ref[0, pl.ds(r0, TAB_GROUP), :] = (ca * cb - sa * sb).astype(BF16)
        s_ref[0, pl.ds(r0, TAB_GROUP), :] = (sa * cb + ca * sb).astype(BF16)
        return carry

    lax.fori_loop(0, KX // TAB_GROUP, group, 0)


def _dft_tables():
    n = jnp.arange(SEQ, dtype=jnp.int32)[None, :]
    kh = jnp.arange(N_KH, dtype=jnp.int32)[:, None]
    kl = jnp.arange(COARSE, dtype=jnp.int32)[:, None]
    alpha = ((kh * n) % (SEQ // COARSE)).astype(F32) * (2.0 * np.pi * COARSE / SEQ)
    beta = ((kl * n) % SEQ).astype(F32) * (2.0 * np.pi / SEQ)
    scale = 1.0 / np.sqrt(SEQ)
    full = lambda rows: pl.BlockSpec((rows, SEQ), lambda i: (0, 0))
    out = pl.BlockSpec((1, KX, SEQ), lambda i: (i, 0, 0))
    return pl.pallas_call(
        _table_kernel,
        out_shape=(jax.ShapeDtypeStruct((N_KBLK, KX, SEQ), BF16),) * 2,
        grid=(N_KBLK,),
        in_specs=[full(N_KH), full(N_KH), full(COARSE), full(COARSE)],
        out_specs=(out, out),
        compiler_params=_cp(("parallel",)),
        name="dft_tables",
    )(jnp.cos(alpha), jnp.sin(alpha), jnp.cos(beta) * scale, jnp.sin(beta) * scale)


HEADS_PER_BLK = TN_DFT // HEAD_DIM


def _dft_kernel(c_ref, s_ref, u_ref, mc_ref, ms_ref, lo_ref, hi_ref):
    u = u_ref[0]
    pc = _dot(c_ref[0], u).astype(BF16)
    ps = _dot(s_ref[0], u).astype(BF16)
    heads = lambda v, m_ref: jnp.concatenate(
        [_dot(v[:, h * HEAD_DIM:(h + 1) * HEAD_DIM], m_ref[h]) for h in range(HEADS_PER_BLK)], axis=1)
    a = heads(pc, mc_ref)
    bq = heads(ps, ms_ref)
    lo_ref[0] = (a[:TM_DFT] - bq[:TM_DFT]).astype(BF16)
    mirrored = (a + bq).astype(BF16)
    u = lax.broadcasted_iota(jnp.int32, (TM_DFT, KX), 0)
    r = lax.broadcasted_iota(jnp.int32, (TM_DFT, KX), 1)
    flip = jnp.where(r == TM_DFT - u, 1.0, 0.0).astype(BF16)
    hi_ref[0] = _dot(flip, mirrored).astype(BF16)


def _dft(ctab, stab, z3, mc, ms, layer):
    nq = FOURIER_WIDTH // TN_DFT
    half = jax.ShapeDtypeStruct((BATCH, HALF_SEQ, FOURIER_WIDTH), BF16)
    head_blk = pl.BlockSpec((None, HEADS_PER_BLK, HEAD_DIM, HEAD_DIM), lambda i, b, n: (layer, n, 0, 0))
    return pl.pallas_call(
        _dft_kernel,
        out_shape=(half, half),
        grid=(N_KBLK, BATCH, nq),
        in_specs=[
            pl.BlockSpec((1, KX, SEQ), lambda i, b, n: (i, 0, 0)),
            pl.BlockSpec((1, KX, SEQ), lambda i, b, n: (i, 0, 0)),
            pl.BlockSpec((1, SEQ, TN_DFT), lambda i, b, n: (b, 0, n)),
            head_blk,
            head_blk,
        ],
        out_specs=(
            pl.BlockSpec((1, TM_DFT, TN_DFT), lambda i, b, n: (b, i, n)),
            pl.BlockSpec((1, TM_DFT, TN_DFT), lambda i, b, n: (b, N_KBLK - 1 - i, n)),
        ),
        compiler_params=_cp(("parallel", "parallel", "parallel")),
        name="position_dft",
    )(ctab, stab, z3, mc, ms)


TM_OUT = 512


BLK_PER_HALF = HALF_SEQ // TM_OUT
BLK_PER_SEQ = SEQ // TM_OUT


HALO = 16
assert HALO >= max(POOL_WINDOWS) // 2


def _pool_minus_identity(prev, cur, nxt, t0):
    xp = jnp.concatenate([prev, cur, nxt], axis=0)
    t = t0 + lax.broadcasted_iota(jnp.int32, (TM_OUT, 1), 0)
    parts = []
    for gi, w in enumerate(POOL_WINDOWS):
        cols = slice(gi * GROUP_DIM, (gi + 1) * GROUP_DIM)
        a, span = xp[:, cols], 1
        while span < w:
            n = a.shape[0] - span
            a = a[:n] + a[span:span + n]
            span *= 2
        start = HALO - w // 2
        cnt = (jnp.minimum(t + (w - w // 2), SEQ) - jnp.maximum(t - w // 2, 0)).astype(F32)
        parts.append(a[start:start + TM_OUT] / cnt - cur[:, cols])
    return jnp.concatenate(parts, axis=1)


def _out_proj_kernel(ylo_ref, yhi_ref, v_ref, vp_ref, vn_ref, wa_ref, wb_ref, x_ref, g_ref, wrs_ref, wrh_ref,
                     o_ref, h_ref, lg_ref):
    blk = pl.program_id(0) % BLK_PER_SEQ
    ya = jnp.where(blk >= BLK_PER_HALF, yhi_ref[...], ylo_ref[...])
    prev = jnp.where(blk > 0, vp_ref[...].astype(F32), 0.0)
    nxt = jnp.where(blk < BLK_PER_SEQ - 1, vn_ref[...].astype(F32), 0.0)
    yb = _pool_minus_identity(prev, v_ref[...].astype(F32), nxt, blk * TM_OUT).astype(BF16)
    x1 = x_ref[...] + _dot(ya, wa_ref[0]) + _dot(yb, wb_ref[0])
    o_ref[...] = x1
    inv = lax.rsqrt(jnp.mean(x1 * x1, axis=-1, keepdims=True) + RMS_EPS)
    h = (x1 * inv) * g_ref[...]
    h_ref[...] = h
    h_hi = h.astype(BF16)
    h_lo = (h - h_hi.astype(F32)).astype(BF16)
    contract_last = (((1,), (1,)), ((), ()))
    a = lax.dot_general(wrs_ref[...], h_hi, contract_last, preferred_element_type=F32)
    b = lax.dot_general(wrh_ref[...], h_lo, contract_last, preferred_element_type=F32)
    lg_ref[...] = a[:N_EXPERTS] + (a[N_EXPERTS:] + b)


def _out_proj(ya_lo, ya_hi, z2d, w_out_bf, x2d, gamma, wr_split, wr_hi, layer):
    row_blk = lambda i: (i, 0)
    fixed = lambda i: (0, 0)
    half_blk = lambda i: ((i // BLK_PER_SEQ) * BLK_PER_HALF + i % BLK_PER_HALF, 0)
    v_col = FOURIER_WIDTH // POOL_WIDTH
    halo_per_blk = TM_OUT // HALO
    halo_prev = lambda i: (jnp.maximum(i * halo_per_blk - 1, 0), v_col)
    halo_next = lambda i: (jnp.minimum((i + 1) * halo_per_blk, ROWS // HALO - 1), v_col)
    return pl.pallas_call(
        _out_proj_kernel,
        out_shape=(
            jax.ShapeDtypeStruct((ROWS, D_MODEL), F32),
            jax.ShapeDtypeStruct((ROWS, D_MODEL), F32),
            jax.ShapeDtypeStruct((N_EXPERTS, ROWS), F32),
        ),
        grid=(ROWS // TM_OUT,),
        in_specs=[
            pl.BlockSpec((TM_OUT, FOURIER_WIDTH), half_blk),
            pl.BlockSpec((TM_OUT, FOURIER_WIDTH), half_blk),
            pl.BlockSpec((TM_OUT, POOL_WIDTH), lambda i: (i, v_col)),
            pl.BlockSpec((HALO, POOL_WIDTH), halo_prev),
            pl.BlockSpec((HALO, POOL_WIDTH), halo_next),
            pl.BlockSpec((1, FOURIER_WIDTH, D_MODEL), lambda i: (layer, 0, 0)),
            pl.BlockSpec((1, POOL_WIDTH, D_MODEL), lambda i: (layer, 1, 0)),
            pl.BlockSpec((TM_OUT, D_MODEL), row_blk),
            pl.BlockSpec((1, D_MODEL), fixed),
            pl.BlockSpec((2 * N_EXPERTS, D_MODEL), fixed),
            pl.BlockSpec((N_EXPERTS, D_MODEL), fixed),
        ],
        out_specs=(
            pl.BlockSpec((TM_OUT, D_MODEL), row_blk),
            pl.BlockSpec((TM_OUT, D_MODEL), row_blk),
            pl.BlockSpec((N_EXPERTS, TM_OUT), lambda i: (0, i)),
        ),
        compiler_params=_cp(("parallel",)),
        name="out_proj_residual",
    )(ya_lo.reshape(BATCH * HALF_SEQ, FOURIER_WIDTH), ya_hi.reshape(BATCH * HALF_SEQ, FOURIER_WIDTH),
      z2d, z2d, z2d, w_out_bf, w_out_bf, x2d, gamma.reshape(1, D_MODEL), wr_split, wr_hi)


LANE = 128
NGRP = SEQ // LANE
GRP_SHIFT = 5
assert 1 << GRP_SHIFT == NGRP
ER = N_EXPERTS * NGRP
TINY = float(np.finfo(np.float32).tiny)
N_BISECT = 36
N_MCOL = 8
EXPERTS_PER_ITER = 4


def _select_kernel(lg_ref, idx_ref, gate_ref, enc_ref, m_sc, ci_sc, cg_sc):
    lg = lg_ref[...]
    ex = jnp.exp(lg - jnp.max(lg, axis=0, keepdims=True))
    p = ex / jnp.sum(ex, axis=0, keepdims=True)

    def total(v):
        return jnp.sum(jnp.sum(v, axis=1, keepdims=True), axis=2, keepdims=True)

    def bisect(_, lohi):
        lo, hi = lohi
        mid = jnp.sqrt(jnp.maximum(lo, TINY)) * jnp.sqrt(hi)
        ok = total(jnp.where(p >= mid, 1.0, 0.0)) >= CAP
        return jnp.where(ok, mid, lo), jnp.where(ok, hi, mid)

    lo0 = jnp.zeros((N_EXPERTS, 1, 1), F32)
    hi0 = jnp.full((N_EXPERTS, 1, 1), 2.0, F32)
    _, hi = lax.fori_loop(0, N_BISECT, bisect, (lo0, hi0))
    below = jnp.where(p < hi, p, -1.0)
    thr = jnp.max(jnp.max(below, axis=2, keepdims=True), axis=1, keepdims=True)
    gt = p > thr
    eq = p == thr
    need = CAP - total(jnp.where(gt, 1.0, 0.0))

    r_i = lax.broadcasted_iota(jnp.int32, (ER, ER), 0)
    c_i = lax.broadcasted_iota(jnp.int32, (ER, ER), 1)
    same_expert = (c_i >> GRP_SHIFT) == (r_i >> GRP_SHIFT)
    rows_before = jnp.where(jnp.logical_and(c_i < r_i, same_expert), 1.0, 0.0).astype(BF16)
    j_i = lax.broadcasted_iota(jnp.int32, (LANE, LANE), 0)
    l_i = lax.broadcasted_iota(jnp.int32, (LANE, LANE), 1)
    lanes_upto = jnp.where(j_i <= l_i, 1.0, 0.0).astype(BF16)
    ones = jnp.ones((LANE, LANE), BF16)

    def prefix(mask_b):
        within = _dot(mask_b, lanes_upto)
        rowtot = _dot(mask_b, ones)
        rowoff = _dot(rows_before, rowtot.astype(BF16))
        return within, rowoff

    eq_f = jnp.where(eq, 1.0, 0.0)
    w_eq, ro_eq = prefix(eq_f.astype(BF16).reshape(ER, LANE))
    eq_before = (w_eq + ro_eq).reshape(N_EXPERTS, NGRP, LANE) - eq_f
    sel = jnp.logical_or(gt, jnp.logical_and(eq, eq_before < need))
    sel_b = jnp.where(sel, 1.0, 0.0).astype(BF16).reshape(ER, LANE)
    within, rowoff = prefix(sel_b)
    count = within + rowoff
    enc_ref[0] = jnp.where(sel_b > 0, count, -count).astype(jnp.int32)

    half = jnp.floor(rowoff * 0.5)
    p2d = p.reshape(ER, LANE)
    p_1 = p2d.astype(BF16)
    rem = p2d - p_1.astype(F32)
    p_2 = rem.astype(BF16)
    p_3 = (rem - p_2.astype(F32)).astype(BF16)
    grp = (lax.broadcasted_iota(jnp.int32, (ER, LANE), 0) & (NGRP - 1)).astype(F32)
    blocks = [within.astype(BF16), grp.astype(BF16), half.astype(BF16), (rowoff - 2.0 * half).astype(BF16),
              p_1, p_2, p_3, sel_b]
    for k, blk in enumerate(blocks):
        m_sc[:, k * LANE:(k + 1) * LANE] = blk
    ci_sc[...] = jnp.zeros_like(ci_sc)
    cg_sc[...] = jnp.zeros_like(cg_sc)

    s_col = lax.broadcasted_iota(jnp.int32, (CAP, 1), 0).astype(F32)
    lane = lax.broadcasted_iota(jnp.int32, (CAP, LANE), 1)
    lane_f = lane.astype(F32)
    ones8 = jnp.ones((8, LANE), BF16)
    g_r = lax.broadcasted_iota(jnp.int32, (NGRP, NGRP), 0)
    g_c = lax.broadcasted_iota(jnp.int32, (NGRP, NGRP), 1)
    groups_before = jnp.where(g_r < g_c, 1.0, 0.0).astype(BF16)

    def one_expert(e):
        r0 = pl.multiple_of(e * NGRP, NGRP)
        table = m_sc[pl.ds(r0, NGRP), :]
        sel_e = table[:, 7 * LANE:]
        rt = lax.dot_general(ones8, sel_e, (((1,), (1,)), ((), ())), preferred_element_type=F32)
        ro = _dot(rt.astype(BF16), groups_before)
        start = ro[0:1]
        stop = start + rt[0:1]
        in_grp = jnp.logical_and(start <= s_col, s_col < stop)
        got = _dot(jnp.where(in_grp, 1.0, 0.0).astype(BF16), table[:, :7 * LANE])
        s_loc = s_col - (2.0 * got[:, 2 * LANE:3 * LANE] + got[:, 3 * LANE:4 * LANE])
        off = _dot(jnp.where(got[:, :LANE] <= s_loc, 1.0, 0.0).astype(BF16), ones)
        tok = got[:, LANE:2 * LANE] * LANE + off
        hit = lane_f == off
        gate = jnp.zeros((CAP, LANE), F32)
        for k in (4, 5, 6):
            gate = gate + _dot(jnp.where(hit, got[:, k * LANE:(k + 1) * LANE], 0.0).astype(BF16), ones)
        return tok, gate

    def per_group(i, carry):
        ci = ci_sc[...]
        cg = cg_sc[...]
        for k in range(EXPERTS_PER_ITER):
            e = i * EXPERTS_PER_ITER + k
            tok, gate = one_expert(e)
            ci = jnp.where(lane == e, tok, ci)
            cg = jnp.where(lane == e, gate, cg)
        ci_sc[...] = ci
        cg_sc[...] = cg
        return carry

    lax.fori_loop(0, N_EXPERTS // EXPERTS_PER_ITER, per_group, 0)
    idx_ref[0] = ci_sc[...].T[:N_EXPERTS].astype(jnp.int32)
    gate_ref[0] = cg_sc[...].T[:N_EXPERTS]


def _select(lg4):
    return pl.pallas_call(
        _select_kernel,
        out_shape=(
            jax.ShapeDtypeStruct((BATCH, N_EXPERTS, CAP), jnp.int32),
            jax.ShapeDtypeStruct((BATCH, N_EXPERTS, CAP), F32),
            jax.ShapeDtypeStruct((BATCH, ER, LANE), jnp.int32),
        ),
        grid=(BATCH,),
        in_specs=[pl.BlockSpec((N_EXPERTS, None, NGRP, LANE), lambda b: (0, b, 0, 0))],
        out_specs=(
            pl.BlockSpec((1, N_EXPERTS, CAP), lambda b: (b, 0, 0)),
            pl.BlockSpec((1, N_EXPERTS, CAP), lambda b: (b, 0, 0)),
            pl.BlockSpec((1, ER, LANE), lambda b: (b, 0, 0)),
        ),
        scratch_shapes=[
            pltpu.VMEM((ER, N_MCOL * LANE), BF16),
            pltpu.VMEM((CAP, LANE), F32),
            pltpu.VMEM((CAP, LANE), F32),
        ],
        compiler_params=_cp(("parallel",)),
        name="select_topc",
    )(lg4)


TF = 256
TD = 256
NF = D_EXPERT // TF
ND = D_MODEL // TD
N_STEPS_E = NF + ND
M_CHUNK = 512
N_CHUNKS = SLOTS // M_CHUNK
SUBLANES = 8
SUBLANE_SHIFT = 3
UNIT_ROWS = SLOTS // N_STEPS_E
BATCH_ROWS = UNIT_ROWS // N_CHUNKS
UNIT_TILES = UNIT_ROWS // SUBLANES
LAST_UNIT = N_STEPS_E - 1


def _ffn_kernel(idx_ref, idxn_ref, meta_ref, h_hbm, wg_ref, wu_ref, wd_ref, o_ref, stage, xs, hid, sem):
    e = pl.program_id(0)
    j = pl.program_id(1)
    cur = e & 1
    nxt = 1 - cur

    def row_copy(ref, slot, buf, r):
        row = ref[0, 0, slot]
        src = h_hbm.at[lax.shift_right_logical(row, SUBLANE_SHIFT), pl.ds(row & (SUBLANES - 1), 1), :]
        return pltpu.make_async_copy(src, stage.at[buf, r // SUBLANES, pl.ds(r % SUBLANES, 1), :], sem.at[buf])

    def wait_unit(buf):
        pltpu.make_async_copy(h_hbm.at[pl.ds(0, UNIT_TILES)], stage.at[buf], sem.at[buf]).wait()

    def land_unit(buf, xbuf, u):
        wait_unit(buf)
        r0 = pl.multiple_of(u * UNIT_ROWS, UNIT_ROWS)
        xs[xbuf, pl.ds(r0, UNIT_ROWS), :] = stage[buf].reshape(UNIT_ROWS, D_MODEL).astype(BF16)

    @pl.when(jnp.logical_and(e == 0, j == 0))
    def _first_expert():
        def unit(u, carry):
            buf = u & 1
            for r in range(UNIT_ROWS):
                row_copy(idx_ref, u * UNIT_ROWS + r, buf, r).start()
            land_unit(buf, 0, u)
            return carry

        lax.fori_loop(0, N_STEPS_E, unit, 0)

    @pl.when(jnp.logical_and(e > 0, j == 0))
    def _last_unit():
        land_unit(LAST_UNIT & 1, cur, LAST_UNIT)

    @pl.when(j > 0)
    def _previous_unit():
        land_unit((j - 1) & 1, nxt, j - 1)

    @pl.when(j == 0)
    def _token_id():
        meta = meta_ref[0]
        lane = lax.broadcasted_iota(jnp.int32, (SLOTS, TD), 1)
        o_ref[0] = jnp.where(lane < TD // 2, meta[:, 1:2], meta[:, 2:3]).astype(o_ref.dtype)

    def prefetch_rows(c):
        buf = j & 1
        for k in range(BATCH_ROWS):
            r = c * BATCH_ROWS + k
            row_copy(idxn_ref, j * UNIT_ROWS + r, buf, r).start()

    @pl.when(j < NF)
    def _up():
        wg = wg_ref[...].astype(BF16)
        wu = wu_ref[...].astype(BF16)
        for c in range(N_CHUNKS):
            rows = slice(c * M_CHUNK, (c + 1) * M_CHUNK)
            x = xs[cur, rows, :]
            g = _dot(x, wg)
            u = _dot(x, wu)
            hid[j, rows, :] = ((g * jax.nn.sigmoid(g)) * u).astype(BF16)
            prefetch_rows(c)

    @pl.when(j >= NF)
    def _down():
        wd = wd_ref[...].astype(BF16)
        for c in range(N_CHUNKS):
            rows = slice(c * M_CHUNK, (c + 1) * M_CHUNK)
            hrows = jnp.concatenate([hid[k, rows, :] for k in range(NF)], axis=1)
            o_ref[0, rows, :] = (_dot(hrows, wd) * meta_ref[0, rows, 0:1]).astype(o_ref.dtype)
            prefetch_rows(c)

    @pl.when(jnp.logical_and(e == N_EXPERTS - 1, j == LAST_UNIT))
    def _drain():
        wait_unit(LAST_UNIT & 1)


Y_WIDTH = D_MODEL + TD
N_META = 4


def _ffn(idx_rows, meta, h2, w_gate, w_up, w_down, layer):
    up_blk = lambda e, j: (layer, e, 0, jnp.minimum(j, NF - 1))
    dn_blk = lambda e, j: (layer, e, 0, jnp.maximum(j - NF, 0))
    out_blk = lambda e, j: (e, 0, jnp.where(j < NF, ND, j - NF))
    return pl.pallas_call(
        _ffn_kernel,
        out_shape=jax.ShapeDtypeStruct((N_EXPERTS, SLOTS, Y_WIDTH), BF16),
        grid=(N_EXPERTS, NF + ND),
        in_specs=[
            pl.BlockSpec((1, 1, SLOTS), lambda e, j: (e, 0, 0), memory_space=pltpu.SMEM),
            pl.BlockSpec((1, 1, SLOTS), lambda e, j: (jnp.minimum(e + 1, N_EXPERTS - 1), 0, 0),
                         memory_space=pltpu.SMEM),
            pl.BlockSpec((1, SLOTS, N_META), lambda e, j: (e, 0, 0)),
            pl.BlockSpec(memory_space=pl.ANY),
            pl.BlockSpec((None, None, D_MODEL, TF), up_blk),
            pl.BlockSpec((None, None, D_MODEL, TF), up_blk),
            pl.BlockSpec((None, None, D_EXPERT, TD), dn_blk),
        ],
        out_specs=pl.BlockSpec((1, SLOTS, TD), out_blk),
        scratch_shapes=[
            pltpu.VMEM((2, UNIT_TILES, SUBLANES, D_MODEL), F32),
            pltpu.VMEM((2, SLOTS, D_MODEL), BF16),
            pltpu.VMEM((NF, SLOTS, TF), BF16),
            pltpu.SemaphoreType.DMA((2,)),
        ],
        compiler_params=_cp(("arbitrary", "arbitrary")),
        name="expert_swiglu",
    )(idx_rows, idx_rows, meta, h2.reshape(ROWS // SUBLANES, SUBLANES, D_MODEL), w_gate, w_up, w_down)


TB_C = 256
NB_C = SEQ // TB_C
ROW_ALIGN = 16
ROW_SHIFT = 4
KC = 512
KC_SHIFT = 9
assert 1 << ROW_SHIFT == ROW_ALIGN and 1 << KC_SHIFT == KC
N_PHASE = 2
E_PER_PHASE = N_EXPERTS // N_PHASE
STAGE_MAX = E_PER_PHASE * (TB_C + ROW_ALIGN)
STAGE_ROWS = -(-STAGE_MAX // KC) * KC
N_STEPS_C = BATCH * NB_C


def _combine_kernel(bnd_ref, x_ref, g_ref, y_hbm, o_ref, stage, sem, rows_sm, *, final_norm):
    b = pl.program_id(0)
    tb = pl.program_id(1)
    step = b * NB_C + tb

    def chunk_copy(e, src, buf, dst):
        return pltpu.make_async_copy(y_hbm.at[e, pl.ds(src, ROW_ALIGN), :],
                                     stage.at[buf, pl.ds(dst, ROW_ALIGN), :], sem.at[buf])

    def issue(bq, tq, ph):
        pos = jnp.int32(0)
        for e in range(ph * E_PER_PHASE, (ph + 1) * E_PER_PHASE):
            lo = bnd_ref[bq, tq, e]
            hi = bnd_ref[bq, tq + 1, e]
            lo_al = lo - (lo & (ROW_ALIGN - 1))
            nch = jnp.where(hi > lo, lax.shift_right_logical(hi - lo_al + (ROW_ALIGN - 1), ROW_SHIFT), 0)

            def start(c, carry, e=e, lo_al=lo_al, pos=pos):
                src = pl.multiple_of(bq * CAP + lo_al + c * ROW_ALIGN, ROW_ALIGN)
                dst = pl.multiple_of(pos + c * ROW_ALIGN, ROW_ALIGN)
                chunk_copy(e, src, ph, dst).start()
                return carry

            lax.fori_loop(0, nch, start, 0)
            pos = pos + nch * ROW_ALIGN
        rows_sm[ph] = pos

    @pl.when(step == 0)
    def _first():
        stage[...] = jnp.zeros_like(stage)
        issue(b, tb, 0)

    t0 = (tb * TB_C).astype(F32)
    lane = lax.broadcasted_iota(jnp.int32, (KC, LANE), 1).astype(F32)
    krow = lax.broadcasted_iota(jnp.int32, (KC, LANE), 0)

    for ph in range(N_PHASE):
        if ph + 1 < N_PHASE:
            issue(b, tb, ph + 1)
        else:
            @pl.when(step + 1 < N_STEPS_C)
            def _prefetch():
                wrap = tb + 1 == NB_C
                issue(jnp.where(wrap, b + 1, b), jnp.where(wrap, 0, tb + 1), 0)

        rows = rows_sm[ph]

        def drain(c, carry, ph=ph):
            chunk_copy(0, 0, ph, 0).wait()
            return carry

        lax.fori_loop(0, lax.shift_right_logical(rows, ROW_SHIFT), drain, 0)

        def scattered(kc, ph=ph, rows=rows):
            k0 = pl.multiple_of(kc * KC, KC)
            blk = stage[ph, pl.ds(k0, KC), :]
            tok = blk[:, D_MODEL:D_MODEL + LANE].astype(F32) * 64.0 + blk[:, D_MODEL + LANE:].astype(F32) - t0
            live = krow + k0 < rows
            hits = [jnp.logical_and(live, tok - float(q * LANE) == lane) for q in range(TB_C // LANE)]
            onehot_t = jnp.where(jnp.concatenate(hits, axis=1), 1.0, 0.0).astype(BF16)
            return lax.dot_general(onehot_t, blk[:, :D_MODEL], (((0,), (0,)), ((), ())), preferred_element_type=F32)

        def accumulate(kc, carry):
            o_ref[...] += scattered(kc)
            return carry

        n_chunks = lax.shift_right_logical(rows + (KC - 1), KC_SHIFT)
        if ph == 0:
            o_ref[...] = x_ref[...] + scattered(0)
            lax.fori_loop(1, n_chunks, accumulate, 0)
        else:
            lax.fori_loop(0, n_chunks, accumulate, 0)

    if final_norm:
        x = o_ref[...]
        inv = lax.rsqrt(jnp.mean(x * x, axis=-1, keepdims=True) + RMS_EPS)
        o_ref[...] = (x * inv) * g_ref[...]


def _combine(bnd, x1_2d, gamma, ysg, final_norm):
    grid_spec = pltpu.PrefetchScalarGridSpec(
        num_scalar_prefetch=1,
        grid=(BATCH, NB_C),
        in_specs=[
            pl.BlockSpec((TB_C, D_MODEL), lambda b, t, bnd: (b * NB_C + t, 0)),
            pl.BlockSpec((1, D_MODEL), lambda b, t, bnd: (0, 0)),
            pl.BlockSpec(memory_space=pl.ANY),
        ],
        out_specs=pl.BlockSpec((TB_C, D_MODEL), lambda b, t, bnd: (b * NB_C + t, 0)),
        scratch_shapes=[
            pltpu.VMEM((N_PHASE, STAGE_ROWS, Y_WIDTH), BF16),
            pltpu.SemaphoreType.DMA((N_PHASE,)),
            pltpu.SMEM((N_PHASE,), jnp.int32),
        ],
    )
    return pl.pallas_call(
        functools.partial(_combine_kernel, final_norm=final_norm),
        out_shape=jax.ShapeDtypeStruct((ROWS, D_MODEL), F32),
        grid_spec=grid_spec,
        compiler_params=_cp(("arbitrary", "arbitrary")),
        name="combine_final" if final_norm else "combine",
    )(bnd, x1_2d, gamma.reshape(1, D_MODEL), ysg)


def kernel(x, norm1_g, w_in, w_fourier, w_pool, pool_scale, w_out, norm2_g, w_router, w_gate, w_up, w_down, final_g):
    wcat = _fold_in_weights(w_in, w_pool, pool_scale)
    mc, ms = _fold_head_weights(w_fourier)
    ctab, stab = _dft_tables()
    w_out_bf = w_out.astype(BF16)
    w_router_t = jnp.swapaxes(w_router, 1, 2)
    wr_hi = w_router_t.astype(BF16)
    wr_lo = (w_router_t - wr_hi.astype(F32)).astype(BF16)
    wr_split = jnp.concatenate([wr_hi, wr_lo], axis=1)
    batch_base = (jnp.arange(BATCH, dtype=jnp.int32) * SEQ)[:, None, None]

    xc = x.reshape(ROWS, D_MODEL)
    for layer in range(DEPTH):
        z = _norm_mm(xc, norm1_g[layer], wcat, layer)
        z3 = z.reshape(BATCH, SEQ, Z_WIDTH)
        ya_lo, ya_hi = _dft(ctab, stab, z3, mc, ms, layer)
        x1, h2, lg = _out_proj(ya_lo, ya_hi, z, w_out_bf, xc,
                               norm2_g[layer], wr_split[layer], wr_hi[layer], layer)
        idx, gates, enc = _select(lg.reshape(N_EXPERTS, BATCH, NGRP, LANE))

        idx_rows = jnp.swapaxes(idx + batch_base, 0, 1).reshape(N_EXPERTS, 1, SLOTS)
        meta = jnp.stack([gates, (idx >> 6).astype(F32), (idx & 63).astype(F32), jnp.zeros_like(gates)], axis=-1)
        meta = jnp.swapaxes(meta, 0, 1).reshape(N_EXPERTS, SLOTS, N_META)
        counts = jnp.abs(enc).reshape(BATCH, N_EXPERTS, SEQ)
        ends = counts[:, :, TB_C - 1::TB_C]
        bnd = jnp.concatenate([jnp.zeros((BATCH, N_EXPERTS, 1), jnp.int32), ends], axis=2)
        bnd = jnp.swapaxes(bnd, 1, 2)

        ysg = _ffn(idx_rows, meta, h2, w_gate, w_up, w_down, layer)
        xc = _combine(bnd, x1, final_g, ysg, layer == DEPTH - 1)
    return xc.reshape(BATCH, SEQ, D_MODEL)
```

```python
import functools

import numpy as np
import jax
import jax.numpy as jnp
from jax import lax
from jax.experimental import pallas as pl
from jax.experimental.pallas import tpu as pltpu

D_MODEL = 2048
BATCH = 4
SEQ = 4096
DEPTH = 2
N_HEADS = 4
HEAD_DIM = 256
POOL_WINDOWS = (2, 4, 8, 16)
N_GROUPS = 4
GROUP_DIM = 256
FOURIER_WIDTH = N_HEADS * HEAD_DIM
POOL_WIDTH = N_GROUPS * GROUP_DIM
N_EXPERTS = 16
CAP = 2 * SEQ // N_EXPERTS
SLOTS = BATCH * CAP
D_EXPERT = D_MODEL
RMS_EPS = 1e-6
ROWS = BATCH * SEQ

F32 = jnp.float32
BF16 = jnp.bfloat16
HIGHEST = lax.Precision.HIGHEST

VMEM_LIMIT = 54 * 1024 * 1024


def _cp(sem, vmem=VMEM_LIMIT):
    return pltpu.CompilerParams(dimension_semantics=sem, vmem_limit_bytes=vmem)


def _dot(a, b):
    return jnp.dot(a, b, preferred_element_type=F32)


N_FOLD = N_HEADS + N_GROUPS
Z_WIDTH = N_FOLD * HEAD_DIM


def _fold_in_kernel(win_ref, r_ref, sc_ref, o_ref):
    j = pl.program_id(1)

    @pl.when(j < N_HEADS)
    def _fourier():
        o_ref[0] = win_ref[0].astype(BF16)

    @pl.when(j >= N_HEADS)
    def _pool():
        t = r_ref[0, 0] * sc_ref[0, 0]
        o_ref[0] = jnp.dot(win_ref[0], t, precision=HIGHEST, preferred_element_type=F32).astype(BF16)


def _fold_in_weights(w_in, w_pool, pool_scale):
    grp = lambda j: jnp.maximum(j - N_HEADS, 0)
    return pl.pallas_call(
        _fold_in_kernel,
        out_shape=jax.ShapeDtypeStruct((DEPTH, D_MODEL, Z_WIDTH), BF16),
        grid=(DEPTH, N_FOLD),
        in_specs=[
            pl.BlockSpec((1, D_MODEL, HEAD_DIM), lambda l, j: (l, 0, j)),
            pl.BlockSpec((1, 1, GROUP_DIM, GROUP_DIM), lambda l, j: (l, grp(j), 0, 0)),
            pl.BlockSpec((1, 1, 1, GROUP_DIM), lambda l, j: (l, grp(j), 0, 0)),
        ],
        out_specs=pl.BlockSpec((1, D_MODEL, HEAD_DIM), lambda l, j: (l, 0, j)),
        compiler_params=_cp(("parallel", "parallel")),
        name="fold_in_weights",
    )(w_in, w_pool, pool_scale.reshape(DEPTH, N_GROUPS, 1, GROUP_DIM))


def _fold_head_kernel(ct_ref, st_ref, wf_ref, mc_ref, ms_ref):
    wf = wf_ref[0, 0]
    mc_ref[0, 0] = jnp.dot(ct_ref[...], wf, precision=HIGHEST, preferred_element_type=F32).astype(BF16)
    ms_ref[0, 0] = jnp.dot(st_ref[...], wf, precision=HIGHEST, preferred_element_type=F32).astype(BF16)


def _fold_head_weights(w_fourier):
    c = np.arange(HEAD_DIM)
    ang = 2.0 * np.pi * ((c[:, None] * c[None, :]) % HEAD_DIM) / HEAD_DIM
    ctab = jnp.asarray((np.cos(ang) / np.sqrt(HEAD_DIM)).astype(np.float32))
    stab = jnp.asarray((np.sin(ang) / np.sqrt(HEAD_DIM)).astype(np.float32))
    tab = pl.BlockSpec((HEAD_DIM, HEAD_DIM), lambda l, h: (0, 0))
    blk = pl.BlockSpec((1, 1, HEAD_DIM, HEAD_DIM), lambda l, h: (l, h, 0, 0))
    out = jax.ShapeDtypeStruct((DEPTH, N_HEADS, HEAD_DIM, HEAD_DIM), BF16)
    return pl.pallas_call(
        _fold_head_kernel,
        out_shape=(out, out),
        grid=(DEPTH, N_HEADS),
        in_specs=[tab, tab, blk],
        out_specs=(blk, blk),
        compiler_params=_cp(("parallel", "parallel")),
        name="fold_head_weights",
    )(ctab, stab, w_fourier)


TM_IN = 512


def _norm_mm_kernel(x_ref, g_ref, w_ref, o_ref):
    x = x_ref[...]
    inv = lax.rsqrt(jnp.mean(x * x, axis=-1, keepdims=True) + RMS_EPS)
    h = ((x * inv) * g_ref[...]).astype(BF16)
    o_ref[...] = _dot(h, w_ref[0]).astype(o_ref.dtype)


def _norm_mm(x2d, gamma, wcat, layer):
    return pl.pallas_call(
        _norm_mm_kernel,
        out_shape=jax.ShapeDtypeStruct((ROWS, Z_WIDTH), BF16),
        grid=(ROWS // TM_IN,),
        in_specs=[
            pl.BlockSpec((TM_IN, D_MODEL), lambda i: (i, 0)),
            pl.BlockSpec((1, D_MODEL), lambda i: (0, 0)),
            pl.BlockSpec((1, D_MODEL, Z_WIDTH), lambda i: (layer, 0, 0)),
        ],
        out_specs=pl.BlockSpec((TM_IN, Z_WIDTH), lambda i: (i, 0)),
        compiler_params=_cp(("parallel",)),
        name="norm_in_proj",
    )(x2d, gamma.reshape(1, D_MODEL), wcat)


TM_DFT = 512
TN_DFT = 512
HALF_SEQ = SEQ // 2
N_KBLK = HALF_SEQ // TM_DFT
TAB_GROUP = 16
KX = TM_DFT + TAB_GROUP
COARSE = 64
N_KH = 40


def _table_kernel(ac_ref, as_ref, bc_ref, bs_ref, c_ref, s_ref):
    i = pl.program_id(0)

    def group(gi, carry):
        r0 = pl.multiple_of(gi * TAB_GROUP, TAB_GROUP)
        k0 = i * TM_DFT + r0
        kh = lax.shift_right_logical(k0, 6)
        kl = pl.multiple_of(k0 & (COARSE - 1), TAB_GROUP)
        ca = ac_ref[pl.ds(kh, 1), :]
        sa = as_ref[pl.ds(kh, 1), :]
        cb = bc_ref[pl.ds(kl, TAB_GROUP), :]
        sb = bs_ref[pl.ds(kl, TAB_GROUP), :]
        c_ref[0, pl.ds(r0, TAB_GROUP), :] = (ca * cb - sa * sb).astype(BF16)
        s_ref[0, pl.ds(r0, TAB_GROUP), :] = (sa * cb + ca * sb).astype(BF16)
        return carry

    lax.fori_loop(0, KX // TAB_GROUP, group, 0)


def _dft_tables():
    n = jnp.arange(SEQ, dtype=jnp.int32)[None, :]
    kh = jnp.arange(N_KH, dtype=jnp.int32)[:, None]
    kl = jnp.arange(COARSE, dtype=jnp.int32)[:, None]
    alpha = ((kh * n) % (SEQ // COARSE)).astype(F32) * (2.0 * np.pi * COARSE / SEQ)
    beta = ((kl * n) % SEQ).astype(F32) * (2.0 * np.pi / SEQ)
    scale = 1.0 / np.sqrt(SEQ)
    full = lambda rows: pl.BlockSpec((rows, SEQ), lambda i: (0, 0))
    out = pl.BlockSpec((1, KX, SEQ), lambda i: (i, 0, 0))
    return pl.pallas_call(
        _table_kernel,
        out_shape=(jax.ShapeDtypeStruct((N_KBLK, KX, SEQ), BF16),) * 2,
        grid=(N_KBLK,),
        in_specs=[full(N_KH), full(N_KH), full(COARSE), full(COARSE)],
        out_specs=(out, out),
        compiler_params=_cp(("parallel",)),
        name="dft_tables",
    )(jnp.cos(alpha), jnp.sin(alpha), jnp.cos(beta) * scale, jnp.sin(beta) * scale)


HEADS_PER_BLK = TN_DFT // HEAD_DIM


def _dft_kernel(c_ref, s_ref, u_ref, mc_ref, ms_ref, lo_ref, hi_ref):
    u = u_ref[0]
    pc = _dot(c_ref[0], u).astype(BF16)
    ps = _dot(s_ref[0], u).astype(BF16)
    heads = lambda v, m_ref: jnp.concatenate(
        [_dot(v[:, h * HEAD_DIM:(h + 1) * HEAD_DIM], m_ref[h]) for h in range(HEADS_PER_BLK)], axis=1)
    a = heads(pc, mc_ref)
    bq = heads(ps, ms_ref)
    lo_ref[0] = (a[:TM_DFT] - bq[:TM_DFT]).astype(BF16)
    mirrored = (a + bq).astype(BF16)
    u = lax.broadcasted_iota(jnp.int32, (TM_DFT, KX), 0)
    r = lax.broadcasted_iota(jnp.int32, (TM_DFT, KX), 1)
    flip = jnp.where(r == TM_DFT - u, 1.0, 0.0).astype(BF16)
    hi_ref[0] = _dot(flip, mirrored).astype(BF16)


def _dft(ctab, stab, z3, mc, ms, layer):
    nq = FOURIER_WIDTH // TN_DFT
    half = jax.ShapeDtypeStruct((BATCH, HALF_SEQ, FOURIER_WIDTH), BF16)
    head_blk = pl.BlockSpec((None, HEADS_PER_BLK, HEAD_DIM, HEAD_DIM), lambda i, b, n: (layer, n, 0, 0))
    return pl.pallas_call(
        _dft_kernel,
        out_shape=(half, half),
        grid=(N_KBLK, BATCH, nq),
        in_specs=[
            pl.BlockSpec((1, KX, SEQ), lambda i, b, n: (i, 0, 0)),
            pl.BlockSpec((1, KX, SEQ), lambda i, b, n: (i, 0, 0)),
            pl.BlockSpec((1, SEQ, TN_DFT), lambda i, b, n: (b, 0, n)),
            head_blk,
            head_blk,
        ],
        out_specs=(
            pl.BlockSpec((1, TM_DFT, TN_DFT), lambda i, b, n: (b, i, n)),
            pl.BlockSpec((1, TM_DFT, TN_DFT), lambda i, b, n: (b, N_KBLK - 1 - i, n)),
        ),
        compiler_params=_cp(("parallel", "parallel", "parallel")),
        name="position_dft",
    )(ctab, stab, z3, mc, ms)


TM_OUT = 512


BLK_PER_HALF = HALF_SEQ // TM_OUT
BLK_PER_SEQ = SEQ // TM_OUT


HALO = 16
assert HALO >= max(POOL_WINDOWS) // 2


def _pool_minus_identity(prev, cur, nxt, t0):
    xp = jnp.concatenate([prev, cur, nxt], axis=0)
    t = t0 + lax.broadcasted_iota(jnp.int32, (TM_OUT, 1), 0)
    parts = []
    for gi, w in enumerate(POOL_WINDOWS):
        cols = slice(gi * GROUP_DIM, (gi + 1) * GROUP_DIM)
        a, span = xp[:, cols], 1
        while span < w:
            n = a.shape[0] - span
            a = a[:n] + a[span:span + n]
            span *= 2
        start = HALO - w // 2
        cnt = (jnp.minimum(t + (w - w // 2), SEQ) - jnp.maximum(t - w // 2, 0)).astype(F32)
        parts.append(a[start:start + TM_OUT] / cnt - cur[:, cols])
    return jnp.concatenate(parts, axis=1)


def _out_proj_kernel(ylo_ref, yhi_ref, v_ref, vp_ref, vn_ref, wa_ref, wb_ref, x_ref, g_ref, wrs_ref, wrh_ref,
                     o_ref, h_ref, lg_ref):
    blk = pl.program_id(0) % BLK_PER_SEQ
    ya = jnp.where(blk >= BLK_PER_HALF, yhi_ref[...], ylo_ref[...])
    prev = jnp.where(blk > 0, vp_ref[...].astype(F32), 0.0)
    nxt = jnp.where(blk < BLK_PER_SEQ - 1, vn_ref[...].astype(F32), 0.0)
    yb = _pool_minus_identity(prev, v_ref[...].astype(F32), nxt, blk * TM_OUT).astype(BF16)
    x1 = x_ref[...] + _dot(ya, wa_ref[0]) + _dot(yb, wb_ref[0])
    o_ref[...] = x1
    inv = lax.rsqrt(jnp.mean(x1 * x1, axis=-1, keepdims=True) + RMS_EPS)
    h = (x1 * inv) * g_ref[...]
    h_ref[...] = h
    h_hi = h.astype(BF16)
    h_lo = (h - h_hi.astype(F32)).astype(BF16)
    contract_last = (((1,), (1,)), ((), ()))
    a = lax.dot_general(wrs_ref[...], h_hi, contract_last, preferred_element_type=F32)
    b = lax.dot_general(wrh_ref[...], h_lo, contract_last, preferred_element_type=F32)
    lg_ref[...] = a[:N_EXPERTS] + (a[N_EXPERTS:] + b)


def _out_proj(ya_lo, ya_hi, z2d, w_out_bf, x2d, gamma, wr_split, wr_hi, layer):
    row_blk = lambda i: (i, 0)
    fixed = lambda i: (0, 0)
    half_blk = lambda i: ((i // BLK_PER_SEQ) * BLK_PER_HALF + i % BLK_PER_HALF, 0)
    v_col = FOURIER_WIDTH // POOL_WIDTH
    halo_per_blk = TM_OUT // HALO
    halo_prev = lambda i: (jnp.maximum(i * halo_per_blk - 1, 0), v_col)
    halo_next = lambda i: (jnp.minimum((i + 1) * halo_per_blk, ROWS // HALO - 1), v_col)
    return pl.pallas_call(
        _out_proj_kernel,
        out_shape=(
            jax.ShapeDtypeStruct((ROWS, D_MODEL), F32),
            jax.ShapeDtypeStruct((ROWS, D_MODEL), F32),
            jax.ShapeDtypeStruct((N_EXPERTS, ROWS), F32),
        ),
        grid=(ROWS // TM_OUT,),
        in_specs=[
            pl.BlockSpec((TM_OUT, FOURIER_WIDTH), half_blk),
            pl.BlockSpec((TM_OUT, FOURIER_WIDTH), half_blk),
            pl.BlockSpec((TM_OUT, POOL_WIDTH), lambda i: (i, v_col)),
            pl.BlockSpec((HALO, POOL_WIDTH), halo_prev),
            pl.BlockSpec((HALO, POOL_WIDTH), halo_next),
            pl.BlockSpec((1, FOURIER_WIDTH, D_MODEL), lambda i: (layer, 0, 0)),
            pl.BlockSpec((1, POOL_WIDTH, D_MODEL), lambda i: (layer, 1, 0)),
            pl.BlockSpec((TM_OUT, D_MODEL), row_blk),
            pl.BlockSpec((1, D_MODEL), fixed),
            pl.BlockSpec((2 * N_EXPERTS, D_MODEL), fixed),
            pl.BlockSpec((N_EXPERTS, D_MODEL), fixed),
        ],
        out_specs=(
            pl.BlockSpec((TM_OUT, D_MODEL), row_blk),
            pl.BlockSpec((TM_OUT, D_MODEL), row_blk),
            pl.BlockSpec((N_EXPERTS, TM_OUT), lambda i: (0, i)),
        ),
        compiler_params=_cp(("parallel",)),
        name="out_proj_residual",
    )(ya_lo.reshape(BATCH * HALF_SEQ, FOURIER_WIDTH), ya_hi.reshape(BATCH * HALF_SEQ, FOURIER_WIDTH),
      z2d, z2d, z2d, w_out_bf, w_out_bf, x2d, gamma.reshape(1, D_MODEL), wr_split, wr_hi)


LANE = 128
NGRP = SEQ // LANE
GRP_SHIFT = 5
assert 1 << GRP_SHIFT == NGRP
ER = N_EXPERTS * NGRP
TINY = float(np.finfo(np.float32).tiny)
N_BISECT = 36
N_MCOL = 8
EXPERTS_PER_ITER = 4


def _select_kernel(lg_ref, idx_ref, gate_ref, enc_ref, m_sc, ci_sc, cg_sc):
    lg = lg_ref[...]
    ex = jnp.exp(lg - jnp.max(lg, axis=0, keepdims=True))
    p = ex / jnp.sum(ex, axis=0, keepdims=True)

    def total(v):
        return jnp.sum(jnp.sum(v, axis=1, keepdims=True), axis=2, keepdims=True)

    def bisect(_, lohi):
        lo, hi = lohi
        mid = jnp.sqrt(jnp.maximum(lo, TINY)) * jnp.sqrt(hi)
        ok = total(jnp.where(p >= mid, 1.0, 0.0)) >= CAP
        return jnp.where(ok, mid, lo), jnp.where(ok, hi, mid)

    lo0 = jnp.zeros((N_EXPERTS, 1, 1), F32)
    hi0 = jnp.full((N_EXPERTS, 1, 1), 2.0, F32)
    _, hi = lax.fori_loop(0, N_BISECT, bisect, (lo0, hi0))
    below = jnp.where(p < hi, p, -1.0)
    thr = jnp.max(jnp.max(below, axis=2, keepdims=True), axis=1, keepdims=True)
    gt = p > thr
    eq = p == thr
    need = CAP - total(jnp.where(gt, 1.0, 0.0))

    r_i = lax.broadcasted_iota(jnp.int32, (ER, ER), 0)
    c_i = lax.broadcasted_iota(jnp.int32, (ER, ER), 1)
    same_expert = (c_i >> GRP_SHIFT) == (r_i >> GRP_SHIFT)
    rows_before = jnp.where(jnp.logical_and(c_i < r_i, same_expert), 1.0, 0.0).astype(BF16)
    j_i = lax.broadcasted_iota(jnp.int32, (LANE, LANE), 0)
    l_i = lax.broadcasted_iota(jnp.int32, (LANE, LANE), 1)
    lanes_upto = jnp.where(j_i <= l_i, 1.0, 0.0).astype(BF16)
    ones = jnp.ones((LANE, LANE), BF16)

    def prefix(mask_b):
        within = _dot(mask_b, lanes_upto)
        rowtot = _dot(mask_b, ones)
        rowoff = _dot(rows_before, rowtot.astype(BF16))
        return within, rowoff

    eq_f = jnp.where(eq, 1.0, 0.0)
    w_eq, ro_eq = prefix(eq_f.astype(BF16).reshape(ER, LANE))
    eq_before = (w_eq + ro_eq).reshape(N_EXPERTS, NGRP, LANE) - eq_f
    sel = jnp.logical_or(gt, jnp.logical_and(eq, eq_before < need))
    sel_b = jnp.where(sel, 1.0, 0.0).astype(BF16).reshape(ER, LANE)
    within, rowoff = prefix(sel_b)
    count = within + rowoff
    enc_ref[0] = jnp.where(sel_b > 0, count, -count).astype(jnp.int32)

    half = jnp.floor(rowoff * 0.5)
    p2d = p.reshape(ER, LANE)
    p_1 = p2d.astype(BF16)
    rem = p2d - p_1.astype(F32)
    p_2 = rem.astype(BF16)
    p_3 = (rem - p_2.astype(F32)).astype(BF16)
    grp = (lax.broadcasted_iota(jnp.int32, (ER, LANE), 0) & (NGRP - 1)).astype(F32)
    blocks = [within.astype(BF16), grp.astype(BF16), half.astype(BF16), (rowoff - 2.0 * half).astype(BF16),
              p_1, p_2, p_3, sel_b]
    for k, blk in enumerate(blocks):
        m_sc[:, k * LANE:(k + 1) * LANE] = blk
    ci_sc[...] = jnp.zeros_like(ci_sc)
    cg_sc[...] = jnp.zeros_like(cg_sc)

    s_col = lax.broadcasted_iota(jnp.int32, (CAP, 1), 0).astype(F32)
    lane = lax.broadcasted_iota(jnp.int32, (CAP, LANE), 1)
    lane_f = lane.astype(F32)
    ones8 = jnp.ones((8, LANE), BF16)
    g_r = lax.broadcasted_iota(jnp.int32, (NGRP, NGRP), 0)
    g_c = lax.broadcasted_iota(jnp.int32, (NGRP, NGRP), 1)
    groups_before = jnp.where(g_r < g_c, 1.0, 0.0).astype(BF16)

    def one_expert(e):
        r0 = pl.multiple_of(e * NGRP, NGRP)
        table = m_sc[pl.ds(r0, NGRP), :]
        sel_e = table[:, 7 * LANE:]
        rt = lax.dot_general(ones8, sel_e, (((1,), (1,)), ((), ())), preferred_element_type=F32)
        ro = _dot(rt.astype(BF16), groups_before)
        start = ro[0:1]
        stop = start + rt[0:1]
        in_grp = jnp.logical_and(start <= s_col, s_col < stop)
        got = _dot(jnp.where(in_grp, 1.0, 0.0).astype(BF16), table[:, :7 * LANE])
        s_loc = s_col - (2.0 * got[:, 2 * LANE:3 * LANE] + got[:, 3 * LANE:4 * LANE])
        off = _dot(jnp.where(got[:, :LANE] <= s_loc, 1.0, 0.0).astype(BF16), ones)
        tok = got[:, LANE:2 * LANE] * LANE + off
        hit = lane_f == off
        gate = jnp.zeros((CAP, LANE), F32)
        for k in (4, 5, 6):
            gate = gate + _dot(jnp.where(hit, got[:, k * LANE:(k + 1) * LANE], 0.0).astype(BF16), ones)
        return tok, gate

    def per_group(i, carry):
        ci = ci_sc[...]
        cg = cg_sc[...]
        for k in range(EXPERTS_PER_ITER):
            e = i * EXPERTS_PER_ITER + k
            tok, gate = one_expert(e)
            ci = jnp.where(lane == e, tok, ci)
            cg = jnp.where(lane == e, gate, cg)
        ci_sc[...] = ci
        cg_sc[...] = cg
        return carry

    lax.fori_loop(0, N_EXPERTS // EXPERTS_PER_ITER, per_group, 0)
    idx_ref[0] = ci_sc[...].T[:N_EXPERTS].astype(jnp.int32)
    gate_ref[0] = cg_sc[...].T[:N_EXPERTS]


def _select(lg4):
    return pl.pallas_call(
        _select_kernel,
        out_shape=(
            jax.ShapeDtypeStruct((BATCH, N_EXPERTS, CAP), jnp.int32),
            jax.ShapeDtypeStruct((BATCH, N_EXPERTS, CAP), F32),
            jax.ShapeDtypeStruct((BATCH, ER, LANE), jnp.int32),
        ),
        grid=(BATCH,),
        in_specs=[pl.BlockSpec((N_EXPERTS, None, NGRP, LANE), lambda b: (0, b, 0, 0))],
        out_specs=(
            pl.BlockSpec((1, N_EXPERTS, CAP), lambda b: (b, 0, 0)),
            pl.BlockSpec((1, N_EXPERTS, CAP), lambda b: (b, 0, 0)),
            pl.BlockSpec((1, ER, LANE), lambda b: (b, 0, 0)),
        ),
        scratch_shapes=[
            pltpu.VMEM((ER, N_MCOL * LANE), BF16),
            pltpu.VMEM((CAP, LANE), F32),
            pltpu.VMEM((CAP, LANE), F32),
        ],
        compiler_params=_cp(("parallel",)),
        name="select_topc",
    )(lg4)


TF = 256
TD = 256
NF = D_EXPERT // TF
ND = D_MODEL // TD
M_CHUNK = 512
SUBLANES = 8
SUBLANE_SHIFT = 3
GATHER_ROWS = 512
N_DMA_THREADS = 2


def _ffn_kernel(idx_ref, meta_ref, h_hbm, wg_ref, wu_ref, wd_ref, o_ref, stage, xs, hid, sem):
    j = pl.program_id(1)

    @pl.when(j == 0)
    def _gather():
        def start_rows(u):
            buf = u % 2

            def issue(grp, c):
                base = pl.multiple_of(grp * SUBLANES, SUBLANES)
                for k in range(SUBLANES):
                    row = idx_ref[0, 0, u * GATHER_ROWS + base + k]
                    src = h_hbm.at[lax.shift_right_logical(row, SUBLANE_SHIFT), pl.ds(row & (SUBLANES - 1), 1), :]
                    pltpu.make_async_copy(src, stage.at[buf, grp, pl.ds(k, 1), :], sem.at[buf]).start(
                        priority=k % N_DMA_THREADS)
                return c

            lax.fori_loop(0, GATHER_ROWS // SUBLANES, issue, 0)

        n_units = SLOTS // GATHER_ROWS
        start_rows(0)
        for u in range(n_units):
            if u + 1 < n_units:
                start_rows(u + 1)
            buf = u % 2
            pltpu.make_async_copy(h_hbm.at[pl.ds(0, GATHER_ROWS // SUBLANES)], stage.at[buf], sem.at[buf]).wait()
            xs[u * GATHER_ROWS:(u + 1) * GATHER_ROWS, :] = stage[buf].reshape(GATHER_ROWS, D_MODEL).astype(BF16)

        meta = meta_ref[0]
        lane = lax.broadcasted_iota(jnp.int32, (SLOTS, TD), 1)
        o_ref[0] = jnp.where(lane < TD // 2, meta[:, 1:2], meta[:, 2:3]).astype(o_ref.dtype)

    @pl.when(j < NF)
    def _up():
        wg = wg_ref[...].astype(BF16)
        wu = wu_ref[...].astype(BF16)
        for c in range(SLOTS // M_CHUNK):
            rows = slice(c * M_CHUNK, (c + 1) * M_CHUNK)
            x = xs[rows, :]
            g = _dot(x, wg)
            u = _dot(x, wu)
            hid[j, rows, :] = ((g * jax.nn.sigmoid(g)) * u).astype(BF16)

    @pl.when(j >= NF)
    def _down():
        wd = wd_ref[...].astype(BF16)
        for c in range(SLOTS // M_CHUNK):
            rows = slice(c * M_CHUNK, (c + 1) * M_CHUNK)
            hrows = jnp.concatenate([hid[k, rows, :] for k in range(NF)], axis=1)
            o_ref[0, rows, :] = (_dot(hrows, wd) * meta_ref[0, rows, 0:1]).astype(o_ref.dtype)


Y_WIDTH = D_MODEL + TD
N_META = 4


def _ffn(idx_rows, meta, h2, w_gate, w_up, w_down, layer):
    up_blk = lambda e, j: (layer, e, 0, jnp.minimum(j, NF - 1))
    dn_blk = lambda e, j: (layer, e, 0, jnp.maximum(j - NF, 0))
    out_blk = lambda e, j: (e, 0, jnp.where(j < NF, ND, j - NF))
    return pl.pallas_call(
        _ffn_kernel,
        out_shape=jax.ShapeDtypeStruct((N_EXPERTS, SLOTS, Y_WIDTH), BF16),
        grid=(N_EXPERTS, NF + ND),
        in_specs=[
            pl.BlockSpec((1, 1, SLOTS), lambda e, j: (e, 0, 0), memory_space=pltpu.SMEM),
            pl.BlockSpec((1, SLOTS, N_META), lambda e, j: (e, 0, 0)),
            pl.BlockSpec(memory_space=pl.ANY),
            pl.BlockSpec((None, None, D_MODEL, TF), up_blk),
            pl.BlockSpec((None, None, D_MODEL, TF), up_blk),
            pl.BlockSpec((None, None, D_EXPERT, TD), dn_blk),
        ],
        out_specs=pl.BlockSpec((1, SLOTS, TD), out_blk),
        scratch_shapes=[
            pltpu.VMEM((2, GATHER_ROWS // SUBLANES, SUBLANES, D_MODEL), F32),
            pltpu.VMEM((SLOTS, D_MODEL), BF16),
            pltpu.VMEM((NF, SLOTS, TF), BF16),
            pltpu.SemaphoreType.DMA((2,)),
        ],
        compiler_params=_cp(("arbitrary", "arbitrary")),
        name="expert_swiglu",
    )(idx_rows, meta, h2.reshape(ROWS // SUBLANES, SUBLANES, D_MODEL), w_gate, w_up, w_down)


TB_C = 256
NB_C = SEQ // TB_C
ROW_ALIGN = 16
ROW_SHIFT = 4
KC = 512
KC_SHIFT = 9
assert 1 << ROW_SHIFT == ROW_ALIGN and 1 << KC_SHIFT == KC
N_PHASE = 2
E_PER_PHASE = N_EXPERTS // N_PHASE
STAGE_MAX = E_PER_PHASE * (TB_C + ROW_ALIGN)
STAGE_ROWS = -(-STAGE_MAX // KC) * KC
N_STEPS_C = BATCH * NB_C


def _combine_kernel(bnd_ref, x_ref, g_ref, y_hbm, o_ref, stage, sem, rows_sm, *, final_norm):
    b = pl.program_id(0)
    tb = pl.program_id(1)
    step = b * NB_C + tb

    def chunk_copy(e, src, buf, dst):
        return pltpu.make_async_copy(y_hbm.at[e, pl.ds(src, ROW_ALIGN), :],
                                     stage.at[buf, pl.ds(dst, ROW_ALIGN), :], sem.at[buf])

    def issue(bq, tq, ph):
        pos = jnp.int32(0)
        for e in range(ph * E_PER_PHASE, (ph + 1) * E_PER_PHASE):
            lo = bnd_ref[bq, tq, e]
            hi = bnd_ref[bq, tq + 1, e]
            lo_al = lo - (lo & (ROW_ALIGN - 1))
            nch = jnp.where(hi > lo, lax.shift_right_logical(hi - lo_al + (ROW_ALIGN - 1), ROW_SHIFT), 0)

            def start(c, carry, e=e, lo_al=lo_al, pos=pos):
                src = pl.multiple_of(bq * CAP + lo_al + c * ROW_ALIGN, ROW_ALIGN)
                dst = pl.multiple_of(pos + c * ROW_ALIGN, ROW_ALIGN)
                chunk_copy(e, src, ph, dst).start()
                return carry

            lax.fori_loop(0, nch, start, 0)
            pos = pos + nch * ROW_ALIGN
        rows_sm[ph] = pos

    @pl.when(step == 0)
    def _first():
        stage[...] = jnp.zeros_like(stage)
        issue(b, tb, 0)

    t0 = (tb * TB_C).astype(F32)
    lane = lax.broadcasted_iota(jnp.int32, (KC, LANE), 1).astype(F32)
    krow = lax.broadcasted_iota(jnp.int32, (KC, LANE), 0)

    for ph in range(N_PHASE):
        if ph + 1 < N_PHASE:
            issue(b, tb, ph + 1)
        else:
            @pl.when(step + 1 < N_STEPS_C)
            def _prefetch():
                wrap = tb + 1 == NB_C
                issue(jnp.where(wrap, b + 1, b), jnp.where(wrap, 0, tb + 1), 0)

        rows = rows_sm[ph]

        def drain(c, carry, ph=ph):
            chunk_copy(0, 0, ph, 0).wait()
            return carry

        lax.fori_loop(0, lax.shift_right_logical(rows, ROW_SHIFT), drain, 0)

        def scattered(kc, ph=ph, rows=rows):
            k0 = pl.multiple_of(kc * KC, KC)
            blk = stage[ph, pl.ds(k0, KC), :]
            tok = blk[:, D_MODEL:D_MODEL + LANE].astype(F32) * 64.0 + blk[:, D_MODEL + LANE:].astype(F32) - t0
            live = krow + k0 < rows
            hits = [jnp.logical_and(live, tok - float(q * LANE) == lane) for q in range(TB_C // LANE)]
            onehot_t = jnp.where(jnp.concatenate(hits, axis=1), 1.0, 0.0).astype(BF16)
            return lax.dot_general(onehot_t, blk[:, :D_MODEL], (((0,), (0,)), ((), ())), preferred_element_type=F32)

        def accumulate(kc, carry):
            o_ref[...] += scattered(kc)
            return carry

        n_chunks = lax.shift_right_logical(rows + (KC - 1), KC_SHIFT)
        if ph == 0:
            o_ref[...] = x_ref[...] + scattered(0)
            lax.fori_loop(1, n_chunks, accumulate, 0)
        else:
            lax.fori_loop(0, n_chunks, accumulate, 0)

    if final_norm:
        x = o_ref[...]
        inv = lax.rsqrt(jnp.mean(x * x, axis=-1, keepdims=True) + RMS_EPS)
        o_ref[...] = (x * inv) * g_ref[...]


def _combine(bnd, x1_2d, gamma, ysg, final_norm):
    grid_spec = pltpu.PrefetchScalarGridSpec(
        num_scalar_prefetch=1,
        grid=(BATCH, NB_C),
        in_specs=[
            pl.BlockSpec((TB_C, D_MODEL), lambda b, t, bnd: (b * NB_C + t, 0)),
            pl.BlockSpec((1, D_MODEL), lambda b, t, bnd: (0, 0)),
            pl.BlockSpec(memory_space=pl.ANY),
        ],
        out_specs=pl.BlockSpec((TB_C, D_MODEL), lambda b, t, bnd: (b * NB_C + t, 0)),
        scratch_shapes=[
            pltpu.VMEM((N_PHASE, STAGE_ROWS, Y_WIDTH), BF16),
            pltpu.SemaphoreType.DMA((N_PHASE,)),
            pltpu.SMEM((N_PHASE,), jnp.int32),
        ],
    )
    return pl.pallas_call(
        functools.partial(_combine_kernel, final_norm=final_norm),
        out_shape=jax.ShapeDtypeStruct((ROWS, D_MODEL), F32),
        grid_spec=grid_spec,
        compiler_params=_cp(("arbitrary", "arbitrary")),
        name="combine_final" if final_norm else "combine",
    )(bnd, x1_2d, gamma.reshape(1, D_MODEL), ysg)


def kernel(x, norm1_g, w_in, w_fourier, w_pool, pool_scale, w_out, norm2_g, w_router, w_gate, w_up, w_down, final_g):
    wcat = _fold_in_weights(w_in, w_pool, pool_scale)
    mc, ms = _fold_head_weights(w_fourier)
    ctab, stab = _dft_tables()
    w_out_bf = w_out.astype(BF16)
    w_router_t = jnp.swapaxes(w_router, 1, 2)
    wr_hi = w_router_t.astype(BF16)
    wr_lo = (w_router_t - wr_hi.astype(F32)).astype(BF16)
    wr_split = jnp.concatenate([wr_hi, wr_lo], axis=1)
    batch_base = (jnp.arange(BATCH, dtype=jnp.int32) * SEQ)[:, None, None]

    xc = x.reshape(ROWS, D_MODEL)
    for layer in range(DEPTH):
        z = _norm_mm(xc, norm1_g[layer], wcat, layer)
        z3 = z.reshape(BATCH, SEQ, Z_WIDTH)
        ya_lo, ya_hi = _dft(ctab, stab, z3, mc, ms, layer)
        x1, h2, lg = _out_proj(ya_lo, ya_hi, z, w_out_bf, xc,
                               norm2_g[layer], wr_split[layer], wr_hi[layer], layer)
        idx, gates, enc = _select(lg.reshape(N_EXPERTS, BATCH, NGRP, LANE))

        idx_rows = jnp.swapaxes(idx + batch_base, 0, 1).reshape(N_EXPERTS, 1, SLOTS)
        meta = jnp.stack([gates, (idx >> 6).astype(F32), (idx & 63).astype(F32), jnp.zeros_like(gates)], axis=-1)
        meta = jnp.swapaxes(meta, 0, 1).reshape(N_EXPERTS, SLOTS, N_META)
        counts = jnp.abs(enc).reshape(BATCH, N_EXPERTS, SEQ)
        ends = counts[:, :, TB_C - 1::TB_C]
        bnd = jnp.concatenate([jnp.zeros((BATCH, N_EXPERTS, 1), jnp.int32), ends], axis=2)
        bnd = jnp.swapaxes(bnd, 1, 2)

        ysg = _ffn(idx_rows, meta, h2, w_gate, w_up, w_down, layer)
        xc = _combine(bnd, x1, final_g, ysg, layer == DEPTH - 1)
    return xc.reshape(BATCH, SEQ, D_MODEL)
```

```python
import functools

import numpy as np
import jax
import jax.numpy as jnp
from jax import lax
from jax.experimental import pallas as pl
from jax.experimental.pallas import tpu as pltpu

D_MODEL = 2048
BATCH = 4
SEQ = 4096
DEPTH = 2
N_HEADS = 4
HEAD_DIM = 256
POOL_WINDOWS = (2, 4, 8, 16)
N_GROUPS = 4
GROUP_DIM = 256
FOURIER_WIDTH = N_HEADS * HEAD_DIM
POOL_WIDTH = N_GROUPS * GROUP_DIM
N_EXPERTS = 16
CAP = 2 * SEQ // N_EXPERTS
SLOTS = BATCH * CAP
D_EXPERT = D_MODEL
RMS_EPS = 1e-6
ROWS = BATCH * SEQ

F32 = jnp.float32
BF16 = jnp.bfloat16
HIGHEST = lax.Precision.HIGHEST

VMEM_LIMIT = 54 * 1024 * 1024


def _cp(sem, vmem=VMEM_LIMIT):
    return pltpu.CompilerParams(dimension_semantics=sem, vmem_limit_bytes=vmem)


def _dot(a, b):
    return jnp.dot(a, b, preferred_element_type=F32)


N_FOLD = N_HEADS + N_GROUPS
Z_WIDTH = N_FOLD * HEAD_DIM


def _fold_in_kernel(win_ref, r_ref, sc_ref, o_ref):
    j = pl.program_id(1)

    @pl.when(j < N_HEADS)
    def _fourier():
        o_ref[0] = win_ref[0].astype(BF16)

    @pl.when(j >= N_HEADS)
    def _pool():
        t = r_ref[0, 0] * sc_ref[0, 0]
        o_ref[0] = jnp.dot(win_ref[0], t, precision=HIGHEST, preferred_element_type=F32).astype(BF16)


def _fold_in_weights(w_in, w_pool, pool_scale):
    grp = lambda j: jnp.maximum(j - N_HEADS, 0)
    return pl.pallas_call(
        _fold_in_kernel,
        out_shape=jax.ShapeDtypeStruct((DEPTH, D_MODEL, Z_WIDTH), BF16),
        grid=(DEPTH, N_FOLD),
        in_specs=[
            pl.BlockSpec((1, D_MODEL, HEAD_DIM), lambda l, j: (l, 0, j)),
            pl.BlockSpec((1, 1, GROUP_DIM, GROUP_DIM), lambda l, j: (l, grp(j), 0, 0)),
            pl.BlockSpec((1, 1, 1, GROUP_DIM), lambda l, j: (l, grp(j), 0, 0)),
        ],
        out_specs=pl.BlockSpec((1, D_MODEL, HEAD_DIM), lambda l, j: (l, 0, j)),
        compiler_params=_cp(("parallel", "parallel")),
        name="fold_in_weights",
    )(w_in, w_pool, pool_scale.reshape(DEPTH, N_GROUPS, 1, GROUP_DIM))


def _fold_head_kernel(ct_ref, st_ref, wf_ref, mc_ref, ms_ref):
    wf = wf_ref[0, 0]
    mc_ref[0, 0] = jnp.dot(ct_ref[...], wf, precision=HIGHEST, preferred_element_type=F32).astype(BF16)
    ms_ref[0, 0] = jnp.dot(st_ref[...], wf, precision=HIGHEST, preferred_element_type=F32).astype(BF16)


def _fold_head_weights(w_fourier):
    c = np.arange(HEAD_DIM)
    ang = 2.0 * np.pi * ((c[:, None] * c[None, :]) % HEAD_DIM) / HEAD_DIM
    ctab = jnp.asarray((np.cos(ang) / np.sqrt(HEAD_DIM)).astype(np.float32))
    stab = jnp.asarray((np.sin(ang) / np.sqrt(HEAD_DIM)).astype(np.float32))
    tab = pl.BlockSpec((HEAD_DIM, HEAD_DIM), lambda l, h: (0, 0))
    blk = pl.BlockSpec((1, 1, HEAD_DIM, HEAD_DIM), lambda l, h: (l, h, 0, 0))
    out = jax.ShapeDtypeStruct((DEPTH, N_HEADS, HEAD_DIM, HEAD_DIM), BF16)
    return pl.pallas_call(
        _fold_head_kernel,
        out_shape=(out, out),
        grid=(DEPTH, N_HEADS),
        in_specs=[tab, tab, blk],
        out_specs=(blk, blk),
        compiler_params=_cp(("parallel", "parallel")),
        name="fold_head_weights",
    )(ctab, stab, w_fourier)


TM_IN = 512


def _norm_mm_kernel(x_ref, g_ref, w_ref, o_ref):
    x = x_ref[...]
    inv = lax.rsqrt(jnp.mean(x * x, axis=-1, keepdims=True) + RMS_EPS)
    h = ((x * inv) * g_ref[...]).astype(BF16)
    o_ref[...] = _dot(h, w_ref[0]).astype(o_ref.dtype)


def _norm_mm(x2d, gamma, wcat, layer):
    return pl.pallas_call(
        _norm_mm_kernel,
        out_shape=jax.ShapeDtypeStruct((ROWS, Z_WIDTH), BF16),
        grid=(ROWS // TM_IN,),
        in_specs=[
            pl.BlockSpec((TM_IN, D_MODEL), lambda i: (i, 0)),
            pl.BlockSpec((1, D_MODEL), lambda i: (0, 0)),
            pl.BlockSpec((1, D_MODEL, Z_WIDTH), lambda i: (layer, 0, 0)),
        ],
        out_specs=pl.BlockSpec((TM_IN, Z_WIDTH), lambda i: (i, 0)),
        compiler_params=_cp(("parallel",)),
        name="norm_in_proj",
    )(x2d, gamma.reshape(1, D_MODEL), wcat)


TM_DFT = 512
TN_DFT = 512
HALF_SEQ = SEQ // 2
N_KBLK = HALF_SEQ // TM_DFT
TAB_GROUP = 16
KX = TM_DFT + TAB_GROUP
COARSE = 64
N_KH = 40


def _table_kernel(ac_ref, as_ref, bc_ref, bs_ref, c_ref, s_ref):
    i = pl.program_id(0)

    def group(gi, carry):
        r0 = pl.multiple_of(gi * TAB_GROUP, TAB_GROUP)
        k0 = i * TM_DFT + r0
        kh = lax.shift_right_logical(k0, 6)
        kl = pl.multiple_of(k0 & (COARSE - 1), TAB_GROUP)
        ca = ac_ref[pl.ds(kh, 1), :]
        sa = as_ref[pl.ds(kh, 1), :]
        cb = bc_ref[pl.ds(kl, TAB_GROUP), :]
        sb = bs_ref[pl.ds(kl, TAB_GROUP), :]
        c_ref[0, pl.ds(r0, TAB_GROUP), :] = (ca * cb - sa * sb).astype(BF16)
        s_ref[0, pl.ds(r0, TAB_GROUP), :] = (sa * cb + ca * sb).astype(BF16)
        return carry

    lax.fori_loop(0, KX // TAB_GROUP, group, 0)


def _dft_tables():
    n = jnp.arange(SEQ, dtype=jnp.int32)[None, :]
    kh = jnp.arange(N_KH, dtype=jnp.int32)[:, None]
    kl = jnp.arange(COARSE, dtype=jnp.int32)[:, None]
    alpha = ((kh * n) % (SEQ // COARSE)).astype(F32) * (2.0 * np.pi * COARSE / SEQ)
    beta = ((kl * n) % SEQ).astype(F32) * (2.0 * np.pi / SEQ)
    scale = 1.0 / np.sqrt(SEQ)
    full = lambda rows: pl.BlockSpec((rows, SEQ), lambda i: (0, 0))
    out = pl.BlockSpec((1, KX, SEQ), lambda i: (i, 0, 0))
    return pl.pallas_call(
        _table_kernel,
        out_shape=(jax.ShapeDtypeStruct((N_KBLK, KX, SEQ), BF16),) * 2,
        grid=(N_KBLK,),
        in_specs=[full(N_KH), full(N_KH), full(COARSE), full(COARSE)],
        out_specs=(out, out),
        compiler_params=_cp(("parallel",)),
        name="dft_tables",
    )(jnp.cos(alpha), jnp.sin(alpha), jnp.cos(beta) * scale, jnp.sin(beta) * scale)


HEADS_PER_BLK = TN_DFT // HEAD_DIM


def _dft_kernel(c_ref, s_ref, u_ref, mc_ref, ms_ref, lo_ref, hi_ref):
    u = u_ref[0]
    pc = _dot(c_ref[0], u).astype(BF16)
    ps = _dot(s_ref[0], u).astype(BF16)
    heads = lambda v, m_ref: jnp.concatenate(
        [_dot(v[:, h * HEAD_DIM:(h + 1) * HEAD_DIM], m_ref[h]) for h in range(HEADS_PER_BLK)], axis=1)
    a = heads(pc, mc_ref)
    bq = heads(ps, ms_ref)
    lo_ref[0] = (a[:TM_DFT] - bq[:TM_DFT]).astype(BF16)
    mirrored = (a + bq).astype(BF16)
    u = lax.broadcasted_iota(jnp.int32, (TM_DFT, KX), 0)
    r = lax.broadcasted_iota(jnp.int32, (TM_DFT, KX), 1)
    flip = jnp.where(r == TM_DFT - u, 1.0, 0.0).astype(BF16)
    hi_ref[0] = _dot(flip, mirrored).astype(BF16)


def _dft(ctab, stab, z3, mc, ms, layer):
    nq = FOURIER_WIDTH // TN_DFT
    half = jax.ShapeDtypeStruct((BATCH, HALF_SEQ, FOURIER_WIDTH), BF16)
    head_blk = pl.BlockSpec((None, HEADS_PER_BLK, HEAD_DIM, HEAD_DIM), lambda i, b, n: (layer, n, 0, 0))
    return pl.pallas_call(
        _dft_kernel,
        out_shape=(half, half),
        grid=(N_KBLK, BATCH, nq),
        in_specs=[
            pl.BlockSpec((1, KX, SEQ), lambda i, b, n: (i, 0, 0)),
            pl.BlockSpec((1, KX, SEQ), lambda i, b, n: (i, 0, 0)),
            pl.BlockSpec((1, SEQ, TN_DFT), lambda i, b, n: (b, 0, n)),
            head_blk,
            head_blk,
        ],
        out_specs=(
            pl.BlockSpec((1, TM_DFT, TN_DFT), lambda i, b, n: (b, i, n)),
            pl.BlockSpec((1, TM_DFT, TN_DFT), lambda i, b, n: (b, N_KBLK - 1 - i, n)),
        ),
        compiler_params=_cp(("parallel", "parallel", "parallel")),
        name="position_dft",
    )(ctab, stab, z3, mc, ms)


TM_OUT = 512


BLK_PER_HALF = HALF_SEQ // TM_OUT
BLK_PER_SEQ = SEQ // TM_OUT


HALO = 16
assert HALO >= max(POOL_WINDOWS) // 2


def _pool_minus_identity(prev, cur, nxt, t0):
    xp = jnp.concatenate([prev, cur, nxt], axis=0)
    t = t0 + lax.broadcasted_iota(jnp.int32, (TM_OUT, 1), 0)
    parts = []
    for gi, w in enumerate(POOL_WINDOWS):
        cols = slice(gi * GROUP_DIM, (gi + 1) * GROUP_DIM)
        a, span = xp[:, cols], 1
        while span < w:
            n = a.shape[0] - span
            a = a[:n] + a[span:span + n]
            span *= 2
        start = HALO - w // 2
        cnt = (jnp.minimum(t + (w - w // 2), SEQ) - jnp.maximum(t - w // 2, 0)).astype(F32)
        parts.append(a[start:start + TM_OUT] / cnt - cur[:, cols])
    return jnp.concatenate(parts, axis=1)


def _out_proj_kernel(ylo_ref, yhi_ref, v_ref, vp_ref, vn_ref, wa_ref, wb_ref, x_ref, g_ref, wrs_ref, wrh_ref,
                     o_ref, h_ref, lg_ref):
    blk = pl.program_id(0) % BLK_PER_SEQ
    ya = jnp.where(blk >= BLK_PER_HALF, yhi_ref[...], ylo_ref[...])
    prev = jnp.where(blk > 0, vp_ref[...].astype(F32), 0.0)
    nxt = jnp.where(blk < BLK_PER_SEQ - 1, vn_ref[...].astype(F32), 0.0)
    yb = _pool_minus_identity(prev, v_ref[...].astype(F32), nxt, blk * TM_OUT).astype(BF16)
    x1 = x_ref[...] + _dot(ya, wa_ref[0]) + _dot(yb, wb_ref[0])
    o_ref[...] = x1
    inv = lax.rsqrt(jnp.mean(x1 * x1, axis=-1, keepdims=True) + RMS_EPS)
    h = (x1 * inv) * g_ref[...]
    h_ref[...] = h
    h_hi = h.astype(BF16)
    h_lo = (h - h_hi.astype(F32)).astype(BF16)
    contract_last = (((1,), (1,)), ((), ()))
    a = lax.dot_general(wrs_ref[...], h_hi, contract_last, preferred_element_type=F32)
    b = lax.dot_general(wrh_ref[...], h_lo, contract_last, preferred_element_type=F32)
    lg_ref[...] = a[:N_EXPERTS] + (a[N_EXPERTS:] + b)


def _out_proj(ya_lo, ya_hi, z2d, w_out_bf, x2d, gamma, wr_split, wr_hi, layer):
    row_blk = lambda i: (i, 0)
    fixed = lambda i: (0, 0)
    half_blk = lambda i: ((i // BLK_PER_SEQ) * BLK_PER_HALF + i % BLK_PER_HALF, 0)
    v_col = FOURIER_WIDTH // POOL_WIDTH
    halo_per_blk = TM_OUT // HALO
    halo_prev = lambda i: (jnp.maximum(i * halo_per_blk - 1, 0), v_col)
    halo_next = lambda i: (jnp.minimum((i + 1) * halo_per_blk, ROWS // HALO - 1), v_col)
    return pl.pallas_call(
        _out_proj_kernel,
        out_shape=(
            jax.ShapeDtypeStruct((ROWS, D_MODEL), F32),
            jax.ShapeDtypeStruct((ROWS, D_MODEL), F32),
            jax.ShapeDtypeStruct((N_EXPERTS, ROWS), F32),
        ),
        grid=(ROWS // TM_OUT,),
        in_specs=[
            pl.BlockSpec((TM_OUT, FOURIER_WIDTH), half_blk),
            pl.BlockSpec((TM_OUT, FOURIER_WIDTH), half_blk),
            pl.BlockSpec((TM_OUT, POOL_WIDTH), lambda i: (i, v_col)),
            pl.BlockSpec((HALO, POOL_WIDTH), halo_prev),
            pl.BlockSpec((HALO, POOL_WIDTH), halo_next),
            pl.BlockSpec((1, FOURIER_WIDTH, D_MODEL), lambda i: (layer, 0, 0)),
            pl.BlockSpec((1, POOL_WIDTH, D_MODEL), lambda i: (layer, 1, 0)),
            pl.BlockSpec((TM_OUT, D_MODEL), row_blk),
            pl.BlockSpec((1, D_MODEL), fixed),
            pl.BlockSpec((2 * N_EXPERTS, D_MODEL), fixed),
            pl.BlockSpec((N_EXPERTS, D_MODEL), fixed),
        ],
        out_specs=(
            pl.BlockSpec((TM_OUT, D_MODEL), row_blk),
            pl.BlockSpec((TM_OUT, D_MODEL), row_blk),
            pl.BlockSpec((N_EXPERTS, TM_OUT), lambda i: (0, i)),
        ),
        compiler_params=_cp(("parallel",)),
        name="out_proj_residual",
    )(ya_lo.reshape(BATCH * HALF_SEQ, FOURIER_WIDTH), ya_hi.reshape(BATCH * HALF_SEQ, FOURIER_WIDTH),
      z2d, z2d, z2d, w_out_bf, w_out_bf, x2d, gamma.reshape(1, D_MODEL), wr_split, wr_hi)


LANE = 128
NGRP = SEQ // LANE
GRP_SHIFT = 5
assert 1 << GRP_SHIFT == NGRP
ER = N_EXPERTS * NGRP
TINY = float(np.finfo(np.float32).tiny)
N_BISECT = 36
N_MCOL = 8
EXPERTS_PER_ITER = 4


def _select_kernel(lg_ref, idx_ref, gate_ref, enc_ref, m_sc, ci_sc, cg_sc):
    lg = lg_ref[...]
    ex = jnp.exp(lg - jnp.max(lg, axis=0, keepdims=True))
    p = ex / jnp.sum(ex, axis=0, keepdims=True)

    def total(v):
        return jnp.sum(jnp.sum(v, axis=1, keepdims=True), axis=2, keepdims=True)

    def bisect(_, lohi):
        lo, hi = lohi
        mid = jnp.sqrt(jnp.maximum(lo, TINY)) * jnp.sqrt(hi)
        ok = total(jnp.where(p >= mid, 1.0, 0.0)) >= CAP
        return jnp.where(ok, mid, lo), jnp.where(ok, hi, mid)

    lo0 = jnp.zeros((N_EXPERTS, 1, 1), F32)
    hi0 = jnp.full((N_EXPERTS, 1, 1), 2.0, F32)
    _, hi = lax.fori_loop(0, N_BISECT, bisect, (lo0, hi0))
    below = jnp.where(p < hi, p, -1.0)
    thr = jnp.max(jnp.max(below, axis=2, keepdims=True), axis=1, keepdims=True)
    gt = p > thr
    eq = p == thr
    need = CAP - total(jnp.where(gt, 1.0, 0.0))

    r_i = lax.broadcasted_iota(jnp.int32, (ER, ER), 0)
    c_i = lax.broadcasted_iota(jnp.int32, (ER, ER), 1)
    same_expert = (c_i >> GRP_SHIFT) == (r_i >> GRP_SHIFT)
    rows_before = jnp.where(jnp.logical_and(c_i < r_i, same_expert), 1.0, 0.0).astype(BF16)
    j_i = lax.broadcasted_iota(jnp.int32, (LANE, LANE), 0)
    l_i = lax.broadcasted_iota(jnp.int32, (LANE, LANE), 1)
    lanes_upto = jnp.where(j_i <= l_i, 1.0, 0.0).astype(BF16)
    ones = jnp.ones((LANE, LANE), BF16)

    def prefix(mask_b):
        within = _dot(mask_b, lanes_upto)
        rowtot = _dot(mask_b, ones)
        rowoff = _dot(rows_before, rowtot.astype(BF16))
        return within, rowoff

    eq_f = jnp.where(eq, 1.0, 0.0)
    w_eq, ro_eq = prefix(eq_f.astype(BF16).reshape(ER, LANE))
    eq_before = (w_eq + ro_eq).reshape(N_EXPERTS, NGRP, LANE) - eq_f
    sel = jnp.logical_or(gt, jnp.logical_and(eq, eq_before < need))
    sel_b = jnp.where(sel, 1.0, 0.0).astype(BF16).reshape(ER, LANE)
    within, rowoff = prefix(sel_b)
    count = within + rowoff
    enc_ref[0] = jnp.where(sel_b > 0, count, -count).astype(jnp.int32)

    half = jnp.floor(rowoff * 0.5)
    p2d = p.reshape(ER, LANE)
    p_1 = p2d.astype(BF16)
    rem = p2d - p_1.astype(F32)
    p_2 = rem.astype(BF16)
    p_3 = (rem - p_2.astype(F32)).astype(BF16)
    grp = (lax.broadcasted_iota(jnp.int32, (ER, LANE), 0) & (NGRP - 1)).astype(F32)
    blocks = [within.astype(BF16), grp.astype(BF16), half.astype(BF16), (rowoff - 2.0 * half).astype(BF16),
              p_1, p_2, p_3, sel_b]
    for k, blk in enumerate(blocks):
        m_sc[:, k * LANE:(k + 1) * LANE] = blk
    ci_sc[...] = jnp.zeros_like(ci_sc)
    cg_sc[...] = jnp.zeros_like(cg_sc)

    s_col = lax.broadcasted_iota(jnp.int32, (CAP, 1), 0).astype(F32)
    lane = lax.broadcasted_iota(jnp.int32, (CAP, LANE), 1)
    lane_f = lane.astype(F32)
    ones8 = jnp.ones((8, LANE), BF16)
    g_r = lax.broadcasted_iota(jnp.int32, (NGRP, NGRP), 0)
    g_c = lax.broadcasted_iota(jnp.int32, (NGRP, NGRP), 1)
    groups_before = jnp.where(g_r < g_c, 1.0, 0.0).astype(BF16)

    def one_expert(e):
        r0 = pl.multiple_of(e * NGRP, NGRP)
        table = m_sc[pl.ds(r0, NGRP), :]
        sel_e = table[:, 7 * LANE:]
        rt = lax.dot_general(ones8, sel_e, (((1,), (1,)), ((), ())), preferred_element_type=F32)
        ro = _dot(rt.astype(BF16), groups_before)
        start = ro[0:1]
        stop = start + rt[0:1]
        in_grp = jnp.logical_and(start <= s_col, s_col < stop)
        got = _dot(jnp.where(in_grp, 1.0, 0.0).astype(BF16), table[:, :7 * LANE])
        s_loc = s_col - (2.0 * got[:, 2 * LANE:3 * LANE] + got[:, 3 * LANE:4 * LANE])
        off = _dot(jnp.where(got[:, :LANE] <= s_loc, 1.0, 0.0).astype(BF16), ones)
        tok = got[:, LANE:2 * LANE] * LANE + off
        hit = lane_f == off
        gate = jnp.zeros((CAP, LANE), F32)
        for k in (4, 5, 6):
            gate = gate + _dot(jnp.where(hit, got[:, k * LANE:(k + 1) * LANE], 0.0).astype(BF16), ones)
        return tok, gate

    def per_group(i, carry):
        ci = ci_sc[...]
        cg = cg_sc[...]
        for k in range(EXPERTS_PER_ITER):
            e = i * EXPERTS_PER_ITER + k
            tok, gate = one_expert(e)
            ci = jnp.where(lane == e, tok, ci)
            cg = jnp.where(lane == e, gate, cg)
        ci_sc[...] = ci
        cg_sc[...] = cg
        return carry

    lax.fori_loop(0, N_EXPERTS // EXPERTS_PER_ITER, per_group, 0)
    idx_ref[0] = ci_sc[...].T[:N_EXPERTS].astype(jnp.int32)
    gate_ref[0] = cg_sc[...].T[:N_EXPERTS]


def _select(lg4):
    return pl.pallas_call(
        _select_kernel,
        out_shape=(
            jax.ShapeDtypeStruct((BATCH, N_EXPERTS, CAP), jnp.int32),
            jax.ShapeDtypeStruct((BATCH, N_EXPERTS, CAP), F32),
            jax.ShapeDtypeStruct((BATCH, ER, LANE), jnp.int32),
        ),
        grid=(BATCH,),
        in_specs=[pl.BlockSpec((N_EXPERTS, None, NGRP, LANE), lambda b: (0, b, 0, 0))],
        out_specs=(
            pl.BlockSpec((1, N_EXPERTS, CAP), lambda b: (b, 0, 0)),
            pl.BlockSpec((1, N_EXPERTS, CAP), lambda b: (b, 0, 0)),
            pl.BlockSpec((1, ER, LANE), lambda b: (b, 0, 0)),
        ),
        scratch_shapes=[
            pltpu.VMEM((ER, N_MCOL * LANE), BF16),
            pltpu.VMEM((CAP, LANE), F32),
            pltpu.VMEM((CAP, LANE), F32),
        ],
        compiler_params=_cp(("parallel",)),
        name="select_topc",
    )(lg4)


TF = 256
TD = 256
NF = D_EXPERT // TF
ND = D_MODEL // TD
M_CHUNK = 512
SUBLANES = 8
SUBLANE_SHIFT = 3
GATHER_ROWS = 512


def _ffn_kernel(idx_ref, meta_ref, h_hbm, wg_ref, wu_ref, wd_ref, o_ref, stage, xs, hid, sem):
    j = pl.program_id(1)

    @pl.when(j == 0)
    def _gather():
        def start_rows(u):
            buf = u % 2

            def issue(grp, c):
                base = pl.multiple_of(grp * SUBLANES, SUBLANES)
                for k in range(SUBLANES):
                    row = idx_ref[0, 0, u * GATHER_ROWS + base + k]
                    src = h_hbm.at[lax.shift_right_logical(row, SUBLANE_SHIFT), pl.ds(row & (SUBLANES - 1), 1), :]
                    pltpu.make_async_copy(src, stage.at[buf, grp, pl.ds(k, 1), :], sem.at[buf]).start()
                return c

            lax.fori_loop(0, GATHER_ROWS // SUBLANES, issue, 0)

        n_units = SLOTS // GATHER_ROWS
        start_rows(0)
        for u in range(n_units):
            if u + 1 < n_units:
                start_rows(u + 1)
            buf = u % 2
            pltpu.make_async_copy(h_hbm.at[pl.ds(0, GATHER_ROWS // SUBLANES)], stage.at[buf], sem.at[buf]).wait()
            xs[u * GATHER_ROWS:(u + 1) * GATHER_ROWS, :] = stage[buf].reshape(GATHER_ROWS, D_MODEL).astype(BF16)

        meta = meta_ref[0]
        lane = lax.broadcasted_iota(jnp.int32, (SLOTS, TD), 1)
        o_ref[0] = jnp.where(lane < TD // 2, meta[:, 1:2], meta[:, 2:3]).astype(o_ref.dtype)

    @pl.when(j < NF)
    def _up():
        wg = wg_ref[...].astype(BF16)
        wu = wu_ref[...].astype(BF16)
        for c in range(SLOTS // M_CHUNK):
            rows = slice(c * M_CHUNK, (c + 1) * M_CHUNK)
            x = xs[rows, :]
            g = _dot(x, wg)
            u = _dot(x, wu)
            hid[j, rows, :] = ((g * jax.nn.sigmoid(g)) * u).astype(BF16)

    @pl.when(j >= NF)
    def _down():
        wd = wd_ref[...].astype(BF16)
        for c in range(SLOTS // M_CHUNK):
            rows = slice(c * M_CHUNK, (c + 1) * M_CHUNK)
            hrows = jnp.concatenate([hid[k, rows, :] for k in range(NF)], axis=1)
            o_ref[0, rows, :] = (_dot(hrows, wd) * meta_ref[0, rows, 0:1]).astype(o_ref.dtype)


Y_WIDTH = D_MODEL + TD
N_META = 4


def _ffn(idx_rows, meta, h2, w_gate, w_up, w_down, layer):
    up_blk = lambda e, j: (layer, e, 0, jnp.minimum(j, NF - 1))
    dn_blk = lambda e, j: (layer, e, 0, jnp.maximum(j - NF, 0))
    out_blk = lambda e, j: (e, 0, jnp.where(j < NF, ND, j - NF))
    return pl.pallas_call(
        _ffn_kernel,
        out_shape=jax.ShapeDtypeStruct((N_EXPERTS, SLOTS, Y_WIDTH), BF16),
        grid=(N_EXPERTS, NF + ND),
        in_specs=[
            pl.BlockSpec((1, 1, SLOTS), lambda e, j: (e, 0, 0), memory_space=pltpu.SMEM),
            pl.BlockSpec((1, SLOTS, N_META), lambda e, j: (e, 0, 0)),
            pl.BlockSpec(memory_space=pl.ANY),
            pl.BlockSpec((None, None, D_MODEL, TF), up_blk),
            pl.BlockSpec((None, None, D_MODEL, TF), up_blk),
            pl.BlockSpec((None, None, D_EXPERT, TD), dn_blk),
        ],
        out_specs=pl.BlockSpec((1, SLOTS, TD), out_blk),
        scratch_shapes=[
            pltpu.VMEM((2, GATHER_ROWS // SUBLANES, SUBLANES, D_MODEL), F32),
            pltpu.VMEM((SLOTS, D_MODEL), BF16),
            pltpu.VMEM((NF, SLOTS, TF), BF16),
            pltpu.SemaphoreType.DMA((2,)),
        ],
        compiler_params=_cp(("arbitrary", "arbitrary")),
        name="expert_swiglu",
    )(idx_rows, meta, h2.reshape(ROWS // SUBLANES, SUBLANES, D_MODEL), w_gate, w_up, w_down)


TB_C = 256
NB_C = SEQ // TB_C
ROW_ALIGN = 16
ROW_SHIFT = 4
KC = 512
KC_SHIFT = 9
assert 1 << ROW_SHIFT == ROW_ALIGN and 1 << KC_SHIFT == KC
N_PHASE = 2
N_BUF = N_PHASE + 1
E_PER_PHASE = N_EXPERTS // N_PHASE
STAGE_MAX = E_PER_PHASE * (TB_C + ROW_ALIGN)
STAGE_ROWS = -(-STAGE_MAX // KC) * KC
N_STEPS_C = BATCH * NB_C


def _combine_kernel(bnd_ref, x_ref, g_ref, y_hbm, o_ref, stage, sem, rows_sm, *, final_norm):
    b = pl.program_id(0)
    tb = pl.program_id(1)
    step = b * NB_C + tb

    def chunk_copy(e, src, buf, dst):
        return pltpu.make_async_copy(y_hbm.at[e, pl.ds(src, ROW_ALIGN), :],
                                     stage.at[buf, pl.ds(dst, ROW_ALIGN), :], sem.at[buf])

    def issue(bq, tq, ph, buf):
        pos = jnp.int32(0)
        for e in range(ph * E_PER_PHASE, (ph + 1) * E_PER_PHASE):
            lo = bnd_ref[bq, tq, e]
            hi = bnd_ref[bq, tq + 1, e]
            lo_al = lo - (lo & (ROW_ALIGN - 1))
            nch = jnp.where(hi > lo, lax.shift_right_logical(hi - lo_al + (ROW_ALIGN - 1), ROW_SHIFT), 0)

            def start(c, carry, e=e, lo_al=lo_al, pos=pos):
                src = pl.multiple_of(bq * CAP + lo_al + c * ROW_ALIGN, ROW_ALIGN)
                dst = pl.multiple_of(pos + c * ROW_ALIGN, ROW_ALIGN)
                chunk_copy(e, src, buf, dst).start()
                return carry

            lax.fori_loop(0, nch, start, 0)
            pos = pos + nch * ROW_ALIGN
        rows_sm[buf] = pos

    @pl.when(step == 0)
    def _first():
        stage[...] = jnp.zeros_like(stage)
        for ph in range(N_PHASE):
            issue(b, tb, ph, ph)

    t0 = (tb * TB_C).astype(F32)
    lane = lax.broadcasted_iota(jnp.int32, (KC, LANE), 1).astype(F32)
    krow = lax.broadcasted_iota(jnp.int32, (KC, LANE), 0)
    wrap = tb + 1 == NB_C
    b_next = jnp.where(wrap, b + 1, b)
    tb_next = jnp.where(wrap, 0, tb + 1)

    for ph in range(N_PHASE):
        seq = step * N_PHASE + ph
        buf = lax.rem(seq, N_BUF)

        @pl.when(step + 1 < N_STEPS_C)
        def _prefetch(ph=ph, seq=seq):
            issue(b_next, tb_next, ph, lax.rem(seq + N_PHASE, N_BUF))

        rows = rows_sm[buf]

        def drain(c, carry, buf=buf):
            chunk_copy(0, 0, buf, 0).wait()
            return carry

        lax.fori_loop(0, lax.shift_right_logical(rows, ROW_SHIFT), drain, 0)

        def scattered(kc, buf=buf, rows=rows):
            k0 = pl.multiple_of(kc * KC, KC)
            blk = stage[buf, pl.ds(k0, KC), :]
            tok = blk[:, D_MODEL:D_MODEL + LANE].astype(F32) * 64.0 + blk[:, D_MODEL + LANE:].astype(F32) - t0
            live = krow + k0 < rows
            hits = [jnp.logical_and(live, tok - float(q * LANE) == lane) for q in range(TB_C // LANE)]
            onehot_t = jnp.where(jnp.concatenate(hits, axis=1), 1.0, 0.0).astype(BF16)
            return lax.dot_general(onehot_t, blk[:, :D_MODEL], (((0,), (0,)), ((), ())), preferred_element_type=F32)

        def accumulate(kc, carry):
            o_ref[...] += scattered(kc)
            return carry

        n_chunks = lax.shift_right_logical(rows + (KC - 1), KC_SHIFT)
        if ph == 0:
            o_ref[...] = x_ref[...] + scattered(0)
            lax.fori_loop(1, n_chunks, accumulate, 0)
        else:
            lax.fori_loop(0, n_chunks, accumulate, 0)

    if final_norm:
        x = o_ref[...]
        inv = lax.rsqrt(jnp.mean(x * x, axis=-1, keepdims=True) + RMS_EPS)
        o_ref[...] = (x * inv) * g_ref[...]


def _combine(bnd, x1_2d, gamma, ysg, final_norm):
    grid_spec = pltpu.PrefetchScalarGridSpec(
        num_scalar_prefetch=1,
        grid=(BATCH, NB_C),
        in_specs=[
            pl.BlockSpec((TB_C, D_MODEL), lambda b, t, bnd: (b * NB_C + t, 0)),
            pl.BlockSpec((1, D_MODEL), lambda b, t, bnd: (0, 0)),
            pl.BlockSpec(memory_space=pl.ANY),
        ],
        out_specs=pl.BlockSpec((TB_C, D_MODEL), lambda b, t, bnd: (b * NB_C + t, 0)),
        scratch_shapes=[
            pltpu.VMEM((N_BUF, STAGE_ROWS, Y_WIDTH), BF16),
            pltpu.SemaphoreType.DMA((N_BUF,)),
            pltpu.SMEM((N_BUF,), jnp.int32),
        ],
    )
    return pl.pallas_call(
        functools.partial(_combine_kernel, final_norm=final_norm),
        out_shape=jax.ShapeDtypeStruct((ROWS, D_MODEL), F32),
        grid_spec=grid_spec,
        compiler_params=_cp(("arbitrary", "arbitrary")),
        name="combine_final" if final_norm else "combine",
    )(bnd, x1_2d, gamma.reshape(1, D_MODEL), ysg)


def kernel(x, norm1_g, w_in, w_fourier, w_pool, pool_scale, w_out, norm2_g, w_router, w_gate, w_up, w_down, final_g):
    wcat = _fold_in_weights(w_in, w_pool, pool_scale)
    mc, ms = _fold_head_weights(w_fourier)
    ctab, stab = _dft_tables()
    w_out_bf = w_out.astype(BF16)
    w_router_t = jnp.swapaxes(w_router, 1, 2)
    wr_hi = w_router_t.astype(BF16)
    wr_lo = (w_router_t - wr_hi.astype(F32)).astype(BF16)
    wr_split = jnp.concatenate([wr_hi, wr_lo], axis=1)
    batch_base = (jnp.arange(BATCH, dtype=jnp.int32) * SEQ)[:, None, None]

    xc = x.reshape(ROWS, D_MODEL)
    for layer in range(DEPTH):
        z = _norm_mm(xc, norm1_g[layer], wcat, layer)
        z3 = z.reshape(BATCH, SEQ, Z_WIDTH)
        ya_lo, ya_hi = _dft(ctab, stab, z3, mc, ms, layer)
        x1, h2, lg = _out_proj(ya_lo, ya_hi, z, w_out_bf, xc,
                               norm2_g[layer], wr_split[layer], wr_hi[layer], layer)
        idx, gates, enc = _select(lg.reshape(N_EXPERTS, BATCH, NGRP, LANE))

        idx_rows = jnp.swapaxes(idx + batch_base, 0, 1).reshape(N_EXPERTS, 1, SLOTS)
        meta = jnp.stack([gates, (idx >> 6).astype(F32), (idx & 63).astype(F32), jnp.zeros_like(gates)], axis=-1)
        meta = jnp.swapaxes(meta, 0, 1).reshape(N_EXPERTS, SLOTS, N_META)
        counts = jnp.abs(enc).reshape(BATCH, N_EXPERTS, SEQ)
        ends = counts[:, :, TB_C - 1::TB_C]
        bnd = jnp.concatenate([jnp.zeros((BATCH, N_EXPERTS, 1), jnp.int32), ends], axis=2)
        bnd = jnp.swapaxes(bnd, 1, 2)

        ysg = _ffn(idx_rows, meta, h2, w_gate, w_up, w_down, layer)
        xc = _combine(bnd, x1, final_g, ysg, layer == DEPTH - 1)
    return xc.reshape(BATCH, SEQ, D_MODEL)
```

```python
import functools

import numpy as np
import jax
import jax.numpy as jnp
from jax import lax
from jax.experimental import pallas as pl
from jax.experimental.pallas import tpu as pltpu

D_MODEL = 2048
BATCH = 4
SEQ = 4096
DEPTH = 2
N_HEADS = 4
HEAD_DIM = 256
POOL_WINDOWS = (2, 4, 8, 16)
N_GROUPS = 4
GROUP_DIM = 256
FOURIER_WIDTH = N_HEADS * HEAD_DIM
POOL_WIDTH = N_GROUPS * GROUP_DIM
N_EXPERTS = 16
CAP = 2 * SEQ // N_EXPERTS
SLOTS = BATCH * CAP
D_EXPERT = D_MODEL
RMS_EPS = 1e-6
ROWS = BATCH * SEQ

F32 = jnp.float32
BF16 = jnp.bfloat16
HIGHEST = lax.Precision.HIGHEST

VMEM_LIMIT = 54 * 1024 * 1024


def _cp(sem, vmem=VMEM_LIMIT):
    return pltpu.CompilerParams(dimension_semantics=sem, vmem_limit_bytes=vmem)


def _dot(a, b):
    return jnp.dot(a, b, preferred_element_type=F32)


N_FOLD = N_HEADS + N_GROUPS
Z_WIDTH = N_FOLD * HEAD_DIM


def _fold_in_kernel(win_ref, r_ref, sc_ref, o_ref):
    j = pl.program_id(1)

    @pl.when(j < N_HEADS)
    def _fourier():
        o_ref[0] = win_ref[0].astype(BF16)

    @pl.when(j >= N_HEADS)
    def _pool():
        t = r_ref[0, 0] * sc_ref[0, 0]
        w = win_ref[0]
        w_hi = w.astype(BF16)
        w_lo = (w - w_hi.astype(F32)).astype(BF16)
        t_hi = t.astype(BF16)
        t_lo = (t - t_hi.astype(F32)).astype(BF16)
        o_ref[0] = (_dot(w_hi, t_hi) + (_dot(w_lo, t_hi) + _dot(w_hi, t_lo))).astype(BF16)


def _fold_in_weights(w_in, w_pool, pool_scale):
    grp = lambda j: jnp.maximum(j - N_HEADS, 0)
    return pl.pallas_call(
        _fold_in_kernel,
        out_shape=jax.ShapeDtypeStruct((DEPTH, D_MODEL, Z_WIDTH), BF16),
        grid=(DEPTH, N_FOLD),
        in_specs=[
            pl.BlockSpec((1, D_MODEL, HEAD_DIM), lambda l, j: (l, 0, j)),
            pl.BlockSpec((1, 1, GROUP_DIM, GROUP_DIM), lambda l, j: (l, grp(j), 0, 0)),
            pl.BlockSpec((1, 1, 1, GROUP_DIM), lambda l, j: (l, grp(j), 0, 0)),
        ],
        out_specs=pl.BlockSpec((1, D_MODEL, HEAD_DIM), lambda l, j: (l, 0, j)),
        compiler_params=_cp(("parallel", "parallel")),
        name="fold_in_weights",
    )(w_in, w_pool, pool_scale.reshape(DEPTH, N_GROUPS, 1, GROUP_DIM))


def _fold_head_kernel(ct_ref, st_ref, wf_ref, mc_ref, ms_ref):
    wf = wf_ref[0, 0]
    mc_ref[0, 0] = jnp.dot(ct_ref[...], wf, precision=HIGHEST, preferred_element_type=F32).astype(BF16)
    ms_ref[0, 0] = jnp.dot(st_ref[...], wf, precision=HIGHEST, preferred_element_type=F32).astype(BF16)


def _fold_head_weights(w_fourier):
    c = np.arange(HEAD_DIM)
    ang = 2.0 * np.pi * ((c[:, None] * c[None, :]) % HEAD_DIM) / HEAD_DIM
    ctab = jnp.asarray((np.cos(ang) / np.sqrt(HEAD_DIM)).astype(np.float32))
    stab = jnp.asarray((np.sin(ang) / np.sqrt(HEAD_DIM)).astype(np.float32))
    tab = pl.BlockSpec((HEAD_DIM, HEAD_DIM), lambda l, h: (0, 0))
    blk = pl.BlockSpec((1, 1, HEAD_DIM, HEAD_DIM), lambda l, h: (l, h, 0, 0))
    out = jax.ShapeDtypeStruct((DEPTH, N_HEADS, HEAD_DIM, HEAD_DIM), BF16)
    return pl.pallas_call(
        _fold_head_kernel,
        out_shape=(out, out),
        grid=(DEPTH, N_HEADS),
        in_specs=[tab, tab, blk],
        out_specs=(blk, blk),
        compiler_params=_cp(("parallel", "parallel")),
        name="fold_head_weights",
    )(ctab, stab, w_fourier)


TM_IN = 512


def _norm_mm_kernel(x_ref, g_ref, w_ref, o_ref):
    x = x_ref[...]
    inv = lax.rsqrt(jnp.mean(x * x, axis=-1, keepdims=True) + RMS_EPS)
    h = ((x * inv) * g_ref[...]).astype(BF16)
    o_ref[...] = _dot(h, w_ref[0]).astype(o_ref.dtype)


def _norm_mm(x2d, gamma, wcat, layer):
    return pl.pallas_call(
        _norm_mm_kernel,
        out_shape=jax.ShapeDtypeStruct((ROWS, Z_WIDTH), BF16),
        grid=(ROWS // TM_IN,),
        in_specs=[
            pl.BlockSpec((TM_IN, D_MODEL), lambda i: (i, 0)),
            pl.BlockSpec((1, D_MODEL), lambda i: (0, 0)),
            pl.BlockSpec((1, D_MODEL, Z_WIDTH), lambda i: (layer, 0, 0)),
        ],
        out_specs=pl.BlockSpec((TM_IN, Z_WIDTH), lambda i: (i, 0)),
        compiler_params=_cp(("parallel",)),
        name="norm_in_proj",
    )(x2d, gamma.reshape(1, D_MODEL), wcat)


TM_DFT = 512
TN_DFT = 512
HALF_SEQ = SEQ // 2
N_KBLK = HALF_SEQ // TM_DFT
TAB_GROUP = 16
KX = TM_DFT + TAB_GROUP
COARSE = 64
N_KH = 40


def _table_kernel(ac_ref, as_ref, bc_ref, bs_ref, c_ref, s_ref):
    i = pl.program_id(0)

    def group(gi, carry):
        r0 = pl.multiple_of(gi * TAB_GROUP, TAB_GROUP)
        k0 = i * TM_DFT + r0
        kh = lax.shift_right_logical(k0, 6)
        kl = pl.multiple_of(k0 & (COARSE - 1), TAB_GROUP)
        ca = ac_ref[pl.ds(kh, 1), :]
        sa = as_ref[pl.ds(kh, 1), :]
        cb = bc_ref[pl.ds(kl, TAB_GROUP), :]
        sb = bs_ref[pl.ds(kl, TAB_GROUP), :]
        c_ref[0, pl.ds(r0, TAB_GROUP), :] = (ca * cb - sa * sb).astype(BF16)
        s_ref[0, pl.ds(r0, TAB_GROUP), :] = (sa * cb + ca * sb).astype(BF16)
        return carry

    lax.fori_loop(0, KX // TAB_GROUP, group, 0)


def _dft_tables():
    n = jnp.arange(SEQ, dtype=jnp.int32)[None, :]
    kh = jnp.arange(N_KH, dtype=jnp.int32)[:, None]
    kl = jnp.arange(COARSE, dtype=jnp.int32)[:, None]
    alpha = ((kh * n) % (SEQ // COARSE)).astype(F32) * (2.0 * np.pi * COARSE / SEQ)
    beta = ((kl * n) % SEQ).astype(F32) * (2.0 * np.pi / SEQ)
    scale = 1.0 / np.sqrt(SEQ)
    full = lambda rows: pl.BlockSpec((rows, SEQ), lambda i: (0, 0))
    out = pl.BlockSpec((1, KX, SEQ), lambda i: (i, 0, 0))
    return pl.pallas_call(
        _table_kernel,
        out_shape=(jax.ShapeDtypeStruct((N_KBLK, KX, SEQ), BF16),) * 2,
        grid=(N_KBLK,),
        in_specs=[full(N_KH), full(N_KH), full(COARSE), full(COARSE)],
        out_specs=(out, out),
        compiler_params=_cp(("parallel",)),
        name="dft_tables",
    )(jnp.cos(alpha), jnp.sin(alpha), jnp.cos(beta) * scale, jnp.sin(beta) * scale)


HEADS_PER_BLK = TN_DFT // HEAD_DIM


def _dft_kernel(c_ref, s_ref, u_ref, mc_ref, ms_ref, lo_ref, hi_ref):
    u = u_ref[0]
    pc = _dot(c_ref[0], u).astype(BF16)
    ps = _dot(s_ref[0], u).astype(BF16)
    heads = lambda v, m_ref: jnp.concatenate(
        [_dot(v[:, h * HEAD_DIM:(h + 1) * HEAD_DIM], m_ref[h]) for h in range(HEADS_PER_BLK)], axis=1)
    a = heads(pc, mc_ref)
    bq = heads(ps, ms_ref)
    lo_ref[0] = (a[:TM_DFT] - bq[:TM_DFT]).astype(BF16)
    mirrored = (a + bq).astype(BF16)
    u = lax.broadcasted_iota(jnp.int32, (TM_DFT, KX), 0)
    r = lax.broadcasted_iota(jnp.int32, (TM_DFT, KX), 1)
    flip = jnp.where(r == TM_DFT - u, 1.0, 0.0).astype(BF16)
    hi_ref[0] = _dot(flip, mirrored).astype(BF16)


def _dft(ctab, stab, z3, mc, ms, layer):
    nq = FOURIER_WIDTH // TN_DFT
    half = jax.ShapeDtypeStruct((BATCH, HALF_SEQ, FOURIER_WIDTH), BF16)
    head_blk = pl.BlockSpec((None, HEADS_PER_BLK, HEAD_DIM, HEAD_DIM), lambda i, b, n: (layer, n, 0, 0))
    return pl.pallas_call(
        _dft_kernel,
        out_shape=(half, half),
        grid=(N_KBLK, BATCH, nq),
        in_specs=[
            pl.BlockSpec((1, KX, SEQ), lambda i, b, n: (i, 0, 0)),
            pl.BlockSpec((1, KX, SEQ), lambda i, b, n: (i, 0, 0)),
            pl.BlockSpec((1, SEQ, TN_DFT), lambda i, b, n: (b, 0, n)),
            head_blk,
            head_blk,
        ],
        out_specs=(
            pl.BlockSpec((1, TM_DFT, TN_DFT), lambda i, b, n: (b, i, n)),
            pl.BlockSpec((1, TM_DFT, TN_DFT), lambda i, b, n: (b, N_KBLK - 1 - i, n)),
        ),
        compiler_params=_cp(("parallel", "parallel", "parallel")),
        name="position_dft",
    )(ctab, stab, z3, mc, ms)


TM_OUT = 512


BLK_PER_HALF = HALF_SEQ // TM_OUT
BLK_PER_SEQ = SEQ // TM_OUT


HALO = 16
assert HALO >= max(POOL_WINDOWS) // 2


def _pool_minus_identity(prev, cur, nxt, t0):
    xp = jnp.concatenate([prev, cur, nxt], axis=0)
    t = t0 + lax.broadcasted_iota(jnp.int32, (TM_OUT, 1), 0)
    parts = []
    for gi, w in enumerate(POOL_WINDOWS):
        cols = slice(gi * GROUP_DIM, (gi + 1) * GROUP_DIM)
        a, span = xp[:, cols], 1
        while span < w:
            n = a.shape[0] - span
            a = a[:n] + a[span:span + n]
            span *= 2
        start = HALO - w // 2
        cnt = (jnp.minimum(t + (w - w // 2), SEQ) - jnp.maximum(t - w // 2, 0)).astype(F32)
        parts.append(a[start:start + TM_OUT] / cnt - cur[:, cols])
    return jnp.concatenate(parts, axis=1)


def _out_proj_kernel(ylo_ref, yhi_ref, v_ref, vp_ref, vn_ref, wa_ref, wb_ref, x_ref, g_ref, wrs_ref, wrh_ref,
                     o_ref, h_ref, lg_ref):
    blk = pl.program_id(0) % BLK_PER_SEQ
    ya = jnp.where(blk >= BLK_PER_HALF, yhi_ref[...], ylo_ref[...])
    prev = jnp.where(blk > 0, vp_ref[...].astype(F32), 0.0)
    nxt = jnp.where(blk < BLK_PER_SEQ - 1, vn_ref[...].astype(F32), 0.0)
    yb = _pool_minus_identity(prev, v_ref[...].astype(F32), nxt, blk * TM_OUT).astype(BF16)
    x1 = x_ref[...] + _dot(ya, wa_ref[0]) + _dot(yb, wb_ref[0])
    o_ref[...] = x1
    inv = lax.rsqrt(jnp.mean(x1 * x1, axis=-1, keepdims=True) + RMS_EPS)
    h = (x1 * inv) * g_ref[...]
    h_ref[...] = h
    h_hi = h.astype(BF16)
    h_lo = (h - h_hi.astype(F32)).astype(BF16)
    contract_last = (((1,), (1,)), ((), ()))
    a = lax.dot_general(wrs_ref[...], h_hi, contract_last, preferred_element_type=F32)
    b = lax.dot_general(wrh_ref[...], h_lo, contract_last, preferred_element_type=F32)
    lg_ref[...] = a[:N_EXPERTS] + (a[N_EXPERTS:] + b)


def _out_proj(ya_lo, ya_hi, z2d, w_out_bf, x2d, gamma, wr_split, wr_hi, layer):
    row_blk = lambda i: (i, 0)
    fixed = lambda i: (0, 0)
    half_blk = lambda i: ((i // BLK_PER_SEQ) * BLK_PER_HALF + i % BLK_PER_HALF, 0)
    v_col = FOURIER_WIDTH // POOL_WIDTH
    halo_per_blk = TM_OUT // HALO
    halo_prev = lambda i: (jnp.maximum(i * halo_per_blk - 1, 0), v_col)
    halo_next = lambda i: (jnp.minimum((i + 1) * halo_per_blk, ROWS // HALO - 1), v_col)
    return pl.pallas_call(
        _out_proj_kernel,
        out_shape=(
            jax.ShapeDtypeStruct((ROWS, D_MODEL), F32),
            jax.ShapeDtypeStruct((ROWS, D_MODEL), F32),
            jax.ShapeDtypeStruct((N_EXPERTS, ROWS), F32),
        ),
        grid=(ROWS // TM_OUT,),
        in_specs=[
            pl.BlockSpec((TM_OUT, FOURIER_WIDTH), half_blk),
            pl.BlockSpec((TM_OUT, FOURIER_WIDTH), half_blk),
            pl.BlockSpec((TM_OUT, POOL_WIDTH), lambda i: (i, v_col)),
            pl.BlockSpec((HALO, POOL_WIDTH), halo_prev),
            pl.BlockSpec((HALO, POOL_WIDTH), halo_next),
            pl.BlockSpec((1, FOURIER_WIDTH, D_MODEL), lambda i: (layer, 0, 0)),
            pl.BlockSpec((1, POOL_WIDTH, D_MODEL), lambda i: (layer, 1, 0)),
            pl.BlockSpec((TM_OUT, D_MODEL), row_blk),
            pl.BlockSpec((1, D_MODEL), fixed),
            pl.BlockSpec((2 * N_EXPERTS, D_MODEL), fixed),
            pl.BlockSpec((N_EXPERTS, D_MODEL), fixed),
        ],
        out_specs=(
            pl.BlockSpec((TM_OUT, D_MODEL), row_blk),
            pl.BlockSpec((TM_OUT, D_MODEL), row_blk),
            pl.BlockSpec((N_EXPERTS, TM_OUT), lambda i: (0, i)),
        ),
        compiler_params=_cp(("parallel",)),
        name="out_proj_residual",
    )(ya_lo.reshape(BATCH * HALF_SEQ, FOURIER_WIDTH), ya_hi.reshape(BATCH * HALF_SEQ, FOURIER_WIDTH),
      z2d, z2d, z2d, w_out_bf, w_out_bf, x2d, gamma.reshape(1, D_MODEL), wr_split, wr_hi)


LANE = 128
NGRP = SEQ // LANE
GRP_SHIFT = 5
assert 1 << GRP_SHIFT == NGRP
ER = N_EXPERTS * NGRP
TINY = float(np.finfo(np.float32).tiny)
N_BISECT = 36
N_MCOL = 8
EXPERTS_PER_ITER = 4


def _select_kernel(lg_ref, idx_ref, gate_ref, enc_ref, m_sc, ci_sc, cg_sc):
    lg = lg_ref[...]
    ex = jnp.exp(lg - jnp.max(lg, axis=0, keepdims=True))
    p_all = ex / jnp.sum(ex, axis=0, keepdims=True)

    def total(v):
        return jnp.sum(jnp.sum(v, axis=-2, keepdims=True), axis=-1, keepdims=True)

    def bisect(_, lohi):
        lo, hi = lohi
        mid = jnp.sqrt(jnp.maximum(lo, TINY)) * jnp.sqrt(hi)
        ok = total(jnp.where(p_all >= mid, 1.0, 0.0)) >= CAP
        return jnp.where(ok, mid, lo), jnp.where(ok, hi, mid)

    lo0 = jnp.zeros((N_EXPERTS, BATCH, 1, 1), F32)
    hi0 = jnp.full((N_EXPERTS, BATCH, 1, 1), 2.0, F32)
    _, hi_all = lax.fori_loop(0, N_BISECT, bisect, (lo0, hi0))
    for b in range(BATCH):
        _select_sequence(p_all[:, b], hi_all[:, b], total, idx_ref.at[b], gate_ref.at[b], enc_ref.at[b],
                         m_sc, ci_sc, cg_sc)


def _select_sequence(p, hi, total, idx_ref, gate_ref, enc_ref, m_sc, ci_sc, cg_sc):
    below = jnp.where(p < hi, p, -1.0)
    thr = jnp.max(jnp.max(below, axis=2, keepdims=True), axis=1, keepdims=True)
    gt = p > thr
    eq = p == thr
    need = CAP - total(jnp.where(gt, 1.0, 0.0))

    r_i = lax.broadcasted_iota(jnp.int32, (ER, ER), 0)
    c_i = lax.broadcasted_iota(jnp.int32, (ER, ER), 1)
    same_expert = (c_i >> GRP_SHIFT) == (r_i >> GRP_SHIFT)
    rows_before = jnp.where(jnp.logical_and(c_i < r_i, same_expert), 1.0, 0.0).astype(BF16)
    j_i = lax.broadcasted_iota(jnp.int32, (LANE, LANE), 0)
    l_i = lax.broadcasted_iota(jnp.int32, (LANE, LANE), 1)
    lanes_upto = jnp.where(j_i <= l_i, 1.0, 0.0).astype(BF16)
    ones = jnp.ones((LANE, LANE), BF16)

    def prefix(mask_b):
        within = _dot(mask_b, lanes_upto)
        rowtot = _dot(mask_b, ones)
        rowoff = _dot(rows_before, rowtot.astype(BF16))
        return within, rowoff

    eq_f = jnp.where(eq, 1.0, 0.0)
    w_eq, ro_eq = prefix(eq_f.astype(BF16).reshape(ER, LANE))
    eq_before = (w_eq + ro_eq).reshape(N_EXPERTS, NGRP, LANE) - eq_f
    sel = jnp.logical_or(gt, jnp.logical_and(eq, eq_before < need))
    sel_b = jnp.where(sel, 1.0, 0.0).astype(BF16).reshape(ER, LANE)
    within, rowoff = prefix(sel_b)
    count = within + rowoff
    enc_ref[...] = jnp.where(sel_b > 0, count, -count).astype(jnp.int32)

    half = jnp.floor(rowoff * 0.5)
    p2d = p.reshape(ER, LANE)
    p_1 = p2d.astype(BF16)
    rem = p2d - p_1.astype(F32)
    p_2 = rem.astype(BF16)
    p_3 = (rem - p_2.astype(F32)).astype(BF16)
    grp = (lax.broadcasted_iota(jnp.int32, (ER, LANE), 0) & (NGRP - 1)).astype(F32)
    blocks = [within.astype(BF16), grp.astype(BF16), half.astype(BF16), (rowoff - 2.0 * half).astype(BF16),
              p_1, p_2, p_3, sel_b]
    for k, blk in enumerate(blocks):
        m_sc[:, k * LANE:(k + 1) * LANE] = blk
    ci_sc[...] = jnp.zeros_like(ci_sc)
    cg_sc[...] = jnp.zeros_like(cg_sc)

    s_col = lax.broadcasted_iota(jnp.int32, (CAP, 1), 0).astype(F32)
    lane = lax.broadcasted_iota(jnp.int32, (CAP, LANE), 1)
    lane_f = lane.astype(F32)
    ones8 = jnp.ones((8, LANE), BF16)
    g_r = lax.broadcasted_iota(jnp.int32, (NGRP, NGRP), 0)
    g_c = lax.broadcasted_iota(jnp.int32, (NGRP, NGRP), 1)
    groups_before = jnp.where(g_r < g_c, 1.0, 0.0).astype(BF16)

    def one_expert(e):
        r0 = pl.multiple_of(e * NGRP, NGRP)
        table = m_sc[pl.ds(r0, NGRP), :]
        sel_e = table[:, 7 * LANE:]
        rt = lax.dot_general(ones8, sel_e, (((1,), (1,)), ((), ())), preferred_element_type=F32)
        ro = _dot(rt.astype(BF16), groups_before)
        start = ro[0:1]
        stop = start + rt[0:1]
        in_grp = jnp.logical_and(start <= s_col, s_col < stop)
        got = _dot(jnp.where(in_grp, 1.0, 0.0).astype(BF16), table[:, :7 * LANE])
        s_loc = s_col - (2.0 * got[:, 2 * LANE:3 * LANE] + got[:, 3 * LANE:4 * LANE])
        off = _dot(jnp.where(got[:, :LANE] <= s_loc, 1.0, 0.0).astype(BF16), ones)
        tok = got[:, LANE:2 * LANE] * LANE + off
        hit = lane_f == off
        gate = jnp.zeros((CAP, LANE), F32)
        for k in (4, 5, 6):
            gate = gate + _dot(jnp.where(hit, got[:, k * LANE:(k + 1) * LANE], 0.0).astype(BF16), ones)
        return tok, gate

    def per_group(i, carry):
        ci = ci_sc[...]
        cg = cg_sc[...]
        for k in range(EXPERTS_PER_ITER):
            e = i * EXPERTS_PER_ITER + k
            tok, gate = one_expert(e)
            ci = jnp.where(lane == e, tok, ci)
            cg = jnp.where(lane == e, gate, cg)
        ci_sc[...] = ci
        cg_sc[...] = cg
        return carry

    lax.fori_loop(0, N_EXPERTS // EXPERTS_PER_ITER, per_group, 0)
    idx_ref[...] = ci_sc[...].T[:N_EXPERTS].astype(jnp.int32)
    gate_ref[...] = cg_sc[...].T[:N_EXPERTS]


def _select(lg4):
    whole = lambda shape: pl.BlockSpec(shape, lambda i: (0,) * len(shape))
    return pl.pallas_call(
        _select_kernel,
        out_shape=(
            jax.ShapeDtypeStruct((BATCH, N_EXPERTS, CAP), jnp.int32),
            jax.ShapeDtypeStruct((BATCH, N_EXPERTS, CAP), F32),
            jax.ShapeDtypeStruct((BATCH, ER, LANE), jnp.int32),
        ),
        grid=(1,),
        in_specs=[whole((N_EXPERTS, BATCH, NGRP, LANE))],
        out_specs=(
            whole((BATCH, N_EXPERTS, CAP)),
            whole((BATCH, N_EXPERTS, CAP)),
            whole((BATCH, ER, LANE)),
        ),
        scratch_shapes=[
            pltpu.VMEM((ER, N_MCOL * LANE), BF16),
            pltpu.VMEM((CAP, LANE), F32),
            pltpu.VMEM((CAP, LANE), F32),
        ],
        compiler_params=_cp(("parallel",)),
        name="select_topc",
    )(lg4)


TF = 256
TD = 256
NF = D_EXPERT // TF
ND = D_MODEL // TD
M_CHUNK = 512
SUBLANES = 8
SUBLANE_SHIFT = 3
GATHER_ROWS = 512


def _ffn_kernel(tile_ref, sub_ref, meta_ref, h_hbm, wg_ref, wu_ref, wd_ref, o_ref, stage, xs, hid, sem):
    j = pl.program_id(1)

    @pl.when(j == 0)
    def _gather():
        def start_rows(u):
            buf = u % 2

            def issue(grp, c):
                base = pl.multiple_of(grp * SUBLANES, SUBLANES)
                for k in range(SUBLANES):
                    slot = u * GATHER_ROWS + base + k
                    src = h_hbm.at[tile_ref[0, 0, slot], pl.ds(sub_ref[0, 0, slot], 1), :]
                    pltpu.make_async_copy(src, stage.at[buf, grp, pl.ds(k, 1), :], sem.at[buf]).start()
                return c

            lax.fori_loop(0, GATHER_ROWS // SUBLANES, issue, 0)

        n_units = SLOTS // GATHER_ROWS
        start_rows(0)
        for u in range(n_units):
            if u + 1 < n_units:
                start_rows(u + 1)
            buf = u % 2
            pltpu.make_async_copy(h_hbm.at[pl.ds(0, GATHER_ROWS // SUBLANES)], stage.at[buf], sem.at[buf]).wait()
            xs[u * GATHER_ROWS:(u + 1) * GATHER_ROWS, :] = stage[buf].reshape(GATHER_ROWS, D_MODEL).astype(BF16)

        meta = meta_ref[0]
        lane = lax.broadcasted_iota(jnp.int32, (SLOTS, TD), 1)
        o_ref[0] = jnp.where(lane < TD // 2, meta[:, 1:2], meta[:, 2:3]).astype(o_ref.dtype)

    @pl.when(j < NF)
    def _up():
        wg = wg_ref[...].astype(BF16)
        wu = wu_ref[...].astype(BF16)
        for c in range(SLOTS // M_CHUNK):
            rows = slice(c * M_CHUNK, (c + 1) * M_CHUNK)
            x = xs[rows, :]
            g = _dot(x, wg)
            u = _dot(x, wu)
            hid[j, rows, :] = ((g * jax.nn.sigmoid(g)) * u).astype(BF16)

    @pl.when(j >= NF)
    def _down():
        wd = wd_ref[...].astype(BF16)
        for c in range(SLOTS // M_CHUNK):
            rows = slice(c * M_CHUNK, (c + 1) * M_CHUNK)
            hrows = jnp.concatenate([hid[k, rows, :] for k in range(NF)], axis=1)
            o_ref[0, rows, :] = (_dot(hrows, wd) * meta_ref[0, rows, 0:1]).astype(o_ref.dtype)


Y_WIDTH = D_MODEL + TD
N_META = 4


def _ffn(idx_rows, meta, h2, w_gate, w_up, w_down, layer):
    up_blk = lambda e, j: (layer, e, 0, jnp.minimum(j, NF - 1))
    dn_blk = lambda e, j: (layer, e, 0, jnp.maximum(j - NF, 0))
    out_blk = lambda e, j: (e, 0, jnp.where(j < NF, ND, j - NF))
    return pl.pallas_call(
        _ffn_kernel,
        out_shape=jax.ShapeDtypeStruct((N_EXPERTS, SLOTS, Y_WIDTH), BF16),
        grid=(N_EXPERTS, NF + ND),
        in_specs=[
            pl.BlockSpec((1, 1, SLOTS), lambda e, j: (e, 0, 0), memory_space=pltpu.SMEM),
            pl.BlockSpec((1, 1, SLOTS), lambda e, j: (e, 0, 0), memory_space=pltpu.SMEM),
            pl.BlockSpec((1, SLOTS, N_META), lambda e, j: (e, 0, 0)),
            pl.BlockSpec(memory_space=pl.ANY),
            pl.BlockSpec((None, None, D_MODEL, TF), up_blk),
            pl.BlockSpec((None, None, D_MODEL, TF), up_blk),
            pl.BlockSpec((None, None, D_EXPERT, TD), dn_blk),
        ],
        out_specs=pl.BlockSpec((1, SLOTS, TD), out_blk),
        scratch_shapes=[
            pltpu.VMEM((2, GATHER_ROWS // SUBLANES, SUBLANES, D_MODEL), F32),
            pltpu.VMEM((SLOTS, D_MODEL), BF16),
            pltpu.VMEM((NF, SLOTS, TF), BF16),
            pltpu.SemaphoreType.DMA((2,)),
        ],
        compiler_params=_cp(("arbitrary", "arbitrary")),
        name="expert_swiglu",
    )(idx_rows >> SUBLANE_SHIFT, idx_rows & (SUBLANES - 1), meta,
      h2.reshape(ROWS // SUBLANES, SUBLANES, D_MODEL), w_gate, w_up, w_down)


TB_C = 256
NB_C = SEQ // TB_C
ROW_ALIGN = 16
ROW_SHIFT = 4
KC = 512
KC_SHIFT = 9
assert 1 << ROW_SHIFT == ROW_ALIGN and 1 << KC_SHIFT == KC
N_PHASE = 2
N_BUF = N_PHASE + 1
E_PER_PHASE = N_EXPERTS // N_PHASE
STAGE_MAX = E_PER_PHASE * (TB_C + ROW_ALIGN)
STAGE_ROWS = -(-STAGE_MAX // KC) * KC
N_STEPS_C = BATCH * NB_C


def _combine_kernel(bnd_ref, x_ref, g_ref, y_hbm, o_ref, stage, sem, rows_sm, *, final_norm):
    b = pl.program_id(0)
    tb = pl.program_id(1)
    step = b * NB_C + tb

    def chunk_copy(e, src, buf, dst):
        return pltpu.make_async_copy(y_hbm.at[e, pl.ds(src, ROW_ALIGN), :],
                                     stage.at[buf, pl.ds(dst, ROW_ALIGN), :], sem.at[buf])

    def issue(bq, tq, ph, buf):
        pos = jnp.int32(0)
        for e in range(ph * E_PER_PHASE, (ph + 1) * E_PER_PHASE):
            lo = bnd_ref[bq, tq, e]
            hi = bnd_ref[bq, tq + 1, e]
            lo_al = lo - (lo & (ROW_ALIGN - 1))
            nch = jnp.where(hi > lo, lax.shift_right_logical(hi - lo_al + (ROW_ALIGN - 1), ROW_SHIFT), 0)

            def start(c, carry, e=e, lo_al=lo_al, pos=pos):
                src = pl.multiple_of(bq * CAP + lo_al + c * ROW_ALIGN, ROW_ALIGN)
                dst = pl.multiple_of(pos + c * ROW_ALIGN, ROW_ALIGN)
                chunk_copy(e, src, buf, dst).start()
                return carry

            lax.fori_loop(0, nch, start, 0)
            pos = pos + nch * ROW_ALIGN
        rows_sm[buf] = pos

    @pl.when(step == 0)
    def _first():
        stage[...] = jnp.zeros_like(stage)
        for ph in range(N_PHASE):
            issue(b, tb, ph, ph)

    t0 = (tb * TB_C).astype(F32)
    lane = lax.broadcasted_iota(jnp.int32, (KC, LANE), 1).astype(F32)
    krow = lax.broadcasted_iota(jnp.int32, (KC, LANE), 0)
    wrap = tb + 1 == NB_C
    b_next = jnp.where(wrap, b + 1, b)
    tb_next = jnp.where(wrap, 0, tb + 1)

    for ph in range(N_PHASE):
        seq = step * N_PHASE + ph
        buf = lax.rem(seq, N_BUF)

        @pl.when(step + 1 < N_STEPS_C)
        def _prefetch(ph=ph, seq=seq):
            issue(b_next, tb_next, ph, lax.rem(seq + N_PHASE, N_BUF))

        rows = rows_sm[buf]

        def drain(c, carry, buf=buf):
            chunk_copy(0, 0, buf, 0).wait()
            return carry

        lax.fori_loop(0, lax.shift_right_logical(rows, ROW_SHIFT), drain, 0)

        def scattered(kc, buf=buf, rows=rows):
            k0 = pl.multiple_of(kc * KC, KC)
            blk = stage[buf, pl.ds(k0, KC), :]
            tok = blk[:, D_MODEL:D_MODEL + LANE].astype(F32) * 64.0 + blk[:, D_MODEL + LANE:].astype(F32) - t0
            live = krow + k0 < rows
            hits = [jnp.logical_and(live, tok - float(q * LANE) == lane) for q in range(TB_C // LANE)]
            onehot_t = jnp.where(jnp.concatenate(hits, axis=1), 1.0, 0.0).astype(BF16)
            return lax.dot_general(onehot_t, blk[:, :D_MODEL], (((0,), (0,)), ((), ())), preferred_element_type=F32)

        def accumulate(kc, carry):
            o_ref[...] += scattered(kc)
            return carry

        n_chunks = lax.shift_right_logical(rows + (KC - 1), KC_SHIFT)
        if ph == 0:
            o_ref[...] = x_ref[...] + scattered(0)
            lax.fori_loop(1, n_chunks, accumulate, 0)
        else:
            lax.fori_loop(0, n_chunks, accumulate, 0)

    if final_norm:
        x = o_ref[...]
        inv = lax.rsqrt(jnp.mean(x * x, axis=-1, keepdims=True) + RMS_EPS)
        o_ref[...] = (x * inv) * g_ref[...]


def _combine(bnd, x1_2d, gamma, ysg, final_norm):
    grid_spec = pltpu.PrefetchScalarGridSpec(
        num_scalar_prefetch=1,
        grid=(BATCH, NB_C),
        in_specs=[
            pl.BlockSpec((TB_C, D_MODEL), lambda b, t, bnd: (b * NB_C + t, 0)),
            pl.BlockSpec((1, D_MODEL), lambda b, t, bnd: (0, 0)),
            pl.BlockSpec(memory_space=pl.ANY),
        ],
        out_specs=pl.BlockSpec((TB_C, D_MODEL), lambda b, t, bnd: (b * NB_C + t, 0)),
        scratch_shapes=[
            pltpu.VMEM((N_BUF, STAGE_ROWS, Y_WIDTH), BF16),
            pltpu.SemaphoreType.DMA((N_BUF,)),
            pltpu.SMEM((N_BUF,), jnp.int32),
        ],
    )
    return pl.pallas_call(
        functools.partial(_combine_kernel, final_norm=final_norm),
        out_shape=jax.ShapeDtypeStruct((ROWS, D_MODEL), F32),
        grid_spec=grid_spec,
        compiler_params=_cp(("arbitrary", "arbitrary")),
        name="combine_final" if final_norm else "combine",
    )(bnd, x1_2d, gamma.reshape(1, D_MODEL), ysg)


def kernel(x, norm1_g, w_in, w_fourier, w_pool, pool_scale, w_out, norm2_g, w_router, w_gate, w_up, w_down, final_g):
    wcat = _fold_in_weights(w_in, w_pool, pool_scale)
    mc, ms = _fold_head_weights(w_fourier)
    ctab, stab = _dft_tables()
    w_out_bf = w_out.astype(BF16)
    w_router_t = jnp.swapaxes(w_router, 1, 2)
    wr_hi = w_router_t.astype(BF16)
    wr_lo = (w_router_t - wr_hi.astype(F32)).astype(BF16)
    wr_split = jnp.concatenate([wr_hi, wr_lo], axis=1)
    batch_base = (jnp.arange(BATCH, dtype=jnp.int32) * SEQ)[:, None, None]

    xc = x.reshape(ROWS, D_MODEL)
    for layer in range(DEPTH):
        z = _norm_mm(xc, norm1_g[layer], wcat, layer)
        z3 = z.reshape(BATCH, SEQ, Z_WIDTH)
        ya_lo, ya_hi = _dft(ctab, stab, z3, mc, ms, layer)
        x1, h2, lg = _out_proj(ya_lo, ya_hi, z, w_out_bf, xc,
                               norm2_g[layer], wr_split[layer], wr_hi[layer], layer)
        idx, gates, enc = _select(lg.reshape(N_EXPERTS, BATCH, NGRP, LANE))

        idx_rows = jnp.swapaxes(idx + batch_base, 0, 1).reshape(N_EXPERTS, 1, SLOTS)
        meta = jnp.stack([gates, (idx >> 6).astype(F32), (idx & 63).astype(F32), jnp.zeros_like(gates)], axis=-1)
        meta = jnp.swapaxes(meta, 0, 1).reshape(N_EXPERTS, SLOTS, N_META)
        counts = jnp.abs(enc).reshape(BATCH, N_EXPERTS, SEQ)
        ends = counts[:, :, TB_C - 1::TB_C]
        bnd = jnp.concatenate([jnp.zeros((BATCH, N_EXPERTS, 1), jnp.int32), ends], axis=2)
        bnd = jnp.swapaxes(bnd, 1, 2)

        ysg = _ffn(idx_rows, meta, h2, w_gate, w_up, w_down, layer)
        xc = _combine(bnd, x1, final_g, ysg, layer == DEPTH - 1)
    return xc.reshape(BATCH, SEQ, D_MODEL)
```
